```python
import jax, jax.numpy as jnp
from jax import lax
import numpy as np

D_MODEL = 2048
BATCH = 4
SEQ = 4096
DEPTH = 4

N_MIXERS = 2
D_FF = 5632
SWA_HEAD_DIM = 64
SWA_HEADS = D_MODEL // SWA_HEAD_DIM
SWA_KV_HEADS = SWA_HEADS // 8
SWA_WINDOW = 128
SWA_IN = (SWA_HEADS + 2 * SWA_KV_HEADS) * SWA_HEAD_DIM
NSA_HEAD_DIM = 128
NSA_HEADS = D_MODEL // NSA_HEAD_DIM
NSA_KV_HEADS = 4
CMP_BLOCK = 32
CMP_STRIDE = 16
CMP_HIDDEN = 2 * NSA_HEAD_DIM
SLC_BLOCK = 64
SLC_TOP_N = 16
NSA_WINDOW = 512
NSA_IN = NSA_HEADS * NSA_HEAD_DIM + 6 * NSA_KV_HEADS * NSA_HEAD_DIM + 3 * NSA_HEADS
ATTN_BLOCK = 128
SEL_CHUNK = 32
ROPE_THETA = 500000.0
ROPE_FRACTION = 4
NORM_EPS = 1e-6
NEG_INF = -1e30
FORCE_SCORE = 1e9
TINY = 1e-30
MAX_START_POS = 1024

kernel_name = "hybrid_swa_sink_nsa_macaron"


def rms_norm(x, g):
    xf = x.astype(jnp.float32)
    y = xf * lax.rsqrt(jnp.mean(xf * xf, axis=-1, keepdims=True) + NORM_EPS)
    return (y * g).astype(x.dtype)


def partial_rope(x, pos):
    d = x.shape[-1]
    rd = d // ROPE_FRACTION
    half = rd // 2
    inv = 1.0 / (ROPE_THETA ** (jnp.arange(half, dtype=jnp.float32) * (2.0 / rd)))
    ang = pos.astype(jnp.float32)[..., None] * inv
    ang = ang.reshape(ang.shape[:2] + (1,) * (x.ndim - 3) + (half,))
    cos, sin = jnp.cos(ang), jnp.sin(ang)
    x1, x2, rest = x[..., :half], x[..., half:rd], x[..., rd:]
    rot = jnp.concatenate([x1 * cos - x2 * sin, x2 * cos + x1 * sin], axis=-1).astype(x.dtype)
    return jnp.concatenate([rot, rest], axis=-1)


def swiglu(x, w_in, w_out):
    gate, up = jnp.split(x @ w_in, 2, axis=-1)
    return (jax.nn.silu(gate) * up) @ w_out


def banded_attention(q, k, v, window, sinks=None):
    B, T, G, R, d = q.shape
    nb = T // ATTN_BLOCK
    P = window // ATTN_BLOCK
    qb = q.reshape(B, nb, ATTN_BLOCK, G, R, d)
    pad = ((0, 0), (P, 0), (0, 0), (0, 0), (0, 0))
    kb = jnp.pad(k.reshape(B, nb, ATTN_BLOCK, G, d), pad)
    vb = jnp.pad(v.reshape(B, nb, ATTN_BLOCK, G, d), pad)
    kw = jnp.concatenate([kb[:, s:s + nb] for s in range(P + 1)], axis=2)
    vw = jnp.concatenate([vb[:, s:s + nb] for s in range(P + 1)], axis=2)
    s = jnp.einsum('bnqgrd,bnkgd->bgrnqk', qb, kw).astype(jnp.float32) * (d ** -0.5)
    qi = jnp.arange(ATTN_BLOCK)[:, None]
    ki = jnp.arange((P + 1) * ATTN_BLOCK)[None, :]
    rel = qi + P * ATTN_BLOCK - ki
    kpos = jnp.arange(nb)[:, None, None] * ATTN_BLOCK - P * ATTN_BLOCK + ki
    mask = (rel >= 0) & (rel < window) & (kpos >= 0)
    s = jnp.where(mask, s, NEG_INF)
    if sinks is None:
        p = jax.nn.softmax(s, axis=-1)
    else:
        sink_col = jnp.broadcast_to(sinks.astype(jnp.float32)[:, :, None, None, None], s.shape[:-1] + (1,))
        p = jax.nn.softmax(jnp.concatenate([s, sink_col], axis=-1), axis=-1)[..., :-1]
    o = jnp.einsum('bgrnqk,bnkgd->bnqgrd', p.astype(v.dtype), vw)
    return o.reshape(B, T, G, R, d)


def swa_mixer(x, positions, w_in, q_norm, k_norm, sinks, w_out):
    B, T, _ = x.shape
    G, d = SWA_KV_HEADS, SWA_HEAD_DIM
    R = SWA_HEADS // G
    q, k, v = jnp.split(x @ w_in, [SWA_HEADS * d, (SWA_HEADS + G) * d], axis=-1)
    q = partial_rope(rms_norm(q.reshape(B, T, G, R, d), q_norm), positions)
    k = partial_rope(rms_norm(k.reshape(B, T, G, d), k_norm), positions)
    o = banded_attention(q, k, v.reshape(B, T, G, d), SWA_WINDOW, sinks.reshape(G, R))
    return o.reshape(B, T, SWA_HEADS * d) @ w_out


def compress(t, pe, w1, w2, idx):
    blocks = t[:, idx] + pe[:, None, :]
    hdn = jax.nn.silu(jnp.einsum('bnlgd,ldh->bngh', blocks, w1))
    return jnp.einsum('bngh,hd->bngd', hdn, w2)


def compressed_attention(q, k, v, positions, k_norm, pe_k, w1_k, w2_k, pe_v, w1_v, w2_v):
    B, T, G, R, d = q.shape
    n_cmp = (T - CMP_BLOCK) // CMP_STRIDE + 1
    starts = jnp.arange(n_cmp) * CMP_STRIDE
    idx = starts[:, None] + jnp.arange(CMP_BLOCK)[None, :]
    end_idx = starts + CMP_BLOCK - 1
    k_c = partial_rope(rms_norm(compress(k, pe_k, w1_k, w2_k, idx), k_norm), positions[:, end_idx])
    v_c = compress(v, pe_v, w1_v, w2_v, idx)
    s = jnp.einsum('btgrd,bngd->bgrtn', q, k_c).astype(jnp.float32) * (d ** -0.5)
    mask = end_idx[None, :] <= jnp.arange(T)[:, None]
    s = jnp.where(mask, s, NEG_INF)
    e = jnp.exp(s - jnp.max(s, axis=-1, keepdims=True)) * mask
    p = e / jnp.maximum(jnp.sum(e, axis=-1, keepdims=True), TINY)
    o = jnp.einsum('bgrtn,bngd->btgrd', p.astype(v.dtype), v_c)
    return o, p


def select_blocks(p_cmp, T):
    n_cmp = p_cmp.shape[-1]
    n_slc = T // SLC_BLOCK
    cs = jnp.arange(n_cmp)[:, None] * CMP_STRIDE
    ss = jnp.arange(n_slc)[None, :] * SLC_BLOCK
    overlap = ((cs < ss + SLC_BLOCK) & (cs + CMP_BLOCK > ss)).astype(jnp.float32)
    imp = jnp.einsum('bgrtn,nj->bgtj', p_cmp, overlap)
    t = jnp.arange(T)[:, None]
    j = jnp.arange(n_slc)[None, :]
    cur = t // SLC_BLOCK
    forced = (j == 0) | (j == cur) | (j == cur - 1)
    imp = jnp.where(forced, FORCE_SCORE, jnp.where(j * SLC_BLOCK <= t, imp, NEG_INF))
    _, idx = lax.top_k(imp, min(SLC_TOP_N, n_slc))
    return idx.transpose(0, 2, 1, 3)


def selected_attention(q, k, v, blk_idx):
    B, T, G, R, d = q.shape
    n_slc = T // SLC_BLOCK
    n_top = blk_idx.shape[-1]
    kb = k.reshape(B, n_slc, SLC_BLOCK, G, d).transpose(0, 3, 1, 2, 4)
    vb = v.reshape(B, n_slc, SLC_BLOCK, G, d).transpose(0, 3, 1, 2, 4)
    nc = T // SEL_CHUNK
    qc = q.reshape(B, nc, SEL_CHUNK, G, R, d).transpose(1, 0, 2, 3, 4, 5)
    ic = blk_idx.reshape(B, nc, SEL_CHUNK, G, n_top).transpose(1, 0, 2, 3, 4)
    tc = jnp.arange(T).reshape(nc, SEL_CHUNK)
    b_ix = jnp.arange(B)[:, None, None, None]
    g_ix = jnp.arange(G)[None, None, :, None]

    def chunk(args):
        q_, i_, t_ = args
        k_sel = kb[b_ix, g_ix, i_]
        v_sel = vb[b_ix, g_ix, i_]
        s = jnp.einsum('bcgrd,bcgnld->bcgrnl', q_, k_sel).astype(jnp.float32) * (d ** -0.5)
        kpos = i_[..., None] * SLC_BLOCK + jnp.arange(SLC_BLOCK)
        mask = (kpos <= t_[None, :, None, None, None])[:, :, :, None]
        s = jnp.where(mask, s, NEG_INF)
        p = jax.nn.softmax(s.reshape(s.shape[:4] + (n_top * SLC_BLOCK,)), axis=-1).reshape(s.shape)
        return jnp.einsum('bcgrnl,bcgnld->bcgrd', p.astype(v.dtype), v_sel)

    o = lax.map(chunk, (qc, ic, tc))
    return o.transpose(1, 0, 2, 3, 4, 5).reshape(B, T, G, R, d)


def nsa_mixer(x, positions, w_in, q_norm, k_norm, pe_k, w1_k, w2_k, pe_v, w1_v, w2_v, w_out):
    B, T, _ = x.shape
    H, G, d = NSA_HEADS, NSA_KV_HEADS, NSA_HEAD_DIM
    R = H // G
    splits = np.cumsum([H * d] + [G * d] * 6).tolist()
    q, kc, vc, ks, vs, kw, vw, gates = jnp.split(x @ w_in, splits, axis=-1)
    q = partial_rope(rms_norm(q.reshape(B, T, G, R, d), q_norm), positions)
    kc, vc, ks, vs, kw, vw = [t.reshape(B, T, G, d) for t in (kc, vc, ks, vs, kw, vw)]
    o_cmp, p_cmp = compressed_attention(q, kc, vc, positions, k_norm, pe_k, w1_k, w2_k, pe_v, w1_v, w2_v)
    ks = partial_rope(rms_norm(ks, k_norm), positions)
    o_slc = selected_attention(q, ks, vs, select_blocks(p_cmp, T))
    kw = partial_rope(rms_norm(kw, k_norm), positions)
    o_win = banded_attention(q, kw, vw, NSA_WINDOW)
    g = jax.nn.sigmoid(gates.reshape(B, T, 3, G, R))[..., None]
    o = g[:, :, 0] * o_cmp + g[:, :, 1] * o_slc + g[:, :, 2] * o_win
    return o.reshape(B, T, H * d) @ w_out


def setup_inputs(seed: int = 0) -> dict:
    key = jax.random.key(seed)
    k = jax.random.split(key, 24)
    n_swa = len(range(0, DEPTH, N_MIXERS))
    n_nsa = len(range(1, DEPTH, N_MIXERS))

    def dense(kk, shape, fan_in):
        return jax.random.normal(kk, shape, jnp.float32) * (fan_in ** -0.5)

    def gain(kk, shape):
        return 1.0 + 0.02 * jax.random.normal(kk, shape, jnp.float32)

    hd = NSA_HEAD_DIM
    x = jax.random.normal(k[0], (BATCH, SEQ, D_MODEL), jnp.float32)
    positions = (jax.random.randint(k[1], (BATCH, 1), 0, MAX_START_POS, dtype=jnp.int32)
                 + jnp.arange(SEQ, dtype=jnp.int32)[None, :])
    return {
        'x': x,
        'positions': positions,
        'ffn1_norm': gain(k[2], (DEPTH, D_MODEL)),
        'ffn1_w_in': dense(k[3], (DEPTH, D_MODEL, 2 * D_FF), D_MODEL),
        'ffn1_w_out': dense(k[4], (DEPTH, D_FF, D_MODEL), D_FF),
        'mix_norm': gain(k[5], (DEPTH, D_MODEL)),
        'ffn2_norm': gain(k[6], (DEPTH, D_MODEL)),
        'ffn2_w_in': dense(k[7], (DEPTH, D_MODEL, 2 * D_FF), D_MODEL),
        'ffn2_w_out': dense(k[8], (DEPTH, D_FF, D_MODEL), D_FF),
        'swa_w_in': dense(k[9], (n_swa, D_MODEL, SWA_IN), D_MODEL),
        'swa_q_norm': gain(k[10], (n_swa, SWA_HEAD_DIM)),
        'swa_k_norm': gain(k[11], (n_swa, SWA_HEAD_DIM)),
        'swa_sinks': jax.random.normal(k[12], (n_swa, SWA_HEADS), jnp.float32),
        'swa_w_out': dense(k[13], (n_swa, SWA_HEADS * SWA_HEAD_DIM, D_MODEL), SWA_HEADS * SWA_HEAD_DIM),
        'nsa_w_in': dense(k[14], (n_nsa, D_MODEL, NSA_IN), D_MODEL),
        'nsa_q_norm': gain(k[15], (n_nsa, hd)),
        'nsa_k_norm': gain(k[16], (n_nsa, hd)),
        'nsa_cmp_pe_k': 0.1 * jax.random.normal(k[17], (n_nsa, CMP_BLOCK, hd), jnp.float32),
        'nsa_cmp_w1_k': dense(k[18], (n_nsa, CMP_BLOCK, hd, CMP_HIDDEN), CMP_BLOCK * hd),
        'nsa_cmp_w2_k': dense(k[19], (n_nsa, CMP_HIDDEN, hd), CMP_HIDDEN),
        'nsa_cmp_pe_v': 0.1 * jax.random.normal(k[20], (n_nsa, CMP_BLOCK, hd), jnp.float32),
        'nsa_cmp_w1_v': dense(k[21], (n_nsa, CMP_BLOCK, hd, CMP_HIDDEN), CMP_BLOCK * hd),
        'nsa_cmp_w2_v': dense(k[22], (n_nsa, CMP_HIDDEN, hd), CMP_HIDDEN),
        'nsa_w_out': dense(k[23], (n_nsa, NSA_HEADS * hd, D_MODEL), NSA_HEADS * hd),
    }


def reference(x, positions, ffn1_norm, ffn1_w_in, ffn1_w_out, mix_norm, ffn2_norm, ffn2_w_in, ffn2_w_out,
              swa_w_in, swa_q_norm, swa_k_norm, swa_sinks, swa_w_out,
              nsa_w_in, nsa_q_norm, nsa_k_norm, nsa_cmp_pe_k, nsa_cmp_w1_k, nsa_cmp_w2_k,
              nsa_cmp_pe_v, nsa_cmp_w1_v, nsa_cmp_w2_v, nsa_w_out):
    h = x
    for i in range(DEPTH):
        h = h + 0.5 * swiglu(rms_norm(h, ffn1_norm[i]), ffn1_w_in[i], ffn1_w_out[i])
        hn = rms_norm(h, mix_norm[i])
        j = i // N_MIXERS
        if i % N_MIXERS == 0:
            h = h + swa_mixer(hn, positions, swa_w_in[j], swa_q_norm[j], swa_k_norm[j], swa_sinks[j], swa_w_out[j])
        else:
            h = h + nsa_mixer(hn, positions, nsa_w_in[j], nsa_q_norm[j], nsa_k_norm[j],
                              nsa_cmp_pe_k[j], nsa_cmp_w1_k[j], nsa_cmp_w2_k[j],
                              nsa_cmp_pe_v[j], nsa_cmp_w1_v[j], nsa_cmp_w2_v[j], nsa_w_out[j])
        h = h + 0.5 * swiglu(rms_norm(h, ffn2_norm[i]), ffn2_w_in[i], ffn2_w_out[i])
    return h
```

```python
import functools

import jax
import jax.numpy as jnp
import numpy as np
from jax import lax
from jax.experimental import pallas as pl
from jax.experimental.pallas import tpu as pltpu

F32 = jnp.float32
BF16 = jnp.bfloat16

N_MIXERS = 2
SWA_HEAD_DIM = 64
SWA_KV_HEADS = 4
SWA_WINDOW = 128
NSA_HEAD_DIM = 128
NSA_KV_HEADS = 4
CMP_BLOCK = 32
CMP_STRIDE = 16
SLC_BLOCK = 64
SLC_TOP_N = 16
NSA_WINDOW = 512
ROPE_THETA = 500000.0
ROPE_FRACTION = 4
NORM_EPS = 1e-6
NEG_INF = -1e30
FORCE_SCORE = 1e9
TINY = 1e-30

LANES = 128
VMEM_LIMIT = 56 * 1024 * 1024

ROW_TILE = 512
FFN_TILE = 512
ATTN_TILE = 128
SEL_KEY_TILE = 256


def _params(*sem):
    return pltpu.CompilerParams(dimension_semantics=sem, vmem_limit_bytes=VMEM_LIMIT)


def _dot(a, b):
    return jnp.dot(a, b, preferred_element_type=F32)


def _dot_nt(a, b):
    return lax.dot_general(a, b, (((1,), (1,)), ((), ())), preferred_element_type=F32)


def _rms(x, g):
    ms = jnp.mean(x * x, axis=-1, keepdims=True)
    return x * lax.rsqrt(ms + NORM_EPS) * g


def _sigmoid(x):
    return 1.0 / (1.0 + jnp.exp(-x))


def _rope(xn, rope, half):
    c, s1, s2 = rope[:, 0:LANES], rope[:, LANES:2 * LANES], rope[:, 2 * LANES:3 * LANES]
    return xn * c + pltpu.roll(xn, LANES - half, 1) * s1 + pltpu.roll(xn, half, 1) * s2


def _ffn_kernel(x_ref, g_ref, wg_ref, wu_ref, wo_ref, o_ref, xn_ref, acc_ref):
    j = pl.program_id(1)

    @pl.when(j == 0)
    def _():
        xn_ref[...] = _rms(x_ref[...], g_ref[...]).astype(BF16)
        acc_ref[...] = jnp.zeros_like(acc_ref)

    xn = xn_ref[...]
    gate = _dot(xn, wg_ref[...])
    up = _dot(xn, wu_ref[...])
    hdn = (gate * _sigmoid(gate) * up).astype(BF16)
    acc_ref[...] += _dot(hdn, wo_ref[...])

    @pl.when(j == pl.num_programs(1) - 1)
    def _():
        o_ref[...] = x_ref[...] + 0.5 * acc_ref[...]


def _ffn(h, norm, w_in, w_out, layer):
    m, d = h.shape
    f = w_out.shape[1]
    nf = f // FFN_TILE
    return pl.pallas_call(
        _ffn_kernel,
        grid=(m // ROW_TILE, nf),
        in_specs=[
            pl.BlockSpec((ROW_TILE, d), lambda i, j: (i, 0)),
            pl.BlockSpec((None, 1, d), lambda i, j: (layer, 0, 0)),
            pl.BlockSpec((None, d, FFN_TILE), lambda i, j: (layer, 0, j)),
            pl.BlockSpec((None, d, FFN_TILE), lambda i, j: (layer, 0, nf + j)),
            pl.BlockSpec((None, FFN_TILE, d), lambda i, j: (layer, j, 0)),
        ],
        out_specs=pl.BlockSpec((ROW_TILE, d), lambda i, j: (i, 0)),
        out_shape=jax.ShapeDtypeStruct((m, d), F32),
        scratch_shapes=[pltpu.VMEM((ROW_TILE, d), BF16), pltpu.VMEM((ROW_TILE, d), F32)],
        compiler_params=_params("parallel", "arbitrary"),
        name="ffn",
    )(h, norm, w_in, w_in, w_out)


def _proj_kernel(x_ref, g_ref, w_ref, o_ref, xn_ref):
    @pl.when(pl.program_id(1) == 0)
    def _():
        xn_ref[...] = _rms(x_ref[...], g_ref[...]).astype(BF16)

    o_ref[...] = _dot(xn_ref[...], w_ref[...])


def _proj(h, norm, w, layer, wl, tn):
    m, d = h.shape
    n = w.shape[2]
    return pl.pallas_call(
        _proj_kernel,
        grid=(m // ROW_TILE, n // tn),
        in_specs=[
            pl.BlockSpec((ROW_TILE, d), lambda i, j: (i, 0)),
            pl.BlockSpec((None, 1, d), lambda i, j: (layer, 0, 0)),
            pl.BlockSpec((None, d, tn), lambda i, j: (wl, 0, j)),
        ],
        out_specs=pl.BlockSpec((ROW_TILE, tn), lambda i, j: (i, j)),
        out_shape=jax.ShapeDtypeStruct((m, n), F32),
        scratch_shapes=[pltpu.VMEM((ROW_TILE, d), BF16)],
        compiler_params=_params("parallel", "arbitrary"),
        name="mixer_in_proj",
    )(h, norm, w)


def _oproj_kernel(h_ref, o_ref, w_ref, out_ref):
    out_ref[...] = h_ref[...] + _dot(o_ref[...], w_ref[...])


def _oproj(h, o, w, wl):
    m, d = h.shape
    k = o.shape[1]
    return pl.pallas_call(
        _oproj_kernel,
        grid=(m // ROW_TILE,),
        in_specs=[
            pl.BlockSpec((ROW_TILE, d), lambda i: (i, 0)),
            pl.BlockSpec((ROW_TILE, k), lambda i: (i, 0)),
            pl.BlockSpec((None, k, d), lambda i: (wl, 0, 0)),
        ],
        out_specs=pl.BlockSpec((ROW_TILE, d), lambda i: (i, 0)),
        out_shape=jax.ShapeDtypeStruct((m, d), F32),
        compiler_params=_params("parallel"),
        name="mixer_out_proj",
    )(h, o, w)


def _swa_kernel(sink_ref, q_ref, kvc_ref, kvp_ref, rc_ref, rp_ref, qn_ref, kn_ref, o_ref):
    i = pl.program_id(1)
    tq = q_ref.shape[0]
    hd = SWA_HEAD_DIM
    half = hd // ROPE_FRACTION // 2
    n_chunks = q_ref.shape[1] // LANES
    heads_per_group = (2 * n_chunks) // SWA_KV_HEADS

    row = lax.broadcasted_iota(jnp.int32, (LANES, LANES), 0)
    col = lax.broadcasted_iota(jnp.int32, (LANES, LANES), 1)
    seg = jnp.where((row // hd) == (col // hd), 1.0 / hd, 0.0).astype(BF16)

    def norm_rope(x, gn, rope):
        xx = x * x
        hi = xx.astype(BF16)
        lo = (xx - hi.astype(F32)).astype(BF16)
        ms = _dot(hi, seg) + _dot(lo, seg)
        return _rope(x * lax.rsqrt(ms + NORM_EPS) * gn, rope, half)

    lane = lax.broadcasted_iota(jnp.int32, (2 * tq, LANES), 1)
    low = lane < hd
    kn = kn_ref[...]
    n_kv_chunks = SWA_KV_HEADS * hd // LANES
    k_chunks, v_chunks = [], []
    for c in range(n_kv_chunks):
        sl = slice(c * LANES, (c + 1) * LANES)
        kp = norm_rope(kvp_ref[:, sl], kn, rp_ref[...])
        kc = norm_rope(kvc_ref[:, sl], kn, rc_ref[...])
        k_chunks.append(jnp.concatenate([kp, kc], axis=0))
        vsl = slice((n_kv_chunks + c) * LANES, (n_kv_chunks + c + 1) * LANES)
        v_chunks.append(jnp.concatenate([kvp_ref[:, vsl], kvc_ref[:, vsl]], axis=0))

    kd, vlo, vhi = [], [], []
    for g in range(SWA_KV_HEADS):
        kc_ = k_chunks[g // 2]
        vc_ = v_chunks[g // 2]
        kc_sw = pltpu.roll(kc_, hd, 1)
        vc_sw = pltpu.roll(vc_, hd, 1)
        if g % 2 == 0:
            kd.append(jnp.where(low, kc_, kc_sw).astype(BF16))
            vlo.append(jnp.where(low, vc_, 0.0).astype(BF16))
            vhi.append(jnp.where(low, 0.0, vc_sw).astype(BF16))
        else:
            kd.append(jnp.where(low, kc_sw, kc_).astype(BF16))
            vlo.append(jnp.where(low, vc_sw, 0.0).astype(BF16))
            vhi.append(jnp.where(low, 0.0, vc_).astype(BF16))

    qi = lax.broadcasted_iota(jnp.int32, (tq, 2 * tq), 0)
    ki = lax.broadcasted_iota(jnp.int32, (tq, 2 * tq), 1)
    rel = qi + tq - ki
    mask = (rel >= 0) & (rel < SWA_WINDOW) & ((ki >= tq) | (i > 0))

    qlane = lax.broadcasted_iota(jnp.int32, (tq, LANES), 1)
    qlow = qlane < hd
    qn = qn_ref[...]
    scale = hd ** -0.5
    for c in range(n_chunks):
        g = (2 * c) // heads_per_group
        xq = norm_rope(q_ref[:, c * LANES:(c + 1) * LANES], qn, rc_ref[...])
        ps = []
        for par in range(2):
            qh = jnp.where(qlow if par == 0 else ~qlow, xq, 0.0).astype(BF16)
            s = _dot_nt(qh, kd[g]) * scale
            s = jnp.where(mask, s, NEG_INF)
            sink = sink_ref[2 * c + par]
            mx = jnp.maximum(jnp.max(s, axis=-1, keepdims=True), sink)
            e = jnp.exp(s - mx)
            den = jnp.sum(e, axis=-1, keepdims=True) + jnp.exp(sink - mx)
            ps.append((e / den).astype(BF16))
        o = _dot(ps[0], vlo[g]) + _dot(ps[1], vhi[g])
        o_ref[:, c * LANES:(c + 1) * LANES] = o.astype(o_ref.dtype)


def _swa_attention(p, rope, q_norm2, k_norm2, sinks, batch, seq):
    m = p.shape[0]
    tq = ATTN_TILE
    nb = seq // tq
    n_q = sinks.shape[0] * SWA_HEAD_DIM
    kv_w = 2 * SWA_KV_HEADS * SWA_HEAD_DIM
    kv_blk = n_q // kv_w
    return pl.pallas_call(
        _swa_kernel,
        grid=(batch, nb),
        in_specs=[
            pl.BlockSpec(memory_space=pltpu.SMEM),
            pl.BlockSpec((tq, n_q), lambda b, i: (b * nb + i, 0)),
            pl.BlockSpec((tq, kv_w), lambda b, i: (b * nb + i, kv_blk)),
            pl.BlockSpec((tq, kv_w), lambda b, i: (b * nb + jnp.maximum(i - 1, 0), kv_blk)),
            pl.BlockSpec((tq, 3 * LANES), lambda b, i: (b * nb + i, 0)),
            pl.BlockSpec((tq, 3 * LANES), lambda b, i: (b * nb + jnp.maximum(i - 1, 0), 0)),
            pl.BlockSpec((1, LANES), lambda b, i: (0, 0)),
            pl.BlockSpec((1, LANES), lambda b, i: (0, 0)),
        ],
        out_specs=pl.BlockSpec((tq, n_q), lambda b, i: (b * nb + i, 0)),
        out_shape=jax.ShapeDtypeStruct((m, n_q), BF16),
        compiler_params=_params("parallel", "arbitrary"),
        name="swa_attention",
    )(sinks, p, p, p, rope, rope, q_norm2, k_norm2)


def _nsa_prep_kernel(kc_ref, vc_ref, ks_ref, vs_ref, kw_ref, vw_ref, rope_ref, ropec_ref, kn_ref,
                     pek_ref, w1k_ref, w2k_ref, pev_ref, w1v_ref, w2v_ref,
                     kco_ref, vco_ref, kso_ref, vso_ref, kwo_ref, vwo_ref):
    half = NSA_HEAD_DIM // ROPE_FRACTION // 2
    kn = kn_ref[...]
    n_chunk = kc_ref.shape[0] // CMP_STRIDE
    per = CMP_BLOCK // CMP_STRIDE

    def compress(t_ref, pe_ref, w1_ref, w2_ref):
        parts = []
        for h in range(per):
            acc = jnp.zeros((n_chunk, w1_ref.shape[2]), F32)
            for l in range(CMP_STRIDE):
                ll = h * CMP_STRIDE + l
                xl = t_ref[pl.ds(l, n_chunk, stride=CMP_STRIDE), :] + pe_ref[ll:ll + 1, :]
                acc = acc + _dot(xl.astype(BF16), w1_ref[ll])
            parts.append(acc)
        pre = parts[0]
        for h in range(1, per):
            pre = pre + pltpu.roll(parts[h], n_chunk - h, 0)
        hdn = pre * _sigmoid(pre)
        return _dot(hdn.astype(BF16), w2_ref[...])

    kcmp = compress(kc_ref, pek_ref, w1k_ref, w2k_ref)
    kco_ref[...] = _rope(_rms(kcmp, kn), ropec_ref[...], half).astype(BF16)
    vco_ref[...] = compress(vc_ref, pev_ref, w1v_ref, w2v_ref).astype(BF16)
    kso_ref[...] = _rope(_rms(ks_ref[...], kn), rope_ref[...], half).astype(BF16)
    kwo_ref[...] = _rope(_rms(kw_ref[...], kn), rope_ref[...], half).astype(BF16)
    vso_ref[...] = vs_ref[...].astype(BF16)
    vwo_ref[...] = vw_ref[...].astype(BF16)


def _nsa_prep(p, rope, rope_c, k_norm, pe_k, w1_k, w2_k, pe_v, w1_v, w2_v, batch, seq):
    d = NSA_HEAD_DIM
    g_n = NSA_KV_HEADS
    n_chunk = seq // CMP_STRIDE
    q_blocks = 4 * g_n

    def col(kind):
        return pl.BlockSpec((seq, d), lambda b, g, kind=kind: (b, q_blocks + kind * g_n + g))

    def full(a):
        return pl.BlockSpec(a.shape, lambda b, g, nd=a.ndim: (0,) * nd)

    out_c = pl.BlockSpec((None, None, n_chunk, d), lambda b, g: (b, g, 0, 0))
    out_t = pl.BlockSpec((None, None, seq, d), lambda b, g: (b, g, 0, 0))
    sh_c = jax.ShapeDtypeStruct((batch, g_n, n_chunk, d), BF16)
    sh_t = jax.ShapeDtypeStruct((batch, g_n, seq, d), BF16)
    return pl.pallas_call(
        _nsa_prep_kernel,
        grid=(batch, g_n),
        in_specs=[col(0), col(1), col(2), col(3), col(4), col(5),
                  pl.BlockSpec((seq, 3 * LANES), lambda b, g: (b, 0)),
                  pl.BlockSpec((n_chunk, 3 * LANES), lambda b, g: (b, 0)),
                  full(k_norm), full(pe_k), full(w1_k), full(w2_k), full(pe_v), full(w1_v), full(w2_v)],
        out_specs=[out_c, out_c, out_t, out_t, out_t, out_t],
        out_shape=[sh_c, sh_c, sh_t, sh_t, sh_t, sh_t],
        compiler_params=_params("parallel", "arbitrary"),
        name="nsa_prep",
    )(p, p, p, p, p, p, rope, rope_c, k_norm, pe_k, w1_k, w2_k, pe_v, w1_v, w2_v)


def _nsa_attn_kernel(q_ref, gate_ref, rope_ref, qn_ref, kc_ref, vc_ref, ks_ref, vs_ref, kw_ref, vw_ref,
                     ovl_ref, exp_ref, o_ref, q_scr, m_scr, l_scr, acc_scr, mask_scr):
    g = pl.program_id(1)
    i = pl.program_id(2)
    tq = q_ref.shape[0]
    d = NSA_HEAD_DIM
    rep = q_ref.shape[1] // d
    half = d // ROPE_FRACTION // 2
    seq = ks_ref.shape[0]
    n_cmp = kc_ref.shape[0]
    n_slc = ovl_ref.shape[0]
    tk = SEL_KEY_TILE
    scale = d ** -0.5
    q0 = i * tq

    for r in range(rep):
        x = q_ref[:, r * d:(r + 1) * d]
        q_scr[r * tq:(r + 1) * tq, :] = _rope(_rms(x, qn_ref[...]), rope_ref[...], half).astype(BF16)
    q = q_scr[...]

    s = (_dot_nt(q, kc_ref[...]) * scale).reshape(rep, tq, n_cmp)
    t_c = q0 + lax.broadcasted_iota(jnp.int32, (tq, n_cmp), 0)
    end_c = lax.broadcasted_iota(jnp.int32, (tq, n_cmp), 1) * CMP_STRIDE + (CMP_BLOCK - 1)
    mask_c = (end_c <= t_c)[None]
    s = jnp.where(mask_c, s, NEG_INF)
    e = jnp.where(mask_c, jnp.exp(s - jnp.max(s, axis=-1, keepdims=True)), 0.0)
    p = e / jnp.maximum(jnp.sum(e, axis=-1, keepdims=True), TINY)
    pb = p.astype(BF16).reshape(rep * tq, n_cmp)
    o_cmp = _dot(pb, vc_ref[...])

    imp = jnp.zeros((n_slc, tq), F32)
    for r in range(rep):
        imp = imp + _dot_nt(ovl_ref[...], pb[r * tq:(r + 1) * tq, :])
    jb = lax.broadcasted_iota(jnp.int32, (n_slc, tq), 0)
    tt = q0 + lax.broadcasted_iota(jnp.int32, (n_slc, tq), 1)
    cur = tt // SLC_BLOCK
    forced = (jb == 0) | (jb == cur) | (jb == cur - 1)
    score = jnp.where(forced, FORCE_SCORE, jnp.where(jb * SLC_BLOCK <= tt, imp, NEG_INF))
    cnt = jnp.zeros((n_slc, tq), F32)
    for ii in range(n_slc):
        rowv = score[ii:ii + 1, :]
        cnt = cnt + jnp.where(jb > ii, jnp.where(rowv >= score, 1.0, 0.0), jnp.where(rowv > score, 1.0, 0.0))
    sel_t = jnp.where(cnt < min(SLC_TOP_N, n_slc), 1.0, 0.0).astype(BF16)
    er = lax.broadcasted_iota(jnp.int32, (tq, tq), 0)
    ec = lax.broadcasted_iota(jnp.int32, (tq, tq), 1)
    eye = jnp.where(er == ec, 1.0, 0.0).astype(BF16)
    sel = _dot_nt(eye, sel_t).astype(BF16)
    mask_scr[...] = _dot(sel, exp_ref[...])

    m_scr[...] = jnp.full(m_scr.shape, NEG_INF, F32)
    l_scr[...] = jnp.zeros(l_scr.shape, F32)
    acc_scr[...] = jnp.zeros(acc_scr.shape, F32)
    t_k = q0 + lax.broadcasted_iota(jnp.int32, (tq, tk), 0)
    c_k = lax.broadcasted_iota(jnp.int32, (tq, tk), 1)

    def sel_step(kt, carry):
        k0 = pl.multiple_of(kt * tk, tk)
        k = ks_ref[pl.ds(k0, tk), :]
        v = vs_ref[pl.ds(k0, tk), :]
        sk = (_dot_nt(q, k) * scale).reshape(rep, tq, tk)
        mk = (mask_scr[:, pl.ds(k0, tk)] > 0.5) & (c_k + k0 <= t_k)
        sk = jnp.where(mk[None], sk, NEG_INF).reshape(rep * tq, tk)
        m_prev = m_scr[...]
        m_new = jnp.maximum(m_prev, jnp.max(sk, axis=-1, keepdims=True))
        alpha = jnp.exp(m_prev - m_new)
        pk = jnp.exp(sk - m_new)
        l_scr[...] = alpha * l_scr[...] + jnp.sum(pk, axis=-1, keepdims=True)
        acc_scr[...] = alpha * acc_scr[...] + _dot(pk.astype(BF16), v)
        m_scr[...] = m_new
        return carry

    lax.fori_loop(0, (q0 + tq + tk - 1) // tk, sel_step, 0)
    o_slc = acc_scr[...] / l_scr[...]

    wk = NSA_WINDOW + tq
    start = pl.multiple_of(jnp.maximum(q0 - NSA_WINDOW, 0), tq)
    kw = kw_ref[pl.ds(start, wk), :]
    vw = vw_ref[pl.ds(start, wk), :]
    sw = (_dot_nt(q, kw) * scale).reshape(rep, tq, wk)
    t_w = q0 + lax.broadcasted_iota(jnp.int32, (tq, wk), 0)
    k_w = start + lax.broadcasted_iota(jnp.int32, (tq, wk), 1)
    mw = ((k_w <= t_w) & (t_w - k_w < NSA_WINDOW))[None]
    sw = jnp.where(mw, sw, NEG_INF)
    ew = jnp.exp(sw - jnp.max(sw, axis=-1, keepdims=True))
    pw = (ew / jnp.sum(ew, axis=-1, keepdims=True)).astype(BF16).reshape(rep * tq, wk)
    o_win = _dot(pw, vw)

    sg = _sigmoid(gate_ref[...])
    glane = lax.broadcasted_iota(jnp.int32, sg.shape, 1)
    n_heads = NSA_KV_HEADS * rep
    for r in range(rep):
        rows = slice(r * tq, (r + 1) * tq)
        out = jnp.zeros((tq, d), F32)
        for c, branch in enumerate((o_cmp, o_slc, o_win)):
            idx = c * n_heads + g * rep + r
            gcol = jnp.sum(jnp.where(glane == idx, sg, 0.0), axis=-1, keepdims=True)
            out = out + gcol * branch[rows, :]
        o_ref[:, r * d:(r + 1) * d] = out.astype(o_ref.dtype)


def _nsa_attention(p, rope, q_norm, kc, vc, ks, vs, kw, vw, ovl_t, expand, batch, seq):
    m = p.shape[0]
    d = NSA_HEAD_DIM
    g_n = NSA_KV_HEADS
    tq = ATTN_TILE
    nq = seq // tq
    rep = 4
    gate_blk = (4 * g_n + 6 * g_n)
    n_chunk = kc.shape[2]

    def kv(n):
        return pl.BlockSpec((None, None, n, d), lambda b, g, i: (b, g, 0, 0))

    return pl.pallas_call(
        _nsa_attn_kernel,
        grid=(batch, g_n, nq),
        in_specs=[
            pl.BlockSpec((tq, rep * d), lambda b, g, i: (b * nq + i, g)),
            pl.BlockSpec((tq, LANES), lambda b, g, i: (b * nq + i, gate_blk)),
            pl.BlockSpec((tq, 3 * LANES), lambda b, g, i: (b * nq + i, 0)),
            pl.BlockSpec((1, d), lambda b, g, i: (0, 0)),
            kv(n_chunk), kv(n_chunk), kv(seq), kv(seq), kv(seq), kv(seq),
            pl.BlockSpec(ovl_t.shape, lambda b, g, i: (0, 0)),
            pl.BlockSpec(expand.shape, lambda b, g, i: (0, 0)),
        ],
        out_specs=pl.BlockSpec((tq, rep * d), lambda b, g, i: (b * nq + i, g)),
        out_shape=jax.ShapeDtypeStruct((m, g_n * rep * d), BF16),
        scratch_shapes=[
            pltpu.VMEM((rep * tq, d), BF16),
            pltpu.VMEM((rep * tq, 1), F32),
            pltpu.VMEM((rep * tq, 1), F32),
            pltpu.VMEM((rep * tq, d), F32),
            pltpu.VMEM((tq, seq), F32),
        ],
        compiler_params=_params("parallel", "parallel", "arbitrary"),
        name="nsa_attention",
    )(p, p, rope, q_norm, kc, vc, ks, vs, kw, vw, ovl_t, expand)


def _rope_table(pos, head_dim):
    rd = head_dim // ROPE_FRACTION
    half = rd // 2
    inv = 1.0 / (ROPE_THETA ** (jnp.arange(half, dtype=F32) * (2.0 / rd)))
    ang = pos.astype(F32)[:, None] * inv
    cos, sin = jnp.cos(ang), jnp.sin(ang)
    n = pos.shape[0]
    ones = jnp.ones((n, head_dim - rd), F32)
    zeros = jnp.zeros((n, head_dim - rd), F32)
    zh = jnp.zeros((n, half), F32)
    c = jnp.concatenate([cos, cos, ones], axis=1)
    s1 = jnp.concatenate([-sin, zh, zeros], axis=1)
    s2 = jnp.concatenate([zh, sin, zeros], axis=1)
    reps = LANES // head_dim
    return jnp.concatenate([jnp.tile(c, (1, reps)), jnp.tile(s1, (1, reps)), jnp.tile(s2, (1, reps))], axis=1)


def _selection_constants(seq):
    n_cmp_pad = seq // CMP_STRIDE
    n_slc = seq // SLC_BLOCK
    cs = np.arange(n_cmp_pad)[None, :] * CMP_STRIDE
    ss = np.arange(n_slc)[:, None] * SLC_BLOCK
    ovl_t = ((cs < ss + SLC_BLOCK) & (cs + CMP_BLOCK > ss)).astype(np.float32)
    expand = (np.arange(seq)[None, :] // SLC_BLOCK == np.arange(n_slc)[:, None]).astype(np.float32)
    return jnp.asarray(ovl_t, BF16), jnp.asarray(expand, BF16)


def kernel(x, positions, ffn1_norm, ffn1_w_in, ffn1_w_out, mix_norm, ffn2_norm, ffn2_w_in, ffn2_w_out,
           swa_w_in, swa_q_norm, swa_k_norm, swa_sinks, swa_w_out,
           nsa_w_in, nsa_q_norm, nsa_k_norm, nsa_cmp_pe_k, nsa_cmp_w1_k, nsa_cmp_w2_k,
           nsa_cmp_pe_v, nsa_cmp_w1_v, nsa_cmp_w2_v, nsa_w_out):
    batch, seq, d_model = x.shape
    depth = ffn1_norm.shape[0]
    m = batch * seq
    h = x.reshape(m, d_model)

    f1_in, f1_out = ffn1_w_in.astype(BF16), ffn1_w_out.astype(BF16)
    f2_in, f2_out = ffn2_w_in.astype(BF16), ffn2_w_out.astype(BF16)
    swa_in, swa_out = swa_w_in.astype(BF16), swa_w_out.astype(BF16)
    nsa_n = nsa_w_in.shape[2]
    nsa_tn = 768
    nsa_pad = -nsa_n % nsa_tn
    nsa_in = jnp.pad(nsa_w_in.astype(BF16), ((0, 0), (0, 0), (0, nsa_pad)))
    nsa_out = nsa_w_out.astype(BF16)
    w1_k, w2_k = nsa_cmp_w1_k.astype(BF16), nsa_cmp_w2_k.astype(BF16)
    w1_v, w2_v = nsa_cmp_w1_v.astype(BF16), nsa_cmp_w2_v.astype(BF16)

    n1 = ffn1_norm.reshape(depth, 1, d_model)
    nm = mix_norm.reshape(depth, 1, d_model)
    n2 = ffn2_norm.reshape(depth, 1, d_model)

    pos_flat = positions.reshape(m)
    rope_swa = _rope_table(pos_flat, SWA_HEAD_DIM)
    rope_nsa = _rope_table(pos_flat, NSA_HEAD_DIM)
    n_chunk = seq // CMP_STRIDE
    end_idx = jnp.minimum(jnp.arange(n_chunk) * CMP_STRIDE + (CMP_BLOCK - 1), seq - 1)
    rope_cmp = _rope_table(positions[:, end_idx].reshape(batch * n_chunk), NSA_HEAD_DIM)
    ovl_t, expand = _selection_constants(seq)

    for i in range(depth):
        h = _ffn(h, n1, f1_in, f1_out, i)
        j = i // N_MIXERS
        if i % N_MIXERS == 0:
            p = _proj(h, nm, swa_in, i, j, 512)
            qn2 = jnp.tile(swa_q_norm[j], LANES // SWA_HEAD_DIM).reshape(1, LANES)
            kn2 = jnp.tile(swa_k_norm[j], LANES // SWA_HEAD_DIM).reshape(1, LANES)
            o = _swa_attention(p, rope_swa, qn2, kn2, swa_sinks[j], batch, seq)
            h = _oproj(h, o, swa_out, j)
        else:
            p = _proj(h, nm, nsa_in, i, j, nsa_tn)
            kn = nsa_k_norm[j].reshape(1, NSA_HEAD_DIM)
            qn = nsa_q_norm[j].reshape(1, NSA_HEAD_DIM)
            kc, vc, ks, vs, kw, vw = _nsa_prep(
                p, rope_nsa, rope_cmp, kn, nsa_cmp_pe_k[j], w1_k[j], w2_k[j],
                nsa_cmp_pe_v[j], w1_v[j], w2_v[j], batch, seq)
            o = _nsa_attention(p, rope_nsa, qn, kc, vc, ks, vs, kw, vw, ovl_t, expand, batch, seq)
            h = _oproj(h, o, nsa_out, j)
        h = _ffn(h, n2, f2_in, f2_out, i)
    return h.reshape(batch, seq, d_model)
```

```python
import math

import jax
import jax.numpy as jnp
import numpy as np
from jax import lax
from jax.experimental import pallas as pl
from jax.experimental.pallas import tpu as pltpu

F32 = jnp.float32
BF16 = jnp.bfloat16

N_MIXERS = 2
SWA_HEAD_DIM = 64
SWA_KV_HEADS = 4
SWA_WINDOW = 128
NSA_HEAD_DIM = 128
NSA_KV_HEADS = 4
CMP_BLOCK = 32
CMP_STRIDE = 16
SLC_BLOCK = 64
SLC_TOP_N = 16
NSA_WINDOW = 512
ROPE_THETA = 500000.0
ROPE_FRACTION = 4
NORM_EPS = 1e-6
NEG_INF = -1e30
FORCE_SCORE = 1e9
TINY = 1e-30
LOG2E = math.log2(math.e)

LANES = 128
SUBLANES = 8
VMEM_LIMIT = 56 * 1024 * 1024

ROW_TILE = 512
FFN_TILE = 512
SWA_TILE = 128
SWA_SKEW = 2
NSA_TILE = 256
NSA_PREP_TILE = 1024
SEL_KEY_TILE = 512


def _params(*sem):
    return pltpu.CompilerParams(dimension_semantics=sem, vmem_limit_bytes=VMEM_LIMIT)


def _dot(a, b):
    return jnp.dot(a, b, preferred_element_type=F32)


def _dot_nt(a, b):
    return lax.dot_general(a, b, (((1,), (1,)), ((), ())), preferred_element_type=F32)


def _eye(n):
    r = lax.broadcasted_iota(jnp.int32, (n, n), 0)
    c = lax.broadcasted_iota(jnp.int32, (n, n), 1)
    return jnp.where(r == c, 1.0, 0.0).astype(BF16)


def _transpose_bf16(x):
    return _dot_nt(_eye(x.shape[1]), x).astype(BF16)


def _software_pipeline(n_items, stages, skew):
    state = [{} for _ in range(n_items)]
    for t in range(n_items + skew * (len(stages) - 1)):
        for k, stage in enumerate(stages):
            c = t - k * skew
            if 0 <= c < n_items:
                stage(c, state[c])


def _rms(x, g):
    ms = jnp.mean(x * x, axis=-1, keepdims=True)
    return x * lax.rsqrt(ms + NORM_EPS) * g


def _sigmoid(x):
    return 1.0 / (1.0 + jnp.exp(-x))


def _rope(xn, rope, half):
    c, s1, s2 = rope[:, 0:LANES], rope[:, LANES:2 * LANES], rope[:, 2 * LANES:3 * LANES]
    return xn * c + pltpu.roll(xn, LANES - half, 1) * s1 + pltpu.roll(xn, half, 1) * s2


def _ffn_kernel(x_ref, g_ref, wg_ref, wu_ref, wo_ref, o_ref, xn_ref, acc_ref):
    j = pl.program_id(1)

    @pl.when(j == 0)
    def _():
        xn_ref[...] = _rms(x_ref[...], g_ref[...]).astype(BF16)
        acc_ref[...] = jnp.zeros_like(acc_ref)

    xn = xn_ref[...]
    gate = _dot(xn, wg_ref[...])
    up = _dot(xn, wu_ref[...])
    hdn = (gate * _sigmoid(gate) * up).astype(BF16)
    acc_ref[...] += _dot(hdn, wo_ref[...])

    @pl.when(j == pl.num_programs(1) - 1)
    def _():
        o_ref[...] = x_ref[...] + 0.5 * acc_ref[...]


def _ffn(h, norm, w_in, w_out, layer):
    m, d = h.shape
    f = w_out.shape[1]
    nf = f // FFN_TILE
    return pl.pallas_call(
        _ffn_kernel,
        grid=(m // ROW_TILE, nf),
        in_specs=[
            pl.BlockSpec((ROW_TILE, d), lambda i, j: (i, 0)),
            pl.BlockSpec((None, 1, d), lambda i, j: (layer, 0, 0)),
            pl.BlockSpec((None, d, FFN_TILE), lambda i, j: (layer, 0, j)),
            pl.BlockSpec((None, d, FFN_TILE), lambda i, j: (layer, 0, nf + j)),
            pl.BlockSpec((None, FFN_TILE, d), lambda i, j: (layer, j, 0)),
        ],
        out_specs=pl.BlockSpec((ROW_TILE, d), lambda i, j: (i, 0)),
        out_shape=jax.ShapeDtypeStruct((m, d), F32),
        scratch_shapes=[pltpu.VMEM((ROW_TILE, d), BF16), pltpu.VMEM((ROW_TILE, d), F32)],
        compiler_params=_params("parallel", "arbitrary"),
        name="ffn",
    )(h, norm, w_in, w_in, w_out)


def _proj_kernel(x_ref, g_ref, w_ref, o_ref, xn_ref):
    @pl.when(pl.program_id(1) == 0)
    def _():
        xn_ref[...] = _rms(x_ref[...], g_ref[...]).astype(BF16)

    o_ref[...] = _dot(xn_ref[...], w_ref[...])


def _proj(h, norm, w, layer, wl, tn):
    m, d = h.shape
    n = w.shape[2]
    return pl.pallas_call(
        _proj_kernel,
        grid=(m // ROW_TILE, n // tn),
        in_specs=[
            pl.BlockSpec((ROW_TILE, d), lambda i, j: (i, 0)),
            pl.BlockSpec((None, 1, d), lambda i, j: (layer, 0, 0)),
            pl.BlockSpec((None, d, tn), lambda i, j: (wl, 0, j)),
        ],
        out_specs=pl.BlockSpec((ROW_TILE, tn), lambda i, j: (i, j)),
        out_shape=jax.ShapeDtypeStruct((m, n), F32),
        scratch_shapes=[pltpu.VMEM((ROW_TILE, d), BF16)],
        compiler_params=_params("parallel", "arbitrary"),
        name="mixer_in_proj",
    )(h, norm, w)


def _oproj_kernel(h_ref, o_ref, w_ref, out_ref):
    out_ref[...] = h_ref[...] + _dot(o_ref[...], w_ref[...])


def _oproj(h, o, w, wl):
    m, d = h.shape
    k = o.shape[1]
    return pl.pallas_call(
        _oproj_kernel,
        grid=(m // ROW_TILE,),
        in_specs=[
            pl.BlockSpec((ROW_TILE, d), lambda i: (i, 0)),
            pl.BlockSpec((ROW_TILE, k), lambda i: (i, 0)),
            pl.BlockSpec((None, k, d), lambda i: (wl, 0, 0)),
        ],
        out_specs=pl.BlockSpec((ROW_TILE, d), lambda i: (i, 0)),
        out_shape=jax.ShapeDtypeStruct((m, d), F32),
        compiler_params=_params("parallel"),
        name="mixer_out_proj",
    )(h, o, w)


def _swa_kernel(sink_ref, q_ref, kvc_ref, kvp_ref, rc_ref, rp_ref, qn_ref, kn_ref, o_ref):
    i = pl.program_id(1)
    tq = q_ref.shape[0]
    hd = SWA_HEAD_DIM
    half = hd // ROPE_FRACTION // 2
    n_chunks = q_ref.shape[1] // LANES
    heads_per_group = (2 * n_chunks) // SWA_KV_HEADS
    kv_w = SWA_KV_HEADS * hd
    n_kv_chunks = kv_w // LANES
    c1 = (hd ** -0.5) * LOG2E

    row = lax.broadcasted_iota(jnp.int32, (LANES, LANES), 0)
    col = lax.broadcasted_iota(jnp.int32, (LANES, LANES), 1)
    seg = jnp.where((row // hd) == (col // hd), 1.0 / hd, 0.0).astype(BF16)
    eye = _eye(LANES)

    def mean_sq(x):
        xx = x * x
        hi = xx.astype(BF16)
        lo = (xx - hi.astype(F32)).astype(BF16)
        return _dot(hi, seg) + _dot(lo, seg)

    def norm_rope(x, gn, rope):
        return _rope(x * lax.rsqrt(mean_sq(x) + NORM_EPS) * gn, rope, half)

    low = lax.broadcasted_iota(jnp.int32, (2 * tq, LANES), 1) < hd
    top = lax.broadcasted_iota(jnp.int32, (LANES, 2 * tq), 0) < hd
    kn = kn_ref[...]
    k_chunks = []
    for c in range(n_kv_chunks):
        sl = slice(c * LANES, (c + 1) * LANES)
        k_chunks.append(jnp.concatenate([norm_rope(kvp_ref[:, sl], kn, rp_ref[...]),
                                         norm_rope(kvc_ref[:, sl], kn, rc_ref[...])], axis=0))
    v_all = jnp.concatenate([kvp_ref[:, kv_w:2 * kv_w], kvc_ref[:, kv_w:2 * kv_w]], axis=0).astype(BF16)
    vt = _dot_nt(_eye(kv_w), v_all)

    kd, vlo, vhi = [], [], []
    for g in range(SWA_KV_HEADS):
        kc_ = k_chunks[g // 2]
        kc_sw = pltpu.roll(kc_, hd, 1)
        vc_ = vt[(g // 2) * LANES:(g // 2 + 1) * LANES, :]
        vc_sw = pltpu.roll(vc_, hd, 0)
        if g % 2 == 0:
            kd.append(jnp.where(low, kc_, kc_sw).astype(BF16))
            vlo.append(jnp.where(top, vc_, 0.0).astype(BF16))
            vhi.append(jnp.where(top, 0.0, vc_sw).astype(BF16))
        else:
            kd.append(jnp.where(low, kc_sw, kc_).astype(BF16))
            vlo.append(jnp.where(top, vc_sw, 0.0).astype(BF16))
            vhi.append(jnp.where(top, 0.0, vc_).astype(BF16))

    ki = lax.broadcasted_iota(jnp.int32, (2 * tq, tq), 0)
    qi = lax.broadcasted_iota(jnp.int32, (2 * tq, tq), 1)
    rel = qi + tq - ki
    mask = (rel >= 0) & (rel < SWA_WINDOW) & ((ki >= tq) | (i > 0))
    qtop = lax.broadcasted_iota(jnp.int32, (LANES, tq), 0) < hd
    qn = qn_ref[...]

    def st_load(c, st):
        st["x"] = q_ref[:, c * LANES:(c + 1) * LANES]
        st["ms"] = mean_sq(st["x"])

    def st_query(c, st):
        xq = _rope(st["x"] * lax.rsqrt(st["ms"] + NORM_EPS) * qn, rc_ref[...], half).astype(BF16)
        st["qt"] = _dot_nt(eye, xq)

    def st_scores(c, st):
        g = (2 * c) // heads_per_group
        qt = st["qt"]
        st["s"] = [_dot(kd[g], jnp.where(qtop, qt, 0.0).astype(BF16)),
                   _dot(kd[g], jnp.where(qtop, 0.0, qt).astype(BF16))]

    def st_values(c, st):
        g = (2 * c) // heads_per_group
        st["pv"], st["inv"] = [], []
        for par, vpl in ((0, vlo[g]), (1, vhi[g])):
            s = jnp.where(mask, st["s"][par], NEG_INF)
            sink = sink_ref[2 * c + par]
            mx = jnp.maximum(jnp.max(s, axis=0, keepdims=True) * (hd ** -0.5), sink) * LOG2E
            e = jnp.exp2(s * c1 - mx)
            den = jnp.sum(e, axis=0, keepdims=True) + jnp.exp2(sink * LOG2E - mx)
            st["inv"].append(1.0 / den)
            st["pv"].append(_dot(vpl, e.astype(BF16)))

    def st_merge(c, st):
        ot = (st["pv"][0] * st["inv"][0] + st["pv"][1] * st["inv"][1]).astype(BF16)
        st["o"] = _dot_nt(eye, ot)

    def st_store(c, st):
        o_ref[:, c * LANES:(c + 1) * LANES] = st["o"].astype(o_ref.dtype)

    _software_pipeline(n_chunks, (st_load, st_query, st_scores, st_values, st_merge, st_store), SWA_SKEW)


def _swa_attention(p, rope, q_norm2, k_norm2, sinks, batch, seq):
    m = p.shape[0]
    tq = SWA_TILE
    nb = seq // tq
    n_q = sinks.shape[0] * SWA_HEAD_DIM
    kv_w = 2 * SWA_KV_HEADS * SWA_HEAD_DIM
    kv_blk = n_q // kv_w
    return pl.pallas_call(
        _swa_kernel,
        grid=(batch, nb),
        in_specs=[
            pl.BlockSpec(memory_space=pltpu.SMEM),
            pl.BlockSpec((tq, n_q), lambda b, i: (b * nb + i, 0)),
            pl.BlockSpec((tq, kv_w), lambda b, i: (b * nb + i, kv_blk)),
            pl.BlockSpec((tq, kv_w), lambda b, i: (b * nb + jnp.maximum(i - 1, 0), kv_blk)),
            pl.BlockSpec((tq, 3 * LANES), lambda b, i: (b * nb + i, 0)),
            pl.BlockSpec((tq, 3 * LANES), lambda b, i: (b * nb + jnp.maximum(i - 1, 0), 0)),
            pl.BlockSpec((1, LANES), lambda b, i: (0, 0)),
            pl.BlockSpec((1, LANES), lambda b, i: (0, 0)),
        ],
        out_specs=pl.BlockSpec((tq, n_q), lambda b, i: (b * nb + i, 0)),
        out_shape=jax.ShapeDtypeStruct((m, n_q), BF16),
        compiler_params=_params("parallel", "arbitrary"),
        name="swa_attention",
    )(sinks, p, p, p, rope, rope, q_norm2, k_norm2)


def _nsa_compress_kernel(kc_ref, vc_ref, ropec_ref, kn_ref, pek_ref, w1k_ref, w2k_ref,
                         pev_ref, w1v_ref, w2v_ref, kco_ref, vco_ref):
    half = NSA_HEAD_DIM // ROPE_FRACTION // 2
    n_chunk = kc_ref.shape[0] // CMP_STRIDE
    per = CMP_BLOCK // CMP_STRIDE

    def compress(t_ref, pe_ref, w1_ref, w2_ref):
        parts = []
        for h in range(per):
            acc = jnp.zeros((n_chunk, w1_ref.shape[2]), F32)
            for l in range(CMP_STRIDE):
                ll = h * CMP_STRIDE + l
                xl = t_ref[pl.ds(l, n_chunk, stride=CMP_STRIDE), :] + pe_ref[ll:ll + 1, :]
                acc = acc + _dot(xl.astype(BF16), w1_ref[ll])
            parts.append(acc)
        pre = parts[0]
        for h in range(1, per):
            pre = pre + pltpu.roll(parts[h], n_chunk - h, 0)
        hdn = pre * _sigmoid(pre)
        return _dot(hdn.astype(BF16), w2_ref[...])

    kcmp = compress(kc_ref, pek_ref, w1k_ref, w2k_ref)
    kco_ref[...] = _rope(_rms(kcmp, kn_ref[...]), ropec_ref[...], half).astype(BF16)
    vco_ref[...] = _transpose_bf16(compress(vc_ref, pev_ref, w1v_ref, w2v_ref).astype(BF16))


def _nsa_compress(p, rope_c, k_norm, pe_k, w1_k, w2_k, pe_v, w1_v, w2_v, batch, seq):
    d = NSA_HEAD_DIM
    g_n = NSA_KV_HEADS
    n_chunk = seq // CMP_STRIDE
    q_blocks = 4 * g_n

    def col(kind):
        return pl.BlockSpec((seq, d), lambda b, g, kind=kind: (b, q_blocks + kind * g_n + g))

    def full(a):
        return pl.BlockSpec(a.shape, lambda b, g, nd=a.ndim: (0,) * nd)

    return pl.pallas_call(
        _nsa_compress_kernel,
        grid=(batch, g_n),
        in_specs=[col(0), col(1),
                  pl.BlockSpec((n_chunk, 3 * LANES), lambda b, g: (b, 0)),
                  full(k_norm), full(pe_k), full(w1_k), full(w2_k), full(pe_v), full(w1_v), full(w2_v)],
        out_specs=[pl.BlockSpec((None, None, n_chunk, d), lambda b, g: (b, g, 0, 0)),
                   pl.BlockSpec((None, None, d, n_chunk), lambda b, g: (b, g, 0, 0))],
        out_shape=[jax.ShapeDtypeStruct((batch, g_n, n_chunk, d), BF16),
                   jax.ShapeDtypeStruct((batch, g_n, d, n_chunk), BF16)],
        compiler_params=_params("parallel", "arbitrary"),
        name="nsa_compress",
    )(p, p, rope_c, k_norm, pe_k, w1_k, w2_k, pe_v, w1_v, w2_v)


def _nsa_qkv_kernel(q_ref, ks_ref, vs_ref, kw_ref, vw_ref, rope_ref, qn_ref, kn_ref,
                    qt_ref, kso_ref, vst_ref, kwo_ref, vwt_ref):
    d = NSA_HEAD_DIM
    half = d // ROPE_FRACTION // 2
    rep = q_ref.shape[1] // d
    tq = NSA_TILE
    eye = _eye(d)
    for s in range(q_ref.shape[0] // tq):
        rows = slice(s * tq, (s + 1) * tq)
        for r in range(rep):
            qr = _rope(_rms(q_ref[rows, r * d:(r + 1) * d], qn_ref[...]), rope_ref[rows, :], half)
            qt_ref[:, (s * rep + r) * tq:(s * rep + r + 1) * tq] = _dot_nt(eye, qr.astype(BF16)).astype(BF16)
    kso_ref[...] = _rope(_rms(ks_ref[...], kn_ref[...]), rope_ref[...], half).astype(BF16)
    kwo_ref[...] = _rope(_rms(kw_ref[...], kn_ref[...]), rope_ref[...], half).astype(BF16)
    vst_ref[...] = _dot_nt(eye, vs_ref[...].astype(BF16)).astype(BF16)
    vwt_ref[...] = _dot_nt(eye, vw_ref[...].astype(BF16)).astype(BF16)


def _nsa_qkv(p, rope, q_norm, k_norm, batch, seq):
    d = NSA_HEAD_DIM
    g_n = NSA_KV_HEADS
    rep = 4
    tt = NSA_PREP_TILE
    nt = seq // tt
    q_blocks = rep * g_n

    def col(kind):
        return pl.BlockSpec((tt, d), lambda b, g, t, kind=kind: (b * nt + t, q_blocks + kind * g_n + g))

    nat = pl.BlockSpec((None, None, tt, d), lambda b, g, t: (b, g, t, 0))
    tra = pl.BlockSpec((None, None, d, tt), lambda b, g, t: (b, g, 0, t))
    sh_nat = jax.ShapeDtypeStruct((batch, g_n, seq, d), BF16)
    sh_tra = jax.ShapeDtypeStruct((batch, g_n, d, seq), BF16)
    return pl.pallas_call(
        _nsa_qkv_kernel,
        grid=(batch, g_n, nt),
        in_specs=[pl.BlockSpec((tt, rep * d), lambda b, g, t: (b * nt + t, g)),
                  col(2), col(3), col(4), col(5),
                  pl.BlockSpec((tt, 3 * LANES), lambda b, g, t: (b * nt + t, 0)),
                  pl.BlockSpec((1, d), lambda b, g, t: (0, 0)),
                  pl.BlockSpec((1, d), lambda b, g, t: (0, 0))],
        out_specs=[pl.BlockSpec((None, None, d, rep * tt), lambda b, g, t: (b, g, 0, t)), nat, tra, nat, tra],
        out_shape=[jax.ShapeDtypeStruct((batch, g_n, d, rep * seq), BF16), sh_nat, sh_tra, sh_nat, sh_tra],
        compiler_params=_params("parallel", "parallel", "arbitrary"),
        name="nsa_qkv_prep",
    )(p, p, p, p, p, rope, q_norm, k_norm)


def _nsa_gate_kernel(g_ref, o_ref):
    o_ref[...] = _sigmoid(g_ref[...]).T


def _nsa_gates(p, batch, seq):
    tt = NSA_PREP_TILE
    nt = seq // tt
    gate_blk = 10 * NSA_KV_HEADS
    return pl.pallas_call(
        _nsa_gate_kernel,
        grid=(batch, nt),
        in_specs=[pl.BlockSpec((tt, LANES), lambda b, t: (b * nt + t, gate_blk))],
        out_specs=pl.BlockSpec((None, LANES, tt), lambda b, t: (b, 0, t)),
        out_shape=jax.ShapeDtypeStruct((batch, LANES, seq), F32),
        compiler_params=_params("parallel", "arbitrary"),
        name="nsa_gates",
    )(p)


def _nsa_kernel(qt_ref, gt_ref, kc_ref, vct_ref, ks_ref, vst_ref, kw_ref, vwt_ref, ovl_ref, o_ref,
                sel_scr, base_scr, acc_scr, s_scr):
    g = pl.program_id(1)
    i = pl.program_id(2)
    d = NSA_HEAD_DIM
    rows = qt_ref.shape[1]
    tq = o_ref.shape[0]
    rep = rows // tq
    n_cmp = kc_ref.shape[0]
    n_slc = ovl_ref.shape[0]
    n_heads = NSA_KV_HEADS * rep
    tk = SEL_KEY_TILE
    c2 = (d ** -0.5) * LOG2E
    q0 = i * tq
    qt = qt_ref[...]

    def tile_heads(a):
        return jnp.concatenate([a] * rep, axis=1)

    def gate_row(branch, r):
        return gt_ref[pl.ds(branch * n_heads + g * rep + r, 1), :]

    wk = NSA_WINDOW + tq
    start = pl.multiple_of(jnp.maximum(q0 - NSA_WINDOW, 0), tq)
    sw_raw = _dot(kw_ref[pl.ds(start, wk), :], qt)
    sc_raw = _dot(kc_ref[...], qt)
    s_scr[0] = _dot(ks_ref[0:tk, :], qt)

    t_w = q0 + lax.broadcasted_iota(jnp.int32, (wk, tq), 1)
    k_w = start + lax.broadcasted_iota(jnp.int32, (wk, tq), 0)
    mw = (k_w <= t_w) & (t_w - k_w < NSA_WINDOW)

    t_c = q0 + lax.broadcasted_iota(jnp.int32, (n_cmp, tq), 1)
    end_c = lax.broadcasted_iota(jnp.int32, (n_cmp, tq), 0) * CMP_STRIDE + (CMP_BLOCK - 1)
    mask_c = end_c <= t_c

    imp = jnp.zeros((n_slc, tq), F32)
    for r in range(rep):
        cols = slice(r * tq, (r + 1) * tq)
        sw = jnp.where(mw, sw_raw[:, cols], NEG_INF)
        pw = jnp.exp2((sw - jnp.max(sw, axis=0, keepdims=True)) * c2)
        l_win = jnp.sum(pw, axis=0, keepdims=True)
        o_win = _dot(vwt_ref[:, pl.ds(start, wk)], pw.astype(BF16))
        sc = jnp.where(mask_c, sc_raw[:, cols], NEG_INF)
        ec = jnp.where(mask_c, jnp.exp2((sc - jnp.max(sc, axis=0, keepdims=True)) * c2), 0.0)
        pc = (ec / jnp.maximum(jnp.sum(ec, axis=0, keepdims=True), TINY)).astype(BF16)
        o_cmp = _dot(vct_ref[...], pc)
        imp = imp + _dot(ovl_ref[...], pc)
        base_scr[:, cols] = gate_row(0, r) * o_cmp + (gate_row(2, r) / l_win) * o_win

    jb = lax.broadcasted_iota(jnp.int32, (n_slc, tq), 0)
    tt = q0 + lax.broadcasted_iota(jnp.int32, (n_slc, tq), 1)
    cur = tt // SLC_BLOCK
    forced = (jb == 0) | (jb == cur) | (jb == cur - 1)
    score = jnp.where(forced, FORCE_SCORE, jnp.where(jb * SLC_BLOCK <= tt, imp, NEG_INF))
    sub = SUBLANES
    blocks = [score[v * sub:(v + 1) * sub, :] for v in range(n_slc // sub)]
    counts = [jnp.zeros((sub, tq), F32) for _ in blocks]
    jsub = lax.broadcasted_iota(jnp.int32, (sub, tq), 0)
    for ii in range(n_slc):
        rowv = score[ii:ii + 1, :]
        for v, blk in enumerate(blocks):
            if (v + 1) * sub - 1 <= ii:
                beat = jnp.where(rowv > blk, 1.0, 0.0)
            elif v * sub > ii:
                beat = jnp.where(rowv >= blk, 1.0, 0.0)
            else:
                beat = jnp.where(jsub + v * sub > ii, jnp.where(rowv >= blk, 1.0, 0.0),
                                 jnp.where(rowv > blk, 1.0, 0.0))
            counts[v] = counts[v] + beat
    n_top = min(SLC_TOP_N, n_slc)
    for v, cnt in enumerate(counts):
        sel_scr[v * sub:(v + 1) * sub, :] = jnp.where(cnt < n_top, 1.0, 0.0)

    acc_scr[...] = jnp.zeros(acc_scr.shape, F32)
    blk_per_tile = tk // SLC_BLOCK
    t_b = q0 + lax.broadcasted_iota(jnp.int32, (SLC_BLOCK, tq), 1)
    k_b = lax.broadcasted_iota(jnp.int32, (SLC_BLOCK, tq), 0)

    n_tiles = (q0 + tq + tk - 1) // tk

    def sel_step(kt, carry):
        m_prev, l_prev = carry
        k0 = pl.multiple_of(kt * tk, tk)
        k1 = pl.multiple_of(jnp.minimum(kt + 1, n_tiles - 1) * tk, tk)
        s_scr[(kt + 1) % 2] = _dot(ks_ref[pl.ds(k1, tk), :], qt)
        parts = []
        for bb in range(blk_per_tile):
            chosen = sel_scr[pl.ds(kt * blk_per_tile + bb, 1), :] > 0.5
            parts.append(chosen & (k_b + (k0 + bb * SLC_BLOCK) <= t_b))
        mk = jnp.concatenate(parts, axis=0)
        slot = kt % 2
        m_out, l_out = [], []
        for r in range(rep):
            cols = slice(r * tq, (r + 1) * tq)
            sk = jnp.where(mk, s_scr[slot, :, cols], NEG_INF)
            m_new = jnp.maximum(m_prev[r], jnp.max(sk, axis=0, keepdims=True))
            alpha = jnp.exp2((m_prev[r] - m_new) * c2)
            pk = jnp.exp2((sk - m_new) * c2)
            l_out.append(alpha * l_prev[r] + jnp.sum(pk, axis=0, keepdims=True))
            m_out.append(m_new)
            acc_scr[:, cols] = alpha * acc_scr[:, cols] + _dot(vst_ref[:, pl.ds(k0, tk)], pk.astype(BF16))
        return tuple(m_out), tuple(l_out)

    init = (tuple(jnp.full((1, tq), NEG_INF, F32) for _ in range(rep)),
            tuple(jnp.zeros((1, tq), F32) for _ in range(rep)))
    _, l_sel = lax.fori_loop(0, n_tiles, sel_step, init)

    for r in range(rep):
        cols = slice(r * tq, (r + 1) * tq)
        out_t = base_scr[:, cols] + (gate_row(1, r) / l_sel[r]) * acc_scr[:, cols]
        o_ref[:, r * d:(r + 1) * d] = _transpose_bf16(out_t.astype(BF16)).astype(o_ref.dtype)


def _nsa_attend(qt, gt, kc, vct, ks, vst, kw, vwt, ovl_t, batch, seq):
    d = NSA_HEAD_DIM
    g_n = NSA_KV_HEADS
    tq = NSA_TILE
    nq = seq // tq
    rep = qt.shape[3] // seq
    n_chunk = kc.shape[2]
    n_slc = ovl_t.shape[0]
    gate_rows = -(-3 * g_n * rep // SUBLANES) * SUBLANES

    def kv(shape):
        return pl.BlockSpec((None, None) + shape, lambda b, g, i: (b, g, 0, 0))

    return pl.pallas_call(
        _nsa_kernel,
        grid=(batch, g_n, nq),
        in_specs=[
            pl.BlockSpec((None, None, d, rep * tq), lambda b, g, i: (b, g, 0, i)),
            pl.BlockSpec((None, gate_rows, tq), lambda b, g, i: (b, 0, i)),
            kv((n_chunk, d)), kv((d, n_chunk)), kv((seq, d)), kv((d, seq)), kv((seq, d)), kv((d, seq)),
            pl.BlockSpec(ovl_t.shape, lambda b, g, i: (0, 0)),
        ],
        out_specs=pl.BlockSpec((tq, rep * d), lambda b, g, i: (b * nq + i, g)),
        out_shape=jax.ShapeDtypeStruct((batch * seq, g_n * rep * d), BF16),
        scratch_shapes=[
            pltpu.VMEM((n_slc, tq), F32),
            pltpu.VMEM((d, rep * tq), F32),
            pltpu.VMEM((d, rep * tq), F32),
            pltpu.VMEM((2, SEL_KEY_TILE, rep * tq), F32),
        ],
        compiler_params=_params("parallel", "parallel", "arbitrary"),
        name="nsa_attention",
    )(qt, gt, kc, vct, ks, vst, kw, vwt, ovl_t)


def _rope_table(pos, head_dim):
    rd = head_dim // ROPE_FRACTION
    half = rd // 2
    inv = 1.0 / (ROPE_THETA ** (jnp.arange(half, dtype=F32) * (2.0 / rd)))
    ang = pos.astype(F32)[:, None] * inv
    cos, sin = jnp.cos(ang), jnp.sin(ang)
    n = pos.shape[0]
    ones = jnp.ones((n, head_dim - rd), F32)
    zeros = jnp.zeros((n, head_dim - rd), F32)
    zh = jnp.zeros((n, half), F32)
    c = jnp.concatenate([cos, cos, ones], axis=1)
    s1 = jnp.concatenate([-sin, zh, zeros], axis=1)
    s2 = jnp.concatenate([zh, sin, zeros], axis=1)
    reps = LANES // head_dim
    return jnp.concatenate([jnp.tile(c, (1, reps)), jnp.tile(s1, (1, reps)), jnp.tile(s2, (1, reps))], axis=1)


def _overlap_matrix(seq):
    n_cmp_pad = seq // CMP_STRIDE
    n_slc = seq // SLC_BLOCK
    cs = np.arange(n_cmp_pad)[None, :] * CMP_STRIDE
    ss = np.arange(n_slc)[:, None] * SLC_BLOCK
    ovl_t = ((cs < ss + SLC_BLOCK) & (cs + CMP_BLOCK > ss)).astype(np.float32)
    return jnp.asarray(ovl_t, BF16)


def kernel(x, positions, ffn1_norm, ffn1_w_in, ffn1_w_out, mix_norm, ffn2_norm, ffn2_w_in, ffn2_w_out,
           swa_w_in, swa_q_norm, swa_k_norm, swa_sinks, swa_w_out,
           nsa_w_in, nsa_q_norm, nsa_k_norm, nsa_cmp_pe_k, nsa_cmp_w1_k, nsa_cmp_w2_k,
           nsa_cmp_pe_v, nsa_cmp_w1_v, nsa_cmp_w2_v, nsa_w_out):
    batch, seq, d_model = x.shape
    depth = ffn1_norm.shape[0]
    m = batch * seq
    h = x.reshape(m, d_model)

    f1_in, f1_out = ffn1_w_in.astype(BF16), ffn1_w_out.astype(BF16)
    f2_in, f2_out = ffn2_w_in.astype(BF16), ffn2_w_out.astype(BF16)
    swa_in, swa_out = swa_w_in.astype(BF16), swa_w_out.astype(BF16)
    nsa_n = nsa_w_in.shape[2]
    nsa_tn = 768
    nsa_pad = -nsa_n % nsa_tn
    nsa_in = jnp.pad(nsa_w_in.astype(BF16), ((0, 0), (0, 0), (0, nsa_pad)))
    nsa_out = nsa_w_out.astype(BF16)
    w1_k, w2_k = nsa_cmp_w1_k.astype(BF16), nsa_cmp_w2_k.astype(BF16)
    w1_v, w2_v = nsa_cmp_w1_v.astype(BF16), nsa_cmp_w2_v.astype(BF16)

    n1 = ffn1_norm.reshape(depth, 1, d_model)
    nm = mix_norm.reshape(depth, 1, d_model)
    n2 = ffn2_norm.reshape(depth, 1, d_model)

    pos_flat = positions.reshape(m)
    rope_swa = _rope_table(pos_flat, SWA_HEAD_DIM)
    rope_nsa = _rope_table(pos_flat, NSA_HEAD_DIM)
    n_chunk = seq // CMP_STRIDE
    end_idx = jnp.minimum(jnp.arange(n_chunk) * CMP_STRIDE + (CMP_BLOCK - 1), seq - 1)
    rope_cmp = _rope_table(positions[:, end_idx].reshape(batch * n_chunk), NSA_HEAD_DIM)
    ovl_t = _overlap_matrix(seq)

    for i in range(depth):
        h = _ffn(h, n1, f1_in, f1_out, i)
        j = i // N_MIXERS
        if i % N_MIXERS == 0:
            p = _proj(h, nm, swa_in, i, j, 512)
            qn2 = jnp.tile(swa_q_norm[j], LANES // SWA_HEAD_DIM).reshape(1, LANES)
            kn2 = jnp.tile(swa_k_norm[j], LANES // SWA_HEAD_DIM).reshape(1, LANES)
            o = _swa_attention(p, rope_swa, qn2, kn2, swa_sinks[j], batch, seq)
            h = _oproj(h, o, swa_out, j)
        else:
            p = _proj(h, nm, nsa_in, i, j, nsa_tn)
            kn = nsa_k_norm[j].reshape(1, NSA_HEAD_DIM)
            qn = nsa_q_norm[j].reshape(1, NSA_HEAD_DIM)
            kc, vct = _nsa_compress(p, rope_cmp, kn, nsa_cmp_pe_k[j], w1_k[j], w2_k[j],
                                    nsa_cmp_pe_v[j], w1_v[j], w2_v[j], batch, seq)
            qt, ks, vst, kw, vwt = _nsa_qkv(p, rope_nsa, qn, kn, batch, seq)
            gt = _nsa_gates(p, batch, seq)
            o = _nsa_attend(qt, gt, kc, vct, ks, vst, kw, vwt, ovl_t, batch, seq)
            h = _oproj(h, o, nsa_out, j)
        h = _ffn(h, n2, f2_in, f2_out, i)
    return h.reshape(batch, seq, d_model)
```

```python
import math

import jax
import jax.numpy as jnp
import numpy as np
from jax import lax
from jax.experimental import pallas as pl
from jax.experimental.pallas import tpu as pltpu

F32 = jnp.float32
BF16 = jnp.bfloat16

N_MIXERS = 2
SWA_HEAD_DIM = 64
SWA_KV_HEADS = 4
SWA_WINDOW = 128
NSA_HEAD_DIM = 128
NSA_KV_HEADS = 4
CMP_BLOCK = 32
CMP_STRIDE = 16
SLC_BLOCK = 64
SLC_TOP_N = 16
NSA_WINDOW = 512
ROPE_THETA = 500000.0
ROPE_FRACTION = 4
NORM_EPS = 1e-6
NEG_INF = -1e30
FORCE_SCORE = 1e9
TINY = 1e-30
LOG2E = math.log2(math.e)

LANES = 128
SUBLANES = 8
VMEM_LIMIT = 56 * 1024 * 1024

ROW_TILE = 512
SWA_PROJ_TILE = 1280
NSA_PROJ_TILE = 1792
FFN_TILE = 512
FFN_SPLIT = 2
SWA_TILE = 128
SWA_SKEW = 2
NSA_TILE = 256
NSA_PREP_TILE = 1024
NSA_V_ROWS = NSA_HEAD_DIM + 16
NSA_QSCALE = (NSA_HEAD_DIM ** -0.5) * LOG2E
SEL_KEY_TILE = 512


def _params(*sem):
    return pltpu.CompilerParams(dimension_semantics=sem, vmem_limit_bytes=VMEM_LIMIT)


def _dot(a, b):
    return jnp.dot(a, b, preferred_element_type=F32)


def _dot_nt(a, b):
    return lax.dot_general(a, b, (((1,), (1,)), ((), ())), preferred_element_type=F32)


def _eye(n):
    r = lax.broadcasted_iota(jnp.int32, (n, n), 0)
    c = lax.broadcasted_iota(jnp.int32, (n, n), 1)
    return jnp.where(r == c, 1.0, 0.0).astype(BF16)


def _transpose_bf16(x):
    return _dot_nt(_eye(x.shape[1]), x).astype(BF16)


def _software_pipeline(n_items, stages, skew):
    state = [{} for _ in range(n_items)]
    for t in range(n_items + skew * (len(stages) - 1)):
        for k, stage in enumerate(stages):
            c = t - k * skew
            if 0 <= c < n_items:
                stage(c, state[c])


def _rms(x, g):
    ms = jnp.mean(x * x, axis=-1, keepdims=True)
    return x * lax.rsqrt(ms + NORM_EPS) * g


def _sigmoid(x):
    return 1.0 / (1.0 + jnp.exp(-x))


def _rope(xn, rope, half):
    c, s1, s2 = rope[:, 0:LANES], rope[:, LANES:2 * LANES], rope[:, 2 * LANES:3 * LANES]
    return xn * c + pltpu.roll(xn, LANES - half, 1) * s1 + pltpu.roll(xn, half, 1) * s2


def _ffn_kernel(x_ref, g_ref, wg_ref, wu_ref, wo_ref, o_ref, xn_ref):
    @pl.when(pl.program_id(1) == 0)
    def _():
        x = x_ref[...]
        xn_ref[...] = _rms(x, g_ref[...]).astype(BF16)
        o_ref[...] = x

    xn = xn_ref[...]
    w = wg_ref.shape[1] // FFN_SPLIT
    pre = [(_dot(xn, wg_ref[:, s * w:(s + 1) * w]), _dot(xn, wu_ref[:, s * w:(s + 1) * w]))
           for s in range(FFN_SPLIT)]
    upd = None
    for s, (gate, up) in enumerate(pre):
        hdn = (gate * _sigmoid(gate) * (0.5 * up)).astype(BF16)
        part = _dot(hdn, wo_ref[s * w:(s + 1) * w, :])
        upd = part if upd is None else upd + part
    o_ref[...] += upd


def _ffn(h, norm, w_in, w_out, layer):
    m, d = h.shape
    f = w_out.shape[1]
    nf = f // FFN_TILE
    return pl.pallas_call(
        _ffn_kernel,
        grid=(m // ROW_TILE, nf),
        in_specs=[
            pl.BlockSpec((ROW_TILE, d), lambda i, j: (i, 0)),
            pl.BlockSpec((None, 1, d), lambda i, j: (layer, 0, 0)),
            pl.BlockSpec((None, d, FFN_TILE), lambda i, j: (layer, 0, j)),
            pl.BlockSpec((None, d, FFN_TILE), lambda i, j: (layer, 0, nf + j)),
            pl.BlockSpec((None, FFN_TILE, d), lambda i, j: (layer, j, 0)),
        ],
        out_specs=pl.BlockSpec((ROW_TILE, d), lambda i, j: (i, 0)),
        out_shape=jax.ShapeDtypeStruct((m, d), F32),
        scratch_shapes=[pltpu.VMEM((ROW_TILE, d), BF16)],
        compiler_params=_params("parallel", "arbitrary"),
        name="ffn",
    )(h, norm, w_in, w_in, w_out)


def _proj_kernel(x_ref, g_ref, w_ref, o_ref, xn_ref):
    @pl.when(pl.program_id(1) == 0)
    def _():
        xn_ref[...] = _rms(x_ref[...], g_ref[...]).astype(BF16)

    o_ref[...] = _dot(xn_ref[...], w_ref[...])


def _proj(h, norm, w, layer, wl, tn):
    m, d = h.shape
    n = w.shape[2]
    return pl.pallas_call(
        _proj_kernel,
        grid=(m // ROW_TILE, n // tn),
        in_specs=[
            pl.BlockSpec((ROW_TILE, d), lambda i, j: (i, 0)),
            pl.BlockSpec((None, 1, d), lambda i, j: (layer, 0, 0)),
            pl.BlockSpec((None, d, tn), lambda i, j: (wl, 0, j)),
        ],
        out_specs=pl.BlockSpec((ROW_TILE, tn), lambda i, j: (i, j)),
        out_shape=jax.ShapeDtypeStruct((m, n), F32),
        scratch_shapes=[pltpu.VMEM((ROW_TILE, d), BF16)],
        compiler_params=_params("parallel", "arbitrary"),
        name="mixer_in_proj",
    )(h, norm, w)


def _oproj_kernel(h_ref, o_ref, w_ref, out_ref):
    out_ref[...] = h_ref[...] + _dot(o_ref[...], w_ref[...])


def _oproj(h, o, w, wl):
    m, d = h.shape
    k = o.shape[1]
    return pl.pallas_call(
        _oproj_kernel,
        grid=(m // ROW_TILE,),
        in_specs=[
            pl.BlockSpec((ROW_TILE, d), lambda i: (i, 0)),
            pl.BlockSpec((ROW_TILE, k), lambda i: (i, 0)),
            pl.BlockSpec((None, k, d), lambda i: (wl, 0, 0)),
        ],
        out_specs=pl.BlockSpec((ROW_TILE, d), lambda i: (i, 0)),
        out_shape=jax.ShapeDtypeStruct((m, d), F32),
        compiler_params=_params("parallel"),
        name="mixer_out_proj",
    )(h, o, w)


def _swa_kernel(sink_ref, q_ref, kvc_ref, kvp_ref, rc_ref, rp_ref, qn_ref, kn_ref, o_ref):
    i = pl.program_id(1)
    tq = q_ref.shape[0]
    hd = SWA_HEAD_DIM
    half = hd // ROPE_FRACTION // 2
    n_chunks = q_ref.shape[1] // LANES
    heads_per_group = (2 * n_chunks) // SWA_KV_HEADS
    kv_w = SWA_KV_HEADS * hd
    n_kv_chunks = kv_w // LANES
    c1 = (hd ** -0.5) * LOG2E

    row = lax.broadcasted_iota(jnp.int32, (LANES, LANES), 0)
    col = lax.broadcasted_iota(jnp.int32, (LANES, LANES), 1)
    seg = jnp.where((row // hd) == (col // hd), 1.0 / hd, 0.0).astype(BF16)
    eye = _eye(LANES)

    def mean_sq(x):
        xx = x * x
        hi = xx.astype(BF16)
        lo = (xx - hi.astype(F32)).astype(BF16)
        return _dot(hi, seg) + _dot(lo, seg)

    def norm_rope(x, gn, rope):
        return _rope(x * lax.rsqrt(mean_sq(x) + NORM_EPS) * gn, rope, half)

    low = lax.broadcasted_iota(jnp.int32, (2 * tq, LANES), 1) < hd
    top = lax.broadcasted_iota(jnp.int32, (LANES, 2 * tq), 0) < hd
    kn = kn_ref[...]
    k_chunks = []
    for c in range(n_kv_chunks):
        sl = slice(c * LANES, (c + 1) * LANES)
        k_chunks.append(jnp.concatenate([norm_rope(kvp_ref[:, sl], kn, rp_ref[...]),
                                         norm_rope(kvc_ref[:, sl], kn, rc_ref[...])], axis=0))
    v_all = jnp.concatenate([kvp_ref[:, kv_w:2 * kv_w], kvc_ref[:, kv_w:2 * kv_w]], axis=0).astype(BF16)
    vt = _dot_nt(_eye(kv_w), v_all)

    kd, vlo, vhi = [], [], []
    for g in range(SWA_KV_HEADS):
        kc_ = k_chunks[g // 2]
        kc_sw = pltpu.roll(kc_, hd, 1)
        vc_ = vt[(g // 2) * LANES:(g // 2 + 1) * LANES, :]
        vc_sw = pltpu.roll(vc_, hd, 0)
        if g % 2 == 0:
            kd.append(jnp.where(low, kc_, kc_sw).astype(BF16))
            vlo.append(jnp.where(top, vc_, 0.0).astype(BF16))
            vhi.append(jnp.where(top, 0.0, vc_sw).astype(BF16))
        else:
            kd.append(jnp.where(low, kc_sw, kc_).astype(BF16))
            vlo.append(jnp.where(top, vc_sw, 0.0).astype(BF16))
            vhi.append(jnp.where(top, 0.0, vc_).astype(BF16))

    ki = lax.broadcasted_iota(jnp.int32, (2 * tq, tq), 0)
    qi = lax.broadcasted_iota(jnp.int32, (2 * tq, tq), 1)
    rel = qi + tq - ki
    mask = (rel >= 0) & (rel < SWA_WINDOW) & ((ki >= tq) | (i > 0))
    qtop = lax.broadcasted_iota(jnp.int32, (LANES, tq), 0) < hd
    qn = qn_ref[...]

    def st_load(c, st):
        st["x"] = q_ref[:, c * LANES:(c + 1) * LANES]
        st["ms"] = mean_sq(st["x"])

    def st_query(c, st):
        xq = _rope(st["x"] * lax.rsqrt(st["ms"] + NORM_EPS) * qn, rc_ref[...], half).astype(BF16)
        st["qt"] = _dot_nt(eye, xq)

    def st_scores(c, st):
        g = (2 * c) // heads_per_group
        qt = st["qt"]
        st["s"] = [_dot(kd[g], jnp.where(qtop, qt, 0.0).astype(BF16)),
                   _dot(kd[g], jnp.where(qtop, 0.0, qt).astype(BF16))]

    def st_values(c, st):
        g = (2 * c) // heads_per_group
        st["pv"], st["inv"] = [], []
        for par, vpl in ((0, vlo[g]), (1, vhi[g])):
            s = jnp.where(mask, st["s"][par], NEG_INF)
            sink = sink_ref[2 * c + par]
            mx = jnp.maximum(jnp.max(s, axis=0, keepdims=True) * (hd ** -0.5), sink) * LOG2E
            e = jnp.exp2(s * c1 - mx)
            den = jnp.sum(e, axis=0, keepdims=True) + jnp.exp2(sink * LOG2E - mx)
            st["inv"].append(1.0 / den)
            st["pv"].append(_dot(vpl, e.astype(BF16)))

    def st_merge(c, st):
        ot = (st["pv"][0] * st["inv"][0] + st["pv"][1] * st["inv"][1]).astype(BF16)
        st["o"] = _dot_nt(eye, ot)

    def st_store(c, st):
        o_ref[:, c * LANES:(c + 1) * LANES] = st["o"].astype(o_ref.dtype)

    _software_pipeline(n_chunks, (st_load, st_query, st_scores, st_values, st_merge, st_store), SWA_SKEW)


def _swa_attention(p, rope, q_norm2, k_norm2, sinks, batch, seq):
    m = p.shape[0]
    tq = SWA_TILE
    nb = seq // tq
    n_q = sinks.shape[0] * SWA_HEAD_DIM
    kv_w = 2 * SWA_KV_HEADS * SWA_HEAD_DIM
    kv_blk = n_q // kv_w
    return pl.pallas_call(
        _swa_kernel,
        grid=(batch, nb),
        in_specs=[
            pl.BlockSpec(memory_space=pltpu.SMEM),
            pl.BlockSpec((tq, n_q), lambda b, i: (b * nb + i, 0)),
            pl.BlockSpec((tq, kv_w), lambda b, i: (b * nb + i, kv_blk)),
            pl.BlockSpec((tq, kv_w), lambda b, i: (b * nb + jnp.maximum(i - 1, 0), kv_blk)),
            pl.BlockSpec((tq, 3 * LANES), lambda b, i: (b * nb + i, 0)),
            pl.BlockSpec((tq, 3 * LANES), lambda b, i: (b * nb + jnp.maximum(i - 1, 0), 0)),
            pl.BlockSpec((1, LANES), lambda b, i: (0, 0)),
            pl.BlockSpec((1, LANES), lambda b, i: (0, 0)),
        ],
        out_specs=pl.BlockSpec((tq, n_q), lambda b, i: (b * nb + i, 0)),
        out_shape=jax.ShapeDtypeStruct((m, n_q), BF16),
        compiler_params=_params("parallel", "arbitrary"),
        name="swa_attention",
    )(sinks, p, p, p, rope, rope, q_norm2, k_norm2)


def _nsa_compress_kernel(kc_ref, vc_ref, ropec_ref, kn_ref, pek_ref, w1k_ref, w2k_ref,
                         pev_ref, w1v_ref, w2v_ref, kco_ref, vco_ref):
    half = NSA_HEAD_DIM // ROPE_FRACTION // 2
    n_chunk = kc_ref.shape[0] // CMP_STRIDE
    per = CMP_BLOCK // CMP_STRIDE

    def compress(t_ref, pe_ref, w1_ref, w2_ref):
        parts = []
        for h in range(per):
            acc = jnp.zeros((n_chunk, w1_ref.shape[2]), F32)
            for l in range(CMP_STRIDE):
                ll = h * CMP_STRIDE + l
                xl = t_ref[pl.ds(l, n_chunk, stride=CMP_STRIDE), :] + pe_ref[ll:ll + 1, :]
                acc = acc + _dot(xl.astype(BF16), w1_ref[ll])
            parts.append(acc)
        pre = parts[0]
        for h in range(1, per):
            pre = pre + pltpu.roll(parts[h], n_chunk - h, 0)
        hdn = pre * _sigmoid(pre)
        return _dot(hdn.astype(BF16), w2_ref[...])

    kcmp = compress(kc_ref, pek_ref, w1k_ref, w2k_ref)
    kco_ref[...] = _rope(_rms(kcmp, kn_ref[...]), ropec_ref[...], half).astype(BF16)
    vco_ref[...] = _transpose_bf16(compress(vc_ref, pev_ref, w1v_ref, w2v_ref).astype(BF16))


def _nsa_compress(p, rope_c, k_norm, pe_k, w1_k, w2_k, pe_v, w1_v, w2_v, batch, seq):
    d = NSA_HEAD_DIM
    g_n = NSA_KV_HEADS
    n_chunk = seq // CMP_STRIDE
    q_blocks = 4 * g_n

    def col(kind):
        return pl.BlockSpec((seq, d), lambda b, g, kind=kind: (b, q_blocks + kind * g_n + g))

    def full(a):
        return pl.BlockSpec(a.shape, lambda b, g, nd=a.ndim: (0,) * nd)

    return pl.pallas_call(
        _nsa_compress_kernel,
        grid=(batch, g_n),
        in_specs=[col(0), col(1),
                  pl.BlockSpec((n_chunk, 3 * LANES), lambda b, g: (b, 0)),
                  full(k_norm), full(pe_k), full(w1_k), full(w2_k), full(pe_v), full(w1_v), full(w2_v)],
        out_specs=[pl.BlockSpec((None, None, n_chunk, d), lambda b, g: (b, g, 0, 0)),
                   pl.BlockSpec((None, None, d, n_chunk), lambda b, g: (b, g, 0, 0))],
        out_shape=[jax.ShapeDtypeStruct((batch, g_n, n_chunk, d), BF16),
                   jax.ShapeDtypeStruct((batch, g_n, d, n_chunk), BF16)],
        compiler_params=_params("parallel", "arbitrary"),
        name="nsa_compress",
    )(p, p, rope_c, k_norm, pe_k, w1_k, w2_k, pe_v, w1_v, w2_v)


def _nsa_qkv_kernel(q_ref, ks_ref, vs_ref, kw_ref, vw_ref, rope_ref, qn_ref, kn_ref,
                    qt_ref, kso_ref, vst_ref, kwo_ref, vwt_ref):
    d = NSA_HEAD_DIM
    half = d // ROPE_FRACTION // 2
    rep = q_ref.shape[1] // d
    tq = NSA_TILE
    eye = _eye(d)
    for s in range(q_ref.shape[0] // tq):
        rows = slice(s * tq, (s + 1) * tq)
        for r in range(rep):
            qr = _rope(_rms(q_ref[rows, r * d:(r + 1) * d], qn_ref[...]), rope_ref[rows, :], half) * NSA_QSCALE
            qt_ref[:, (s * rep + r) * tq:(s * rep + r + 1) * tq] = _dot_nt(eye, qr.astype(BF16)).astype(BF16)
    kso_ref[...] = _rope(_rms(ks_ref[...], kn_ref[...]), rope_ref[...], half).astype(BF16)
    kwo_ref[...] = _rope(_rms(kw_ref[...], kn_ref[...]), rope_ref[...], half).astype(BF16)
    ones = jnp.ones((NSA_V_ROWS - d, vs_ref.shape[0]), BF16)
    vst_ref[0:d, :] = _dot_nt(eye, vs_ref[...].astype(BF16)).astype(BF16)
    vst_ref[d:NSA_V_ROWS, :] = ones
    vwt_ref[0:d, :] = _dot_nt(eye, vw_ref[...].astype(BF16)).astype(BF16)
    vwt_ref[d:NSA_V_ROWS, :] = ones


def _nsa_qkv(p, rope, q_norm, k_norm, batch, seq):
    d = NSA_HEAD_DIM
    g_n = NSA_KV_HEADS
    rep = 4
    tt = NSA_PREP_TILE
    nt = seq // tt
    q_blocks = rep * g_n

    def col(kind):
        return pl.BlockSpec((tt, d), lambda b, g, t, kind=kind: (b * nt + t, q_blocks + kind * g_n + g))

    nat = pl.BlockSpec((None, None, tt, d), lambda b, g, t: (b, g, t, 0))
    tra = pl.BlockSpec((None, None, NSA_V_ROWS, tt), lambda b, g, t: (b, g, 0, t))
    sh_nat = jax.ShapeDtypeStruct((batch, g_n, seq, d), BF16)
    sh_tra = jax.ShapeDtypeStruct((batch, g_n, NSA_V_ROWS, seq), BF16)
    return pl.pallas_call(
        _nsa_qkv_kernel,
        grid=(batch, g_n, nt),
        in_specs=[pl.BlockSpec((tt, rep * d), lambda b, g, t: (b * nt + t, g)),
                  col(2), col(3), col(4), col(5),
                  pl.BlockSpec((tt, 3 * LANES), lambda b, g, t: (b * nt + t, 0)),
                  pl.BlockSpec((1, d), lambda b, g, t: (0, 0)),
                  pl.BlockSpec((1, d), lambda b, g, t: (0, 0))],
        out_specs=[pl.BlockSpec((None, None, d, rep * tt), lambda b, g, t: (b, g, 0, t)), nat, tra, nat, tra],
        out_shape=[jax.ShapeDtypeStruct((batch, g_n, d, rep * seq), BF16), sh_nat, sh_tra, sh_nat, sh_tra],
        compiler_params=_params("parallel", "parallel", "arbitrary"),
        name="nsa_qkv_prep",
    )(p, p, p, p, p, rope, q_norm, k_norm)


def _nsa_gate_kernel(g_ref, o_ref):
    o_ref[...] = _sigmoid(g_ref[...]).T


def _nsa_gates(p, batch, seq):
    tt = NSA_PREP_TILE
    nt = seq // tt
    gate_blk = 10 * NSA_KV_HEADS
    return pl.pallas_call(
        _nsa_gate_kernel,
        grid=(batch, nt),
        in_specs=[pl.BlockSpec((tt, LANES), lambda b, t: (b * nt + t, gate_blk))],
        out_specs=pl.BlockSpec((None, LANES, tt), lambda b, t: (b, 0, t)),
        out_shape=jax.ShapeDtypeStruct((batch, LANES, seq), F32),
        compiler_params=_params("parallel", "arbitrary"),
        name="nsa_gates",
    )(p)


def _nsa_kernel(qt_ref, gt_ref, kc_ref, vct_ref, ks_ref, vst_ref, kw_ref, vwt_ref, ovl_ref, o_ref,
                sel_scr, base_scr, acc_scr, s_scr):
    g = pl.program_id(1)
    i = pl.program_id(2)
    d = NSA_HEAD_DIM
    rows = qt_ref.shape[1]
    tq = o_ref.shape[0]
    rep = rows // tq
    n_cmp = kc_ref.shape[0]
    n_slc = ovl_ref.shape[0]
    n_heads = NSA_KV_HEADS * rep
    tk = SEL_KEY_TILE
    q0 = i * tq
    qt = qt_ref[...]

    def tile_heads(a):
        return jnp.concatenate([a] * rep, axis=1)

    def gate_row(branch, r):
        return gt_ref[pl.ds(branch * n_heads + g * rep + r, 1), :]

    wk = NSA_WINDOW + tq
    start = pl.multiple_of(jnp.maximum(q0 - NSA_WINDOW, 0), tq)
    sw_raw = _dot(kw_ref[pl.ds(start, wk), :], qt)
    sc_raw = _dot(kc_ref[...], qt)
    s_scr[0] = _dot(ks_ref[0:tk, :], qt)

    t_w = q0 + lax.broadcasted_iota(jnp.int32, (wk, tq), 1)
    k_w = start + lax.broadcasted_iota(jnp.int32, (wk, tq), 0)
    mw = (k_w <= t_w) & (t_w - k_w < NSA_WINDOW)

    t_c = q0 + lax.broadcasted_iota(jnp.int32, (n_cmp, tq), 1)
    end_c = lax.broadcasted_iota(jnp.int32, (n_cmp, tq), 0) * CMP_STRIDE + (CMP_BLOCK - 1)
    mask_c = end_c <= t_c

    imp = jnp.zeros((n_slc, tq), F32)
    for r in range(rep):
        cols = slice(r * tq, (r + 1) * tq)
        sw = jnp.where(mw, sw_raw[:, cols], NEG_INF)
        pw = jnp.exp2(sw - jnp.max(sw, axis=0, keepdims=True))
        o_win = _dot(vwt_ref[:, pl.ds(start, wk)], pw.astype(BF16))
        l_win = o_win[d:d + 1, :]
        sc = jnp.where(mask_c, sc_raw[:, cols], NEG_INF)
        ec = jnp.where(mask_c, jnp.exp2(sc - jnp.max(sc, axis=0, keepdims=True)), 0.0)
        pc = (ec / jnp.maximum(jnp.sum(ec, axis=0, keepdims=True), TINY)).astype(BF16)
        o_cmp = _dot(vct_ref[...], pc)
        imp = imp + _dot(ovl_ref[...], pc)
        base_scr[:, cols] = gate_row(0, r) * o_cmp + (gate_row(2, r) / l_win) * o_win[0:d, :]

    jb = lax.broadcasted_iota(jnp.int32, (n_slc, tq), 0)
    tt = q0 + lax.broadcasted_iota(jnp.int32, (n_slc, tq), 1)
    cur = tt // SLC_BLOCK
    forced = (jb == 0) | (jb == cur) | (jb == cur - 1)
    score = jnp.where(forced, FORCE_SCORE, jnp.where(jb * SLC_BLOCK <= tt, imp, NEG_INF))
    sub = SUBLANES
    blocks = [score[v * sub:(v + 1) * sub, :] for v in range(n_slc // sub)]
    counts = [jnp.zeros((sub, tq), F32) for _ in blocks]
    jsub = lax.broadcasted_iota(jnp.int32, (sub, tq), 0)
    for ii in range(n_slc):
        rowv = score[ii:ii + 1, :]
        for v, blk in enumerate(blocks):
            if (v + 1) * sub - 1 <= ii:
                beat = jnp.where(rowv > blk, 1.0, 0.0)
            elif v * sub > ii:
                beat = jnp.where(rowv >= blk, 1.0, 0.0)
            else:
                beat = jnp.where(jsub + v * sub > ii, jnp.where(rowv >= blk, 1.0, 0.0),
                                 jnp.where(rowv > blk, 1.0, 0.0))
            counts[v] = counts[v] + beat
    n_top = min(SLC_TOP_N, n_slc)
    for v, cnt in enumerate(counts):
        sel_scr[v * sub:(v + 1) * sub, :] = jnp.where(cnt < n_top, 1.0, 0.0)

    acc_scr[...] = jnp.zeros(acc_scr.shape, F32)
    blk_per_tile = tk // SLC_BLOCK
    t_b = q0 + lax.broadcasted_iota(jnp.int32, (SLC_BLOCK, tq), 1)
    k_b = lax.broadcasted_iota(jnp.int32, (SLC_BLOCK, tq), 0)

    n_tiles = (q0 + tq + tk - 1) // tk

    def sel_step(kt, m_prev):
        k0 = pl.multiple_of(kt * tk, tk)
        k1 = pl.multiple_of(jnp.minimum(kt + 1, n_tiles - 1) * tk, tk)
        s_scr[(kt + 1) % 2] = _dot(ks_ref[pl.ds(k1, tk), :], qt)
        parts = []
        for bb in range(blk_per_tile):
            chosen = sel_scr[pl.ds(kt * blk_per_tile + bb, 1), :] > 0.5
            parts.append(chosen & (k_b + (k0 + bb * SLC_BLOCK) <= t_b))
        mk = jnp.concatenate(parts, axis=0)
        slot = kt % 2
        m_out = []
        for r in range(rep):
            cols = slice(r * tq, (r + 1) * tq)
            sk = jnp.where(mk, s_scr[slot, :, cols], NEG_INF)
            m_new = jnp.maximum(m_prev[r], jnp.max(sk, axis=0, keepdims=True))
            alpha = jnp.exp2(m_prev[r] - m_new)
            pk = jnp.exp2(sk - m_new)
            m_out.append(m_new)
            acc_scr[:, cols] = alpha * acc_scr[:, cols] + _dot(vst_ref[:, pl.ds(k0, tk)], pk.astype(BF16))
        return tuple(m_out)

    lax.fori_loop(0, n_tiles, sel_step, tuple(jnp.full((1, tq), NEG_INF, F32) for _ in range(rep)))

    for r in range(rep):
        cols = slice(r * tq, (r + 1) * tq)
        out_t = base_scr[:, cols] + (gate_row(1, r) / acc_scr[d:d + 1, cols]) * acc_scr[0:d, cols]
        o_ref[:, r * d:(r + 1) * d] = _transpose_bf16(out_t.astype(BF16)).astype(o_ref.dtype)


def _nsa_attend(qt, gt, kc, vct, ks, vst, kw, vwt, ovl_t, batch, seq):
    d = NSA_HEAD_DIM
    g_n = NSA_KV_HEADS
    tq = NSA_TILE
    nq = seq // tq
    rep = qt.shape[3] // seq
    n_chunk = kc.shape[2]
    n_slc = ovl_t.shape[0]
    gate_rows = -(-3 * g_n * rep // SUBLANES) * SUBLANES

    def kv(shape):
        return pl.BlockSpec((None, None) + shape, lambda b, g, i: (b, g, 0, 0))

    return pl.pallas_call(
        _nsa_kernel,
        grid=(batch, g_n, nq),
        in_specs=[
            pl.BlockSpec((None, None, d, rep * tq), lambda b, g, i: (b, g, 0, i)),
            pl.BlockSpec((None, gate_rows, tq), lambda b, g, i: (b, 0, i)),
            kv((n_chunk, d)), kv((d, n_chunk)),
            kv((seq, d)), kv((NSA_V_ROWS, seq)), kv((seq, d)), kv((NSA_V_ROWS, seq)),
            pl.BlockSpec(ovl_t.shape, lambda b, g, i: (0, 0)),
        ],
        out_specs=pl.BlockSpec((tq, rep * d), lambda b, g, i: (b * nq + i, g)),
        out_shape=jax.ShapeDtypeStruct((batch * seq, g_n * rep * d), BF16),
        scratch_shapes=[
            pltpu.VMEM((n_slc, tq), F32),
            pltpu.VMEM((d, rep * tq), F32),
            pltpu.VMEM((NSA_V_ROWS, rep * tq), F32),
            pltpu.VMEM((2, SEL_KEY_TILE, rep * tq), F32),
        ],
        compiler_params=_params("parallel", "parallel", "arbitrary"),
        name="nsa_attention",
    )(qt, gt, kc, vct, ks, vst, kw, vwt, ovl_t)


def _rope_table(pos, head_dim):
    rd = head_dim // ROPE_FRACTION
    half = rd // 2
    inv = 1.0 / (ROPE_THETA ** (jnp.arange(half, dtype=F32) * (2.0 / rd)))
    ang = pos.astype(F32)[:, None] * inv
    cos, sin = jnp.cos(ang), jnp.sin(ang)
    n = pos.shape[0]
    ones = jnp.ones((n, head_dim - rd), F32)
    zeros = jnp.zeros((n, head_dim - rd), F32)
    zh = jnp.zeros((n, half), F32)
    c = jnp.concatenate([cos, cos, ones], axis=1)
    s1 = jnp.concatenate([-sin, zh, zeros], axis=1)
    s2 = jnp.concatenate([zh, sin, zeros], axis=1)
    reps = LANES // head_dim
    return jnp.concatenate([jnp.tile(c, (1, reps)), jnp.tile(s1, (1, reps)), jnp.tile(s2, (1, reps))], axis=1)


def _overlap_matrix(seq):
    n_cmp_pad = seq // CMP_STRIDE
    n_slc = seq // SLC_BLOCK
    cs = np.arange(n_cmp_pad)[None, :] * CMP_STRIDE
    ss = np.arange(n_slc)[:, None] * SLC_BLOCK
    ovl_t = ((cs < ss + SLC_BLOCK) & (cs + CMP_BLOCK > ss)).astype(np.float32)
    return jnp.asarray(ovl_t, BF16)


def kernel(x, positions, ffn1_norm, ffn1_w_in, ffn1_w_out, mix_norm, ffn2_norm, ffn2_w_in, ffn2_w_out,
           swa_w_in, swa_q_norm, swa_k_norm, swa_sinks, swa_w_out,
           nsa_w_in, nsa_q_norm, nsa_k_norm, nsa_cmp_pe_k, nsa_cmp_w1_k, nsa_cmp_w2_k,
           nsa_cmp_pe_v, nsa_cmp_w1_v, nsa_cmp_w2_v, nsa_w_out):
    batch, seq, d_model = x.shape
    depth = ffn1_norm.shape[0]
    m = batch * seq
    h = x.reshape(m, d_model)

    f1_in, f1_out = ffn1_w_in.astype(BF16), ffn1_w_out.astype(BF16)
    f2_in, f2_out = ffn2_w_in.astype(BF16), ffn2_w_out.astype(BF16)
    swa_in, swa_out = swa_w_in.astype(BF16), swa_w_out.astype(BF16)
    nsa_n = nsa_w_in.shape[2]
    nsa_tn = NSA_PROJ_TILE
    nsa_pad = -nsa_n % nsa_tn
    nsa_in = jnp.pad(nsa_w_in.astype(BF16), ((0, 0), (0, 0), (0, nsa_pad)))
    nsa_out = nsa_w_out.astype(BF16)
    w1_k, w2_k = nsa_cmp_w1_k.astype(BF16), nsa_cmp_w2_k.astype(BF16)
    w1_v, w2_v = nsa_cmp_w1_v.astype(BF16), nsa_cmp_w2_v.astype(BF16)

    n1 = ffn1_norm.reshape(depth, 1, d_model)
    nm = mix_norm.reshape(depth, 1, d_model)
    n2 = ffn2_norm.reshape(depth, 1, d_model)

    pos_flat = positions.reshape(m)
    rope_swa = _rope_table(pos_flat, SWA_HEAD_DIM)
    rope_nsa = _rope_table(pos_flat, NSA_HEAD_DIM)
    n_chunk = seq // CMP_STRIDE
    end_idx = jnp.minimum(jnp.arange(n_chunk) * CMP_STRIDE + (CMP_BLOCK - 1), seq - 1)
    rope_cmp = _rope_table(positions[:, end_idx].reshape(batch * n_chunk), NSA_HEAD_DIM)
    ovl_t = _overlap_matrix(seq)

    for i in range(depth):
        h = _ffn(h, n1, f1_in, f1_out, i)
        j = i // N_MIXERS
        if i % N_MIXERS == 0:
            p = _proj(h, nm, swa_in, i, j, SWA_PROJ_TILE)
            qn2 = jnp.tile(swa_q_norm[j], LANES // SWA_HEAD_DIM).reshape(1, LANES)
            kn2 = jnp.tile(swa_k_norm[j], LANES // SWA_HEAD_DIM).reshape(1, LANES)
            o = _swa_attention(p, rope_swa, qn2, kn2, swa_sinks[j], batch, seq)
            h = _oproj(h, o, swa_out, j)
        else:
            p = _proj(h, nm, nsa_in, i, j, nsa_tn)
            kn = nsa_k_norm[j].reshape(1, NSA_HEAD_DIM)
            qn = nsa_q_norm[j].reshape(1, NSA_HEAD_DIM)
            kc, vct = _nsa_compress(p, rope_cmp, kn, nsa_cmp_pe_k[j], w1_k[j], w2_k[j],
                                    nsa_cmp_pe_v[j], w1_v[j], w2_v[j], batch, seq)
            qt, ks, vst, kw, vwt = _nsa_qkv(p, rope_nsa, qn, kn, batch, seq)
            gt = _nsa_gates(p, batch, seq)
            o = _nsa_attend(qt, gt, kc, vct, ks, vst, kw, vwt, ovl_t, batch, seq)
            h = _oproj(h, o, nsa_out, j)
        h = _ffn(h, n2, f2_in, f2_out, i)
    return h.reshape(batch, seq, d_model)
```

```python
import math

import jax
import jax.numpy as jnp
import numpy as np
from jax import lax
from jax.experimental import pallas as pl
from jax.experimental.pallas import tpu as pltpu

F32 = jnp.float32
BF16 = jnp.bfloat16

N_MIXERS = 2
SWA_HEAD_DIM = 64
SWA_KV_HEADS = 4
SWA_WINDOW = 128
NSA_HEAD_DIM = 128
NSA_KV_HEADS = 4
CMP_BLOCK = 32
CMP_STRIDE = 16
SLC_BLOCK = 64
SLC_TOP_N = 16
NSA_WINDOW = 512
ROPE_THETA = 500000.0
ROPE_FRACTION = 4
NORM_EPS = 1e-6
NEG_INF = -1e30
FORCE_SCORE = 1e9
TINY = 1e-30
LOG2E = math.log2(math.e)

LANES = 128
SUBLANES = 8
VMEM_LIMIT = 56 * 1024 * 1024

ROW_TILE = 512
SWA_PROJ_TILE = 1280
NSA_PROJ_TILE = 1792
FFN_TILE = 512
FFN_SPLIT = 2
SWA_TILE = 128
SWA_SKEW = 2
NSA_TILE = 256
NSA_PREP_TILE = 1024
NSA_V_ROWS = NSA_HEAD_DIM + 16
NSA_QSCALE = (NSA_HEAD_DIM ** -0.5) * LOG2E
PAIR = 2
SEL_KEY_TILE = 512


def _params(*sem):
    return pltpu.CompilerParams(dimension_semantics=sem, vmem_limit_bytes=VMEM_LIMIT)


def _dot(a, b):
    return jnp.dot(a, b, preferred_element_type=F32)


def _dot_nt(a, b):
    return lax.dot_general(a, b, (((1,), (1,)), ((), ())), preferred_element_type=F32)


def _eye(n):
    r = lax.broadcasted_iota(jnp.int32, (n, n), 0)
    c = lax.broadcasted_iota(jnp.int32, (n, n), 1)
    return jnp.where(r == c, 1.0, 0.0).astype(BF16)


def _transpose_bf16(x):
    return _dot_nt(_eye(x.shape[1]), x).astype(BF16)


def _software_pipeline(n_items, stages, skew):
    state = [{} for _ in range(n_items)]
    for t in range(n_items + skew * (len(stages) - 1)):
        for k, stage in enumerate(stages):
            c = t - k * skew
            if 0 <= c < n_items:
                stage(c, state[c])


def _rms(x, g):
    ms = jnp.mean(x * x, axis=-1, keepdims=True)
    return x * lax.rsqrt(ms + NORM_EPS) * g


def _sigmoid(x):
    return 1.0 / (1.0 + jnp.exp(-x))


def _rope(xn, rope, half):
    c, s1, s2 = rope[:, 0:LANES], rope[:, LANES:2 * LANES], rope[:, 2 * LANES:3 * LANES]
    return xn * c + pltpu.roll(xn, LANES - half, 1) * s1 + pltpu.roll(xn, half, 1) * s2


def _ffn_kernel(x_ref, g_ref, wg_ref, wu_ref, wo_ref, o_ref, xn_ref):
    @pl.when(pl.program_id(1) == 0)
    def _():
        x = x_ref[...]
        xn_ref[...] = _rms(x, g_ref[...]).astype(BF16)
        o_ref[...] = x

    xn = xn_ref[...]
    w = wg_ref.shape[1] // FFN_SPLIT
    pre = [(_dot(xn, wg_ref[:, s * w:(s + 1) * w]), _dot(xn, wu_ref[:, s * w:(s + 1) * w]))
           for s in range(FFN_SPLIT)]
    upd = None
    for s, (gate, up) in enumerate(pre):
        hdn = (gate * _sigmoid(gate) * (0.5 * up)).astype(BF16)
        part = _dot(hdn, wo_ref[s * w:(s + 1) * w, :])
        upd = part if upd is None else upd + part
    o_ref[...] += upd


def _ffn(h, norm, w_in, w_out, layer):
    m, d = h.shape
    f = w_out.shape[1]
    nf = f // FFN_TILE
    return pl.pallas_call(
        _ffn_kernel,
        grid=(m // ROW_TILE, nf),
        in_specs=[
            pl.BlockSpec((ROW_TILE, d), lambda i, j: (i, 0)),
            pl.BlockSpec((None, 1, d), lambda i, j: (layer, 0, 0)),
            pl.BlockSpec((None, d, FFN_TILE), lambda i, j: (layer, 0, j)),
            pl.BlockSpec((None, d, FFN_TILE), lambda i, j: (layer, 0, nf + j)),
            pl.BlockSpec((None, FFN_TILE, d), lambda i, j: (layer, j, 0)),
        ],
        out_specs=pl.BlockSpec((ROW_TILE, d), lambda i, j: (i, 0)),
        out_shape=jax.ShapeDtypeStruct((m, d), F32),
        scratch_shapes=[pltpu.VMEM((ROW_TILE, d), BF16)],
        compiler_params=_params("parallel", "arbitrary"),
        name="ffn",
    )(h, norm, w_in, w_in, w_out)


def _proj_kernel(x_ref, g_ref, w_ref, o_ref, xn_ref):
    @pl.when(pl.program_id(1) == 0)
    def _():
        xn_ref[...] = _rms(x_ref[...], g_ref[...]).astype(BF16)

    o_ref[...] = _dot(xn_ref[...], w_ref[...])


def _proj(h, norm, w, layer, wl, tn):
    m, d = h.shape
    n = w.shape[2]
    return pl.pallas_call(
        _proj_kernel,
        grid=(m // ROW_TILE, n // tn),
        in_specs=[
            pl.BlockSpec((ROW_TILE, d), lambda i, j: (i, 0)),
            pl.BlockSpec((None, 1, d), lambda i, j: (layer, 0, 0)),
            pl.BlockSpec((None, d, tn), lambda i, j: (wl, 0, j)),
        ],
        out_specs=pl.BlockSpec((ROW_TILE, tn), lambda i, j: (i, j)),
        out_shape=jax.ShapeDtypeStruct((m, n), F32),
        scratch_shapes=[pltpu.VMEM((ROW_TILE, d), BF16)],
        compiler_params=_params("parallel", "arbitrary"),
        name="mixer_in_proj",
    )(h, norm, w)


def _oproj_kernel(h_ref, o_ref, w_ref, out_ref):
    out_ref[...] = h_ref[...] + _dot(o_ref[...], w_ref[...])


def _oproj(h, o, w, wl):
    m, d = h.shape
    k = o.shape[1]
    return pl.pallas_call(
        _oproj_kernel,
        grid=(m // ROW_TILE,),
        in_specs=[
            pl.BlockSpec((ROW_TILE, d), lambda i: (i, 0)),
            pl.BlockSpec((ROW_TILE, k), lambda i: (i, 0)),
            pl.BlockSpec((None, k, d), lambda i: (wl, 0, 0)),
        ],
        out_specs=pl.BlockSpec((ROW_TILE, d), lambda i: (i, 0)),
        out_shape=jax.ShapeDtypeStruct((m, d), F32),
        compiler_params=_params("parallel"),
        name="mixer_out_proj",
    )(h, o, w)


def _swa_kernel(sink_ref, q_ref, kvc_ref, kvp_ref, rc_ref, rp_ref, qn_ref, kn_ref, o_ref):
    i = pl.program_id(1)
    tq = q_ref.shape[0]
    hd = SWA_HEAD_DIM
    half = hd // ROPE_FRACTION // 2
    n_chunks = q_ref.shape[1] // LANES
    heads_per_group = (2 * n_chunks) // SWA_KV_HEADS
    kv_w = SWA_KV_HEADS * hd
    n_kv_chunks = kv_w // LANES
    c1 = (hd ** -0.5) * LOG2E

    row = lax.broadcasted_iota(jnp.int32, (LANES, LANES), 0)
    col = lax.broadcasted_iota(jnp.int32, (LANES, LANES), 1)
    seg = jnp.where((row // hd) == (col // hd), 1.0 / hd, 0.0).astype(BF16)
    eye = _eye(LANES)

    def mean_sq(x):
        xx = x * x
        hi = xx.astype(BF16)
        lo = (xx - hi.astype(F32)).astype(BF16)
        return _dot(hi, seg) + _dot(lo, seg)

    def norm_rope(x, gn, rope):
        return _rope(x * lax.rsqrt(mean_sq(x) + NORM_EPS) * gn, rope, half)

    low = lax.broadcasted_iota(jnp.int32, (2 * tq, LANES), 1) < hd
    top = lax.broadcasted_iota(jnp.int32, (LANES, 2 * tq), 0) < hd
    kn = kn_ref[...]
    k_chunks = []
    for c in range(n_kv_chunks):
        sl = slice(c * LANES, (c + 1) * LANES)
        k_chunks.append(jnp.concatenate([norm_rope(kvp_ref[:, sl], kn, rp_ref[...]),
                                         norm_rope(kvc_ref[:, sl], kn, rc_ref[...])], axis=0))
    v_all = jnp.concatenate([kvp_ref[:, kv_w:2 * kv_w], kvc_ref[:, kv_w:2 * kv_w]], axis=0).astype(BF16)
    vt = _dot_nt(_eye(kv_w), v_all)

    kd, vlo, vhi = [], [], []
    for g in range(SWA_KV_HEADS):
        kc_ = k_chunks[g // 2]
        kc_sw = pltpu.roll(kc_, hd, 1)
        vc_ = vt[(g // 2) * LANES:(g // 2 + 1) * LANES, :]
        vc_sw = pltpu.roll(vc_, hd, 0)
        if g % 2 == 0:
            kd.append(jnp.where(low, kc_, kc_sw).astype(BF16))
            vlo.append(jnp.where(top, vc_, 0.0).astype(BF16))
            vhi.append(jnp.where(top, 0.0, vc_sw).astype(BF16))
        else:
            kd.append(jnp.where(low, kc_sw, kc_).astype(BF16))
            vlo.append(jnp.where(top, vc_sw, 0.0).astype(BF16))
            vhi.append(jnp.where(top, 0.0, vc_).astype(BF16))

    ki = lax.broadcasted_iota(jnp.int32, (2 * tq, tq), 0)
    qi = lax.broadcasted_iota(jnp.int32, (2 * tq, tq), 1)
    rel = qi + tq - ki
    mask = (rel >= 0) & (rel < SWA_WINDOW) & ((ki >= tq) | (i > 0))
    qtop = lax.broadcasted_iota(jnp.int32, (LANES, tq), 0) < hd
    qn = qn_ref[...]

    def st_load(c, st):
        st["x"] = q_ref[:, c * LANES:(c + 1) * LANES]
        st["ms"] = mean_sq(st["x"])

    def st_query(c, st):
        xq = _rope(st["x"] * lax.rsqrt(st["ms"] + NORM_EPS) * qn, rc_ref[...], half).astype(BF16)
        st["qt"] = _dot_nt(eye, xq)

    def st_scores(c, st):
        g = (2 * c) // heads_per_group
        qt = st["qt"]
        st["s"] = [_dot(kd[g], jnp.where(qtop, qt, 0.0).astype(BF16)),
                   _dot(kd[g], jnp.where(qtop, 0.0, qt).astype(BF16))]

    def st_values(c, st):
        g = (2 * c) // heads_per_group
        st["pv"], st["inv"] = [], []
        for par, vpl in ((0, vlo[g]), (1, vhi[g])):
            s = jnp.where(mask, st["s"][par], NEG_INF)
            sink = sink_ref[2 * c + par]
            mx = jnp.maximum(jnp.max(s, axis=0, keepdims=True) * (hd ** -0.5), sink) * LOG2E
            e = jnp.exp2(s * c1 - mx)
            den = jnp.sum(e, axis=0, keepdims=True) + jnp.exp2(sink * LOG2E - mx)
            st["inv"].append(1.0 / den)
            st["pv"].append(_dot(vpl, e.astype(BF16)))

    def st_merge(c, st):
        ot = (st["pv"][0] * st["inv"][0] + st["pv"][1] * st["inv"][1]).astype(BF16)
        st["o"] = _dot_nt(eye, ot)

    def st_store(c, st):
        o_ref[:, c * LANES:(c + 1) * LANES] = st["o"].astype(o_ref.dtype)

    _software_pipeline(n_chunks, (st_load, st_query, st_scores, st_values, st_merge, st_store), SWA_SKEW)


def _swa_attention(p, rope, q_norm2, k_norm2, sinks, batch, seq):
    m = p.shape[0]
    tq = SWA_TILE
    nb = seq // tq
    n_q = sinks.shape[0] * SWA_HEAD_DIM
    kv_w = 2 * SWA_KV_HEADS * SWA_HEAD_DIM
    kv_blk = n_q // kv_w
    return pl.pallas_call(
        _swa_kernel,
        grid=(batch, nb),
        in_specs=[
            pl.BlockSpec(memory_space=pltpu.SMEM),
            pl.BlockSpec((tq, n_q), lambda b, i: (b * nb + i, 0)),
            pl.BlockSpec((tq, kv_w), lambda b, i: (b * nb + i, kv_blk)),
            pl.BlockSpec((tq, kv_w), lambda b, i: (b * nb + jnp.maximum(i - 1, 0), kv_blk)),
            pl.BlockSpec((tq, 3 * LANES), lambda b, i: (b * nb + i, 0)),
            pl.BlockSpec((tq, 3 * LANES), lambda b, i: (b * nb + jnp.maximum(i - 1, 0), 0)),
            pl.BlockSpec((1, LANES), lambda b, i: (0, 0)),
            pl.BlockSpec((1, LANES), lambda b, i: (0, 0)),
        ],
        out_specs=pl.BlockSpec((tq, n_q), lambda b, i: (b * nb + i, 0)),
        out_shape=jax.ShapeDtypeStruct((m, n_q), BF16),
        compiler_params=_params("parallel", "arbitrary"),
        name="swa_attention",
    )(sinks, p, p, p, rope, rope, q_norm2, k_norm2)


def _nsa_compress_kernel(kc_ref, vc_ref, ropec_ref, kn_ref, pek_ref, w1k_ref, w2k_ref,
                         pev_ref, w1v_ref, w2v_ref, kco_ref, vco_ref):
    half = NSA_HEAD_DIM // ROPE_FRACTION // 2
    n_chunk = kc_ref.shape[0] // CMP_STRIDE
    per = CMP_BLOCK // CMP_STRIDE

    def compress(t_ref, pe_ref, w1_ref, w2_ref):
        parts = []
        for h in range(per):
            acc = jnp.zeros((n_chunk, w1_ref.shape[2]), F32)
            for l in range(CMP_STRIDE):
                ll = h * CMP_STRIDE + l
                xl = t_ref[pl.ds(l, n_chunk, stride=CMP_STRIDE), :] + pe_ref[ll:ll + 1, :]
                acc = acc + _dot(xl.astype(BF16), w1_ref[ll])
            parts.append(acc)
        pre = parts[0]
        for h in range(1, per):
            pre = pre + pltpu.roll(parts[h], n_chunk - h, 0)
        hdn = pre * _sigmoid(pre)
        return _dot(hdn.astype(BF16), w2_ref[...])

    kcmp = compress(kc_ref, pek_ref, w1k_ref, w2k_ref)
    kco_ref[...] = _rope(_rms(kcmp, kn_ref[...]), ropec_ref[...], half).astype(BF16)
    vco_ref[...] = _transpose_bf16(compress(vc_ref, pev_ref, w1v_ref, w2v_ref).astype(BF16))


def _nsa_compress(p, rope_c, k_norm, pe_k, w1_k, w2_k, pe_v, w1_v, w2_v, batch, seq):
    d = NSA_HEAD_DIM
    g_n = NSA_KV_HEADS
    n_chunk = seq // CMP_STRIDE
    q_blocks = 4 * g_n

    def col(kind):
        return pl.BlockSpec((seq, d), lambda b, g, kind=kind: (b, q_blocks + kind * g_n + g))

    def full(a):
        return pl.BlockSpec(a.shape, lambda b, g, nd=a.ndim: (0,) * nd)

    return pl.pallas_call(
        _nsa_compress_kernel,
        grid=(batch, g_n),
        in_specs=[col(0), col(1),
                  pl.BlockSpec((n_chunk, 3 * LANES), lambda b, g: (b, 0)),
                  full(k_norm), full(pe_k), full(w1_k), full(w2_k), full(pe_v), full(w1_v), full(w2_v)],
        out_specs=[pl.BlockSpec((None, None, n_chunk, d), lambda b, g: (b, g, 0, 0)),
                   pl.BlockSpec((None, None, d, n_chunk), lambda b, g: (b, g, 0, 0))],
        out_shape=[jax.ShapeDtypeStruct((batch, g_n, n_chunk, d), BF16),
                   jax.ShapeDtypeStruct((batch, g_n, d, n_chunk), BF16)],
        compiler_params=_params("parallel", "arbitrary"),
        name="nsa_compress",
    )(p, p, rope_c, k_norm, pe_k, w1_k, w2_k, pe_v, w1_v, w2_v)


def _nsa_qkv_kernel(q_ref, ks_ref, vs_ref, kw_ref, vw_ref, rope_ref, qn_ref, kn_ref,
                    qt_ref, kso_ref, vst_ref, kwo_ref, vwt_ref):
    d = NSA_HEAD_DIM
    half = d // ROPE_FRACTION // 2
    rep = q_ref.shape[1] // d
    tq = NSA_TILE
    eye = _eye(d)
    for s in range(q_ref.shape[0] // tq):
        rows = slice(s * tq, (s + 1) * tq)
        for r in range(rep):
            qr = _rope(_rms(q_ref[rows, r * d:(r + 1) * d], qn_ref[...]), rope_ref[rows, :], half) * NSA_QSCALE
            qt_ref[:, (s * rep + r) * tq:(s * rep + r + 1) * tq] = _dot_nt(eye, qr.astype(BF16)).astype(BF16)
    kso_ref[...] = _rope(_rms(ks_ref[...], kn_ref[...]), rope_ref[...], half).astype(BF16)
    kwo_ref[...] = _rope(_rms(kw_ref[...], kn_ref[...]), rope_ref[...], half).astype(BF16)
    ones = jnp.ones((NSA_V_ROWS - d, vs_ref.shape[0]), BF16)
    vst_ref[0:d, :] = _dot_nt(eye, vs_ref[...].astype(BF16)).astype(BF16)
    vst_ref[d:NSA_V_ROWS, :] = ones
    vwt_ref[0:d, :] = _dot_nt(eye, vw_ref[...].astype(BF16)).astype(BF16)
    vwt_ref[d:NSA_V_ROWS, :] = ones


def _nsa_qkv(p, rope, q_norm, k_norm, batch, seq):
    d = NSA_HEAD_DIM
    g_n = NSA_KV_HEADS
    rep = 4
    tt = NSA_PREP_TILE
    nt = seq // tt
    q_blocks = rep * g_n

    def col(kind):
        return pl.BlockSpec((tt, d), lambda b, g, t, kind=kind: (b * nt + t, q_blocks + kind * g_n + g))

    nat = pl.BlockSpec((None, None, tt, d), lambda b, g, t: (b, g, t, 0))
    tra = pl.BlockSpec((None, None, NSA_V_ROWS, tt), lambda b, g, t: (b, g, 0, t))
    sh_nat = jax.ShapeDtypeStruct((batch, g_n, seq, d), BF16)
    sh_tra = jax.ShapeDtypeStruct((batch, g_n, NSA_V_ROWS, seq), BF16)
    return pl.pallas_call(
        _nsa_qkv_kernel,
        grid=(batch, g_n, nt),
        in_specs=[pl.BlockSpec((tt, rep * d), lambda b, g, t: (b * nt + t, g)),
                  col(2), col(3), col(4), col(5),
                  pl.BlockSpec((tt, 3 * LANES), lambda b, g, t: (b * nt + t, 0)),
                  pl.BlockSpec((1, d), lambda b, g, t: (0, 0)),
                  pl.BlockSpec((1, d), lambda b, g, t: (0, 0))],
        out_specs=[pl.BlockSpec((None, None, d, rep * tt), lambda b, g, t: (b, g, 0, t)), nat, tra, nat, tra],
        out_shape=[jax.ShapeDtypeStruct((batch, g_n, d, rep * seq), BF16), sh_nat, sh_tra, sh_nat, sh_tra],
        compiler_params=_params("parallel", "parallel", "arbitrary"),
        name="nsa_qkv_prep",
    )(p, p, p, p, p, rope, q_norm, k_norm)


def _nsa_gate_kernel(g_ref, o_ref):
    o_ref[...] = _sigmoid(g_ref[...]).T


def _nsa_gates(p, batch, seq):
    tt = NSA_PREP_TILE
    nt = seq // tt
    gate_blk = 10 * NSA_KV_HEADS
    return pl.pallas_call(
        _nsa_gate_kernel,
        grid=(batch, nt),
        in_specs=[pl.BlockSpec((tt, LANES), lambda b, t: (b * nt + t, gate_blk))],
        out_specs=pl.BlockSpec((None, LANES, tt), lambda b, t: (b, 0, t)),
        out_shape=jax.ShapeDtypeStruct((batch, LANES, seq), F32),
        compiler_params=_params("parallel", "arbitrary"),
        name="nsa_gates",
    )(p)


def _nsa_kernel(qt_ref, gt_ref, kc_ref, vct_ref, ks_ref, vst_ref, kw_ref, vwt_ref, ovl_ref, o_ref,
                sel_scr, score_scr, cnt_scr, base_scr, acc_scr, s0_scr, sw_scr, bias_scr, p_scr):
    g = pl.program_id(1)
    i = pl.program_id(2)
    d = NSA_HEAD_DIM
    rows = qt_ref.shape[1]
    tq = o_ref.shape[0]
    rep = rows // tq
    n_cmp = kc_ref.shape[0]
    n_slc = ovl_ref.shape[0]
    n_heads = NSA_KV_HEADS * rep
    tk = SEL_KEY_TILE
    ch = SLC_BLOCK
    q0 = i * tq
    qt = qt_ref[...]

    def head_cols(r):
        return slice(r * tq, (r + 1) * tq)

    def pair_cols(pr):
        return slice(pr * PAIR * tq, (pr + 1) * PAIR * tq)

    def gate_row(branch, r):
        return gt_ref[pl.ds(branch * n_heads + g * rep + r, 1), :]

    def fold_max(x):
        parts = [x[j * SUBLANES:(j + 1) * SUBLANES, :] for j in range(x.shape[0] // SUBLANES)]
        while len(parts) > 1:
            parts = [jnp.maximum(parts[j], parts[j + 1]) for j in range(0, len(parts), 2)]
        return parts[0]

    def chunked_softmax(chunk, n_chunks, m_floor, p_slot, p_cols):
        m8 = fold_max(chunk(0))
        for c in range(1, n_chunks):
            m8 = jnp.maximum(m8, fold_max(chunk(c)))
        m = jnp.max(m8, axis=0, keepdims=True)
        if m_floor is not None:
            m = jnp.maximum(m, m_floor)
        for c in range(n_chunks):
            p_scr[p_slot, c * ch:(c + 1) * ch, p_cols] = jnp.exp2(chunk(c) - m).astype(BF16)
        return m

    wk = NSA_WINDOW + tq
    start = pl.multiple_of(jnp.maximum(q0 - NSA_WINDOW, 0), tq)

    @pl.when(i <= NSA_WINDOW // tq)
    def _():
        for c in range(wk // ch):
            t_w = q0 + lax.broadcasted_iota(jnp.int32, (ch, tq), 1)
            k_w = start + c * ch + lax.broadcasted_iota(jnp.int32, (ch, tq), 0)
            ok = (k_w <= t_w) & (t_w - k_w < NSA_WINDOW)
            bias_scr[c * ch:(c + 1) * ch, :] = jnp.where(ok, 0.0, NEG_INF)

    sw_scr[0] = _dot(kw_ref[pl.ds(start, wk), :], qt[:, pair_cols(0)])
    sc_raw = _dot(kc_ref[...], qt)
    for pr in range(1, rep // PAIR):
        sw_scr[pr] = _dot(kw_ref[pl.ds(start, wk), :], qt[:, pair_cols(pr)])
    s0_scr[...] = _dot(ks_ref[0:tk, :], qt[:, pair_cols(0)])

    t_c = q0 + lax.broadcasted_iota(jnp.int32, (n_cmp, tq), 1)
    end_c = lax.broadcasted_iota(jnp.int32, (n_cmp, tq), 0) * CMP_STRIDE + (CMP_BLOCK - 1)
    mask_c = end_c <= t_c

    imp = jnp.zeros((n_slc, tq), F32)
    for pr in range(rep // PAIR):
        pcs = []
        for h in range(PAIR):
            cols = head_cols(pr * PAIR + h)
            chunked_softmax(lambda c, pr=pr, h=h: (sw_scr[pr, c * ch:(c + 1) * ch, head_cols(h)]
                                                   + bias_scr[c * ch:(c + 1) * ch, :]),
                            wk // ch, None, pr % 2, head_cols(h))
            sc = jnp.where(mask_c, sc_raw[:, cols], NEG_INF)
            ec = jnp.where(mask_c, jnp.exp2(sc - jnp.max(sc, axis=0, keepdims=True)), 0.0)
            pcs.append((ec / jnp.maximum(jnp.sum(ec, axis=0, keepdims=True), TINY)).astype(BF16))
        pc = jnp.concatenate(pcs, axis=1)
        o_win = _dot(vwt_ref[:, pl.ds(start, wk)], p_scr[pr % 2, 0:wk, :])
        o_cmp = _dot(vct_ref[...], pc)
        imp_pair = _dot(ovl_ref[...], pc)
        for h in range(PAIR):
            r = pr * PAIR + h
            hc = head_cols(h)
            imp = imp + imp_pair[:, hc]
            base_scr[:, head_cols(r)] = (gate_row(0, r) * o_cmp[:, hc]
                                         + (gate_row(2, r) / o_win[d:d + 1, hc]) * o_win[0:d, hc])

    jb = lax.broadcasted_iota(jnp.int32, (n_slc, tq), 0)
    tt = q0 + lax.broadcasted_iota(jnp.int32, (n_slc, tq), 1)
    cur = tt // SLC_BLOCK
    forced = (jb == 0) | (jb == cur) | (jb == cur - 1)
    score = jnp.where(forced, FORCE_SCORE, jnp.where(jb * SLC_BLOCK <= tt, imp, NEG_INF))
    sub = SUBLANES
    n_grp = n_slc // sub
    score_scr[...] = score
    cnt_scr[...] = jnp.zeros(cnt_scr.shape, F32)
    jsub = lax.broadcasted_iota(jnp.int32, (sub, tq), 0)
    for gi in range(n_grp):
        @pl.when(gi * sub * SLC_BLOCK <= q0 + tq - 1)
        def _(gi=gi):
            blocks = [score_scr[v * sub:(v + 1) * sub, :] for v in range(n_grp)]
            counts = [cnt_scr[v * sub:(v + 1) * sub, :] for v in range(n_grp)]
            for ii in range(gi * sub, (gi + 1) * sub):
                rowv = score_scr[ii:ii + 1, :]
                for v, blk in enumerate(blocks):
                    if (v + 1) * sub - 1 <= ii:
                        beat = jnp.where(rowv > blk, 1.0, 0.0)
                    elif v * sub > ii:
                        beat = jnp.where(rowv >= blk, 1.0, 0.0)
                    else:
                        beat = jnp.where(jsub + v * sub > ii, jnp.where(rowv >= blk, 1.0, 0.0),
                                         jnp.where(rowv > blk, 1.0, 0.0))
                    counts[v] = counts[v] + beat
            for v, cnt in enumerate(counts):
                cnt_scr[v * sub:(v + 1) * sub, :] = cnt
    n_top = min(SLC_TOP_N, n_slc)
    sel_scr[...] = jnp.where(cnt_scr[...] < n_top, 1.0, 0.0)

    acc_scr[...] = jnp.zeros(acc_scr.shape, F32)
    blk_per_tile = tk // ch
    t_b = q0 + lax.broadcasted_iota(jnp.int32, (ch, tq), 1)
    k_b = lax.broadcasted_iota(jnp.int32, (ch, tq), 0)
    n_full = q0 // tk

    def sel_step(kt, m_prev, diagonal):
        k0 = pl.multiple_of(kt * tk, tk)
        keys = ks_ref[pl.ds(k0, tk), :]
        sel_bias = [(sel_scr[pl.ds(kt * blk_per_tile + bb, 1), :] - 1.0) * (-NEG_INF)
                    for bb in range(blk_per_tile)]
        n_pairs = rep // PAIR
        scores = {1: _dot(keys, qt[:, pair_cols(1)])}
        m_out = []
        for pr in range(n_pairs):
            alphas = []
            for h in range(PAIR):
                r = pr * PAIR + h

                def chunk(bb, pr=pr, h=h):
                    rows_bb = slice(bb * ch, (bb + 1) * ch)
                    x = (s0_scr[rows_bb, head_cols(h)] if pr == 0 else scores[pr][rows_bb, head_cols(h)])
                    x = x + sel_bias[bb]
                    if diagonal:
                        x = jnp.where(k_b + (k0 + bb * ch) <= t_b, x, NEG_INF)
                    return x

                m_new = chunked_softmax(chunk, blk_per_tile, m_prev[r], pr % 2, head_cols(h))
                alphas.append(jnp.exp2(m_prev[r] - m_new))
                m_out.append(m_new)
            pv = _dot(vst_ref[:, pl.ds(k0, tk)], p_scr[pr % 2, 0:tk, :])
            for h in range(PAIR):
                cols = head_cols(pr * PAIR + h)
                acc_scr[:, cols] = alphas[h] * acc_scr[:, cols] + pv[:, head_cols(h)]
            if pr + 2 < n_pairs:
                scores[pr + 2] = _dot(keys, qt[:, pair_cols(pr + 2)])
            elif pr + 2 == n_pairs and not diagonal:
                k1 = pl.multiple_of(k0 + tk, tk)
                s0_scr[...] = _dot(ks_ref[pl.ds(k1, tk), :], qt[:, pair_cols(0)])
        return tuple(m_out)

    m_run = lax.fori_loop(0, n_full, lambda kt, m: sel_step(kt, m, False),
                          tuple(jnp.full((1, tq), NEG_INF, F32) for _ in range(rep)))
    sel_step(n_full, m_run, True)

    for r in range(rep):
        cols = slice(r * tq, (r + 1) * tq)
        out_t = base_scr[:, cols] + (gate_row(1, r) / acc_scr[d:d + 1, cols]) * acc_scr[0:d, cols]
        o_ref[:, r * d:(r + 1) * d] = _transpose_bf16(out_t.astype(BF16)).astype(o_ref.dtype)


def _nsa_attend(qt, gt, kc, vct, ks, vst, kw, vwt, ovl_t, batch, seq):
    d = NSA_HEAD_DIM
    g_n = NSA_KV_HEADS
    tq = NSA_TILE
    nq = seq // tq
    rep = qt.shape[3] // seq
    n_chunk = kc.shape[2]
    n_slc = ovl_t.shape[0]
    gate_rows = -(-3 * g_n * rep // SUBLANES) * SUBLANES

    def kv(shape):
        return pl.BlockSpec((None, None) + shape, lambda b, g, i: (b, g, 0, 0))

    return pl.pallas_call(
        _nsa_kernel,
        grid=(batch, g_n, nq),
        in_specs=[
            pl.BlockSpec((None, None, d, rep * tq), lambda b, g, i: (b, g, 0, i)),
            pl.BlockSpec((None, gate_rows, tq), lambda b, g, i: (b, 0, i)),
            kv((n_chunk, d)), kv((d, n_chunk)),
            kv((seq, d)), kv((NSA_V_ROWS, seq)), kv((seq, d)), kv((NSA_V_ROWS, seq)),
            pl.BlockSpec(ovl_t.shape, lambda b, g, i: (0, 0)),
        ],
        out_specs=pl.BlockSpec((tq, rep * d), lambda b, g, i: (b * nq + i, g)),
        out_shape=jax.ShapeDtypeStruct((batch * seq, g_n * rep * d), BF16),
        scratch_shapes=[
            pltpu.VMEM((n_slc, tq), F32),
            pltpu.VMEM((n_slc, tq), F32),
            pltpu.VMEM((n_slc, tq), F32),
            pltpu.VMEM((d, rep * tq), F32),
            pltpu.VMEM((NSA_V_ROWS, rep * tq), F32),
            pltpu.VMEM((SEL_KEY_TILE, PAIR * tq), F32),
            pltpu.VMEM((rep // PAIR, NSA_WINDOW + tq, PAIR * tq), F32),
            pltpu.VMEM((NSA_WINDOW + tq, tq), F32),
            pltpu.VMEM((2, max(NSA_WINDOW + tq, SEL_KEY_TILE), PAIR * tq), BF16),
        ],
        compiler_params=_params("parallel", "parallel", "arbitrary"),
        name="nsa_attention",
    )(qt, gt, kc, vct, ks, vst, kw, vwt, ovl_t)


def _rope_table(pos, head_dim):
    rd = head_dim // ROPE_FRACTION
    half = rd // 2
    inv = 1.0 / (ROPE_THETA ** (jnp.arange(half, dtype=F32) * (2.0 / rd)))
    ang = pos.astype(F32)[:, None] * inv
    cos, sin = jnp.cos(ang), jnp.sin(ang)
    n = pos.shape[0]
    ones = jnp.ones((n, head_dim - rd), F32)
    zeros = jnp.zeros((n, head_dim - rd), F32)
    zh = jnp.zeros((n, half), F32)
    c = jnp.concatenate([cos, cos, ones], axis=1)
    s1 = jnp.concatenate([-sin, zh, zeros], axis=1)
    s2 = jnp.concatenate([zh, sin, zeros], axis=1)
    reps = LANES // head_dim
    return jnp.concatenate([jnp.tile(c, (1, reps)), jnp.tile(s1, (1, reps)), jnp.tile(s2, (1, reps))], axis=1)


def _overlap_matrix(seq):
    n_cmp_pad = seq // CMP_STRIDE
    n_slc = seq // SLC_BLOCK
    cs = np.arange(n_cmp_pad)[None, :] * CMP_STRIDE
    ss = np.arange(n_slc)[:, None] * SLC_BLOCK
    ovl_t = ((cs < ss + SLC_BLOCK) & (cs + CMP_BLOCK > ss)).astype(np.float32)
    return jnp.asarray(ovl_t, BF16)


def kernel(x, positions, ffn1_norm, ffn1_w_in, ffn1_w_out, mix_norm, ffn2_norm, ffn2_w_in, ffn2_w_out,
           swa_w_in, swa_q_norm, swa_k_norm, swa_sinks, swa_w_out,
           nsa_w_in, nsa_q_norm, nsa_k_norm, nsa_cmp_pe_k, nsa_cmp_w1_k, nsa_cmp_w2_k,
           nsa_cmp_pe_v, nsa_cmp_w1_v, nsa_cmp_w2_v, nsa_w_out):
    batch, seq, d_model = x.shape
    depth = ffn1_norm.shape[0]
    m = batch * seq
    h = x.reshape(m, d_model)

    f1_in, f1_out = ffn1_w_in.astype(BF16), ffn1_w_out.astype(BF16)
    f2_in, f2_out = ffn2_w_in.astype(BF16), ffn2_w_out.astype(BF16)
    swa_in, swa_out = swa_w_in.astype(BF16), swa_w_out.astype(BF16)
    nsa_n = nsa_w_in.shape[2]
    nsa_tn = NSA_PROJ_TILE
    nsa_pad = -nsa_n % nsa_tn
    nsa_in = jnp.pad(nsa_w_in.astype(BF16), ((0, 0), (0, 0), (0, nsa_pad)))
    nsa_out = nsa_w_out.astype(BF16)
    w1_k, w2_k = nsa_cmp_w1_k.astype(BF16), nsa_cmp_w2_k.astype(BF16)
    w1_v, w2_v = nsa_cmp_w1_v.astype(BF16), nsa_cmp_w2_v.astype(BF16)

    n1 = ffn1_norm.reshape(depth, 1, d_model)
    nm = mix_norm.reshape(depth, 1, d_model)
    n2 = ffn2_norm.reshape(depth, 1, d_model)

    pos_flat = positions.reshape(m)
    rope_swa = _rope_table(pos_flat, SWA_HEAD_DIM)
    rope_nsa = _rope_table(pos_flat, NSA_HEAD_DIM)
    n_chunk = seq // CMP_STRIDE
    end_idx = jnp.minimum(jnp.arange(n_chunk) * CMP_STRIDE + (CMP_BLOCK - 1), seq - 1)
    rope_cmp = _rope_table(positions[:, end_idx].reshape(batch * n_chunk), NSA_HEAD_DIM)
    ovl_t = _overlap_matrix(seq)

    for i in range(depth):
        h = _ffn(h, n1, f1_in, f1_out, i)
        j = i // N_MIXERS
        if i % N_MIXERS == 0:
            p = _proj(h, nm, swa_in, i, j, SWA_PROJ_TILE)
            qn2 = jnp.tile(swa_q_norm[j], LANES // SWA_HEAD_DIM).reshape(1, LANES)
            kn2 = jnp.tile(swa_k_norm[j], LANES // SWA_HEAD_DIM).reshape(1, LANES)
            o = _swa_attention(p, rope_swa, qn2, kn2, swa_sinks[j], batch, seq)
            h = _oproj(h, o, swa_out, j)
        else:
            p = _proj(h, nm, nsa_in, i, j, nsa_tn)
            kn = nsa_k_norm[j].reshape(1, NSA_HEAD_DIM)
            qn = nsa_q_norm[j].reshape(1, NSA_HEAD_DIM)
            kc, vct = _nsa_compress(p, rope_cmp, kn, nsa_cmp_pe_k[j], w1_k[j], w2_k[j],
                                    nsa_cmp_pe_v[j], w1_v[j], w2_v[j], batch, seq)
            qt, ks, vst, kw, vwt = _nsa_qkv(p, rope_nsa, qn, kn, batch, seq)
            gt = _nsa_gates(p, batch, seq)
            o = _nsa_attend(qt, gt, kc, vct, ks, vst, kw, vwt, ovl_t, batch, seq)
            h = _oproj(h, o, nsa_out, j)
        h = _ffn(h, n2, f2_in, f2_out, i)
    return h.reshape(batch, seq, d_model)
```

```python
import math

import jax
import jax.numpy as jnp
import numpy as np
from jax import lax
from jax.experimental import pallas as pl
from jax.experimental.pallas import tpu as pltpu

F32 = jnp.float32
BF16 = jnp.bfloat16

N_MIXERS = 2
SWA_HEAD_DIM = 64
SWA_KV_HEADS = 4
SWA_WINDOW = 128
NSA_HEAD_DIM = 128
NSA_KV_HEADS = 4
CMP_BLOCK = 32
CMP_STRIDE = 16
SLC_BLOCK = 64
SLC_TOP_N = 16
NSA_WINDOW = 512
ROPE_THETA = 500000.0
ROPE_FRACTION = 4
NORM_EPS = 1e-6
NEG_INF = -1e30
FORCE_SCORE = 1e9
TINY = 1e-30
LOG2E = math.log2(math.e)

LANES = 128
SUBLANES = 8
VMEM_LIMIT = 56 * 1024 * 1024

ROW_TILE = 512
FFN_ROW_TILE = 1024
SWA_PROJ_TILE = 1280
NSA_PROJ_TILE = 1792
FFN_TILE = 512
FFN_SPLIT = 2
SWA_TILE = 128
SWA_SKEW = 2
NSA_TILE = 256
NSA_PREP_TILE = 1024
NSA_V_ROWS = NSA_HEAD_DIM + 16
NSA_QSCALE = (NSA_HEAD_DIM ** -0.5) * LOG2E
PAIR = 2
SEL_KEY_TILE = 512


def _params(*sem):
    return pltpu.CompilerParams(dimension_semantics=sem, vmem_limit_bytes=VMEM_LIMIT)


def _dot(a, b):
    return jnp.dot(a, b, preferred_element_type=F32)


def _dot_nt(a, b):
    return lax.dot_general(a, b, (((1,), (1,)), ((), ())), preferred_element_type=F32)


def _eye(n):
    r = lax.broadcasted_iota(jnp.int32, (n, n), 0)
    c = lax.broadcasted_iota(jnp.int32, (n, n), 1)
    return jnp.where(r == c, 1.0, 0.0).astype(BF16)


def _transpose_bf16(x):
    return _dot_nt(_eye(x.shape[1]), x).astype(BF16)


def _software_pipeline(n_items, stages, skew):
    state = [{} for _ in range(n_items)]
    for t in range(n_items + skew * (len(stages) - 1)):
        for k, stage in enumerate(stages):
            c = t - k * skew
            if 0 <= c < n_items:
                stage(c, state[c])


def _rms(x, g):
    ms = jnp.mean(x * x, axis=-1, keepdims=True)
    return x * lax.rsqrt(ms + NORM_EPS) * g


def _sigmoid(x):
    return 1.0 / (1.0 + jnp.exp(-x))


def _rope(xn, rope, half):
    c, s1, s2 = rope[:, 0:LANES], rope[:, LANES:2 * LANES], rope[:, 2 * LANES:3 * LANES]
    return xn * c + pltpu.roll(xn, LANES - half, 1) * s1 + pltpu.roll(xn, half, 1) * s2


def _ffn_kernel(x_ref, g_ref, wg_ref, wu_ref, wo_ref, o_ref, xn_ref):
    @pl.when(pl.program_id(1) == 0)
    def _():
        x = x_ref[...]
        xn_ref[...] = _rms(x, g_ref[...]).astype(BF16)
        o_ref[...] = x

    xn = xn_ref[...]
    w = wg_ref.shape[1] // FFN_SPLIT
    pre = [(_dot(xn, wg_ref[:, s * w:(s + 1) * w]), _dot(xn, wu_ref[:, s * w:(s + 1) * w]))
           for s in range(FFN_SPLIT)]
    upd = None
    for s, (gate, up) in enumerate(pre):
        hdn = (gate * _sigmoid(gate) * (0.5 * up)).astype(BF16)
        part = _dot(hdn, wo_ref[s * w:(s + 1) * w, :])
        upd = part if upd is None else upd + part
    o_ref[...] += upd


def _ffn(h, norm, w_in, w_out, layer):
    m, d = h.shape
    f = w_out.shape[1]
    nf = f // FFN_TILE
    return pl.pallas_call(
        _ffn_kernel,
        grid=(m // FFN_ROW_TILE, nf),
        in_specs=[
            pl.BlockSpec((FFN_ROW_TILE, d), lambda i, j: (i, 0)),
            pl.BlockSpec((None, 1, d), lambda i, j: (layer, 0, 0)),
            pl.BlockSpec((None, d, FFN_TILE), lambda i, j: (layer, 0, j)),
            pl.BlockSpec((None, d, FFN_TILE), lambda i, j: (layer, 0, nf + j)),
            pl.BlockSpec((None, FFN_TILE, d), lambda i, j: (layer, j, 0)),
        ],
        out_specs=pl.BlockSpec((FFN_ROW_TILE, d), lambda i, j: (i, 0)),
        out_shape=jax.ShapeDtypeStruct((m, d), F32),
        scratch_shapes=[pltpu.VMEM((FFN_ROW_TILE, d), BF16)],
        compiler_params=_params("parallel", "arbitrary"),
        name="ffn",
    )(h, norm, w_in, w_in, w_out)


def _proj_kernel(x_ref, g_ref, w_ref, o_ref, xn_ref):
    @pl.when(pl.program_id(1) == 0)
    def _():
        xn_ref[...] = _rms(x_ref[...], g_ref[...]).astype(BF16)

    o_ref[...] = _dot(xn_ref[...], w_ref[...])


def _proj(h, norm, w, layer, wl, tn):
    m, d = h.shape
    n = w.shape[2]
    return pl.pallas_call(
        _proj_kernel,
        grid=(m // ROW_TILE, n // tn),
        in_specs=[
            pl.BlockSpec((ROW_TILE, d), lambda i, j: (i, 0)),
            pl.BlockSpec((None, 1, d), lambda i, j: (layer, 0, 0)),
            pl.BlockSpec((None, d, tn), lambda i, j: (wl, 0, j)),
        ],
        out_specs=pl.BlockSpec((ROW_TILE, tn), lambda i, j: (i, j)),
        out_shape=jax.ShapeDtypeStruct((m, n), F32),
        scratch_shapes=[pltpu.VMEM((ROW_TILE, d), BF16)],
        compiler_params=_params("parallel", "arbitrary"),
        name="mixer_in_proj",
    )(h, norm, w)


def _oproj_kernel(h_ref, o_ref, w_ref, out_ref):
    out_ref[...] = h_ref[...] + _dot(o_ref[...], w_ref[...])


def _oproj(h, o, w, wl):
    m, d = h.shape
    k = o.shape[1]
    return pl.pallas_call(
        _oproj_kernel,
        grid=(m // ROW_TILE,),
        in_specs=[
            pl.BlockSpec((ROW_TILE, d), lambda i: (i, 0)),
            pl.BlockSpec((ROW_TILE, k), lambda i: (i, 0)),
            pl.BlockSpec((None, k, d), lambda i: (wl, 0, 0)),
        ],
        out_specs=pl.BlockSpec((ROW_TILE, d), lambda i: (i, 0)),
        out_shape=jax.ShapeDtypeStruct((m, d), F32),
        compiler_params=_params("parallel"),
        name="mixer_out_proj",
    )(h, o, w)


def _swa_kernel(sink_ref, q_ref, kvc_ref, kvp_ref, rc_ref, rp_ref, qn_ref, kn_ref, o_ref):
    i = pl.program_id(1)
    tq = q_ref.shape[0]
    hd = SWA_HEAD_DIM
    half = hd // ROPE_FRACTION // 2
    n_chunks = q_ref.shape[1] // LANES
    heads_per_group = (2 * n_chunks) // SWA_KV_HEADS
    kv_w = SWA_KV_HEADS * hd
    n_kv_chunks = kv_w // LANES
    c1 = (hd ** -0.5) * LOG2E


    row = lax.broadcasted_iota(jnp.int32, (LANES, LANES), 0)
    col = lax.broadcasted_iota(jnp.int32, (LANES, LANES), 1)
    seg = jnp.where((row // hd) == (col // hd), 1.0 / hd, 0.0).astype(BF16)
    eye = _eye(LANES)

    def mean_sq(x):
        xx = x * x
        hi = xx.astype(BF16)
        lo = (xx - hi.astype(F32)).astype(BF16)
        return _dot(hi, seg) + _dot(lo, seg)

    def norm_rope(x, gn, rope):
        return _rope(x * lax.rsqrt(mean_sq(x) + NORM_EPS) * gn, rope, half)

    low = lax.broadcasted_iota(jnp.int32, (2 * tq, LANES), 1) < hd
    top = lax.broadcasted_iota(jnp.int32, (LANES, 2 * tq), 0) < hd
    kn = kn_ref[...]
    k_chunks = []
    for c in range(n_kv_chunks):
        sl = slice(c * LANES, (c + 1) * LANES)
        k_chunks.append(jnp.concatenate([norm_rope(kvp_ref[:, sl], kn, rp_ref[...]),
                                         norm_rope(kvc_ref[:, sl], kn, rc_ref[...])], axis=0))
    v_all = jnp.concatenate([kvp_ref[:, kv_w:2 * kv_w], kvc_ref[:, kv_w:2 * kv_w]], axis=0).astype(BF16)
    vt = _dot_nt(_eye(kv_w), v_all)

    kd, vlo, vhi = [], [], []
    for g in range(SWA_KV_HEADS):
        kc_ = k_chunks[g // 2]
        kc_sw = pltpu.roll(kc_, hd, 1)
        vc_ = vt[(g // 2) * LANES:(g // 2 + 1) * LANES, :]
        vc_sw = pltpu.roll(vc_, hd, 0)
        if g % 2 == 0:
            kd.append(jnp.where(low, kc_, kc_sw).astype(BF16))
            vlo.append(jnp.where(top, vc_, 1.0).astype(BF16))
            vhi.append(jnp.where(top, 1.0, vc_sw).astype(BF16))
        else:
            kd.append(jnp.where(low, kc_sw, kc_).astype(BF16))
            vlo.append(jnp.where(top, vc_sw, 1.0).astype(BF16))
            vhi.append(jnp.where(top, 1.0, vc_).astype(BF16))

    ki = lax.broadcasted_iota(jnp.int32, (2 * tq, tq), 0)
    qi = lax.broadcasted_iota(jnp.int32, (2 * tq, tq), 1)
    rel = qi + tq - ki
    mask = (rel >= 0) & (rel < SWA_WINDOW) & ((ki >= tq) | (i > 0))
    qtop = lax.broadcasted_iota(jnp.int32, (LANES, tq), 0) < hd
    qn = qn_ref[...]

    def st_load(c, st):
        st["x"] = q_ref[:, c * LANES:(c + 1) * LANES]
        st["ms"] = mean_sq(st["x"])

    def st_query(c, st):
        xq = (_rope(st["x"] * lax.rsqrt(st["ms"] + NORM_EPS) * qn, rc_ref[...], half) * c1).astype(BF16)
        st["qt"] = _dot_nt(eye, xq)

    def st_scores(c, st):
        g = (2 * c) // heads_per_group
        qt = st["qt"]
        st["s"] = [_dot(kd[g], jnp.where(qtop, qt, 0.0).astype(BF16)),
                   _dot(kd[g], jnp.where(qtop, 0.0, qt).astype(BF16))]

    def st_values(c, st):
        g = (2 * c) // heads_per_group
        st["pv"] = []
        for par, vpl in ((0, vlo[g]), (1, vhi[g])):
            s = jnp.where(mask, st["s"][par], NEG_INF)
            sink = sink_ref[2 * c + par] * LOG2E
            mx = jnp.maximum(jnp.max(s, axis=0, keepdims=True), sink)
            pv = _dot(vpl, jnp.exp2(s - mx).astype(BF16))
            den = (pv[hd:hd + 1, :] if par == 0 else pv[0:1, :]) + jnp.exp2(sink - mx)
            st["pv"].append(pv * (1.0 / den))

    def st_merge(c, st):
        ot = jnp.where(qtop, st["pv"][0], st["pv"][1]).astype(BF16)
        st["o"] = _dot_nt(eye, ot)

    def st_store(c, st):
        o_ref[:, c * LANES:(c + 1) * LANES] = st["o"].astype(o_ref.dtype)

    _software_pipeline(n_chunks, (st_load, st_query, st_scores, st_values, st_merge, st_store), SWA_SKEW)


def _swa_attention(p, rope, q_norm2, k_norm2, sinks, batch, seq):
    m = p.shape[0]
    tq = SWA_TILE
    nb = seq // tq
    n_q = sinks.shape[0] * SWA_HEAD_DIM
    kv_w = 2 * SWA_KV_HEADS * SWA_HEAD_DIM
    kv_blk = n_q // kv_w
    return pl.pallas_call(
        _swa_kernel,
        grid=(batch, nb),
        in_specs=[
            pl.BlockSpec(memory_space=pltpu.SMEM),
            pl.BlockSpec((tq, n_q), lambda b, i: (b * nb + i, 0)),
            pl.BlockSpec((tq, kv_w), lambda b, i: (b * nb + i, kv_blk)),
            pl.BlockSpec((tq, kv_w), lambda b, i: (b * nb + jnp.maximum(i - 1, 0), kv_blk)),
            pl.BlockSpec((tq, 3 * LANES), lambda b, i: (b * nb + i, 0)),
            pl.BlockSpec((tq, 3 * LANES), lambda b, i: (b * nb + jnp.maximum(i - 1, 0), 0)),
            pl.BlockSpec((1, LANES), lambda b, i: (0, 0)),
            pl.BlockSpec((1, LANES), lambda b, i: (0, 0)),
        ],
        out_specs=pl.BlockSpec((tq, n_q), lambda b, i: (b * nb + i, 0)),
        out_shape=jax.ShapeDtypeStruct((m, n_q), BF16),
        compiler_params=_params("parallel", "arbitrary"),
        name="swa_attention",
    )(sinks, p, p, p, rope, rope, q_norm2, k_norm2)


def _nsa_compress_kernel(kc_ref, vc_ref, ropec_ref, kn_ref, pek_ref, w1k_ref, w2k_ref,
                         pev_ref, w1v_ref, w2v_ref, kco_ref, vco_ref):
    half = NSA_HEAD_DIM // ROPE_FRACTION // 2
    n_chunk = kc_ref.shape[0] // CMP_STRIDE
    per = CMP_BLOCK // CMP_STRIDE

    def compress(t_ref, pe_ref, w1_ref, w2_ref):
        parts = []
        for h in range(per):
            acc = jnp.zeros((n_chunk, w1_ref.shape[2]), F32)
            for l in range(CMP_STRIDE):
                ll = h * CMP_STRIDE + l
                xl = t_ref[pl.ds(l, n_chunk, stride=CMP_STRIDE), :] + pe_ref[ll:ll + 1, :]
                acc = acc + _dot(xl.astype(BF16), w1_ref[ll])
            parts.append(acc)
        pre = parts[0]
        for h in range(1, per):
            pre = pre + pltpu.roll(parts[h], n_chunk - h, 0)
        hdn = pre * _sigmoid(pre)
        return _dot(hdn.astype(BF16), w2_ref[...])

    kcmp = compress(kc_ref, pek_ref, w1k_ref, w2k_ref)
    kco_ref[...] = _rope(_rms(kcmp, kn_ref[...]), ropec_ref[...], half).astype(BF16)
    vco_ref[...] = _transpose_bf16(compress(vc_ref, pev_ref, w1v_ref, w2v_ref).astype(BF16))


def _nsa_compress(p, rope_c, k_norm, pe_k, w1_k, w2_k, pe_v, w1_v, w2_v, batch, seq):
    d = NSA_HEAD_DIM
    g_n = NSA_KV_HEADS
    n_chunk = seq // CMP_STRIDE
    q_blocks = 4 * g_n

    def col(kind):
        return pl.BlockSpec((seq, d), lambda b, g, kind=kind: (b, q_blocks + kind * g_n + g))

    def full(a):
        return pl.BlockSpec(a.shape, lambda b, g, nd=a.ndim: (0,) * nd)

    return pl.pallas_call(
        _nsa_compress_kernel,
        grid=(batch, g_n),
        in_specs=[col(0), col(1),
                  pl.BlockSpec((n_chunk, 3 * LANES), lambda b, g: (b, 0)),
                  full(k_norm), full(pe_k), full(w1_k), full(w2_k), full(pe_v), full(w1_v), full(w2_v)],
        out_specs=[pl.BlockSpec((None, None, n_chunk, d), lambda b, g: (b, g, 0, 0)),
                   pl.BlockSpec((None, None, d, n_chunk), lambda b, g: (b, g, 0, 0))],
        out_shape=[jax.ShapeDtypeStruct((batch, g_n, n_chunk, d), BF16),
                   jax.ShapeDtypeStruct((batch, g_n, d, n_chunk), BF16)],
        compiler_params=_params("parallel", "arbitrary"),
        name="nsa_compress",
    )(p, p, rope_c, k_norm, pe_k, w1_k, w2_k, pe_v, w1_v, w2_v)


def _nsa_qkv_kernel(q_ref, ks_ref, vs_ref, kw_ref, vw_ref, rope_ref, qn_ref, kn_ref,
                    qt_ref, kso_ref, vst_ref, kwo_ref, vwt_ref):
    d = NSA_HEAD_DIM
    half = d // ROPE_FRACTION // 2
    rep = q_ref.shape[1] // d
    tq = NSA_TILE
    eye = _eye(d)
    for s in range(q_ref.shape[0] // tq):
        rows = slice(s * tq, (s + 1) * tq)
        for r in range(rep):
            qr = _rope(_rms(q_ref[rows, r * d:(r + 1) * d], qn_ref[...]), rope_ref[rows, :], half) * NSA_QSCALE
            qt_ref[:, (s * rep + r) * tq:(s * rep + r + 1) * tq] = _dot_nt(eye, qr.astype(BF16)).astype(BF16)
    kso_ref[...] = _rope(_rms(ks_ref[...], kn_ref[...]), rope_ref[...], half).astype(BF16)
    kwo_ref[...] = _rope(_rms(kw_ref[...], kn_ref[...]), rope_ref[...], half).astype(BF16)
    ones = jnp.ones((NSA_V_ROWS - d, vs_ref.shape[0]), BF16)
    vst_ref[0:d, :] = _dot_nt(eye, vs_ref[...].astype(BF16)).astype(BF16)
    vst_ref[d:NSA_V_ROWS, :] = ones
    vwt_ref[0:d, :] = _dot_nt(eye, vw_ref[...].astype(BF16)).astype(BF16)
    vwt_ref[d:NSA_V_ROWS, :] = ones


def _nsa_qkv(p, rope, q_norm, k_norm, batch, seq):
    d = NSA_HEAD_DIM
    g_n = NSA_KV_HEADS
    rep = 4
    tt = NSA_PREP_TILE
    nt = seq // tt
    q_blocks = rep * g_n

    def col(kind):
        return pl.BlockSpec((tt, d), lambda b, g, t, kind=kind: (b * nt + t, q_blocks + kind * g_n + g))

    nat = pl.BlockSpec((None, None, tt, d), lambda b, g, t: (b, g, t, 0))
    tra = pl.BlockSpec((None, None, NSA_V_ROWS, tt), lambda b, g, t: (b, g, 0, t))
    sh_nat = jax.ShapeDtypeStruct((batch, g_n, seq, d), BF16)
    sh_tra = jax.ShapeDtypeStruct((batch, g_n, NSA_V_ROWS, seq), BF16)
    return pl.pallas_call(
        _nsa_qkv_kernel,
        grid=(batch, g_n, nt),
        in_specs=[pl.BlockSpec((tt, rep * d), lambda b, g, t: (b * nt + t, g)),
                  col(2), col(3), col(4), col(5),
                  pl.BlockSpec((tt, 3 * LANES), lambda b, g, t: (b * nt + t, 0)),
                  pl.BlockSpec((1, d), lambda b, g, t: (0, 0)),
                  pl.BlockSpec((1, d), lambda b, g, t: (0, 0))],
        out_specs=[pl.BlockSpec((None, None, d, rep * tt), lambda b, g, t: (b, g, 0, t)), nat, tra, nat, tra],
        out_shape=[jax.ShapeDtypeStruct((batch, g_n, d, rep * seq), BF16), sh_nat, sh_tra, sh_nat, sh_tra],
        compiler_params=_params("parallel", "parallel", "arbitrary"),
        name="nsa_qkv_prep",
    )(p, p, p, p, p, rope, q_norm, k_norm)


def _nsa_gate_kernel(g_ref, o_ref):
    o_ref[...] = _sigmoid(g_ref[...]).T


def _nsa_gates(p, batch, seq):
    tt = NSA_PREP_TILE
    nt = seq // tt
    gate_blk = 10 * NSA_KV_HEADS
    return pl.pallas_call(
        _nsa_gate_kernel,
        grid=(batch, nt),
        in_specs=[pl.BlockSpec((tt, LANES), lambda b, t: (b * nt + t, gate_blk))],
        out_specs=pl.BlockSpec((None, LANES, tt), lambda b, t: (b, 0, t)),
        out_shape=jax.ShapeDtypeStruct((batch, LANES, seq), F32),
        compiler_params=_params("parallel", "arbitrary"),
        name="nsa_gates",
    )(p)


def _nsa_kernel(qt_ref, gt_ref, kc_ref, vct_ref, ks_ref, vst_ref, kw_ref, vwt_ref, ovl_ref, o_ref,
                sel_scr, score_scr, cnt_scr, base_scr, acc_scr, s0_scr, sw_scr, bias_scr, p_scr):
    g = pl.program_id(1)
    i = pl.program_id(2)
    d = NSA_HEAD_DIM
    rows = qt_ref.shape[1]
    tq = o_ref.shape[0]
    rep = rows // tq
    n_cmp = kc_ref.shape[0]
    n_slc = ovl_ref.shape[0]
    n_heads = NSA_KV_HEADS * rep
    tk = SEL_KEY_TILE
    ch = SLC_BLOCK
    q0 = i * tq
    qt = qt_ref[...]

    def head_cols(r):
        return slice(r * tq, (r + 1) * tq)

    def pair_cols(pr):
        return slice(pr * PAIR * tq, (pr + 1) * PAIR * tq)

    def gate_row(branch, r):
        return gt_ref[pl.ds(branch * n_heads + g * rep + r, 1), :]

    def fold_max(x):
        parts = [x[j * SUBLANES:(j + 1) * SUBLANES, :] for j in range(x.shape[0] // SUBLANES)]
        while len(parts) > 1:
            parts = [jnp.maximum(parts[j], parts[j + 1]) for j in range(0, len(parts), 2)]
        return parts[0]

    def chunked_softmax(chunk, n_chunks, m_floor, p_slot, p_cols):
        m8 = fold_max(chunk(0))
        for c in range(1, n_chunks):
            m8 = jnp.maximum(m8, fold_max(chunk(c)))
        m = jnp.max(m8, axis=0, keepdims=True)
        if m_floor is not None:
            m = jnp.maximum(m, m_floor)
        for c in range(n_chunks):
            p_scr[p_slot, c * ch:(c + 1) * ch, p_cols] = jnp.exp2(chunk(c) - m).astype(BF16)
        return m

    wk = NSA_WINDOW + tq
    start = pl.multiple_of(jnp.maximum(q0 - NSA_WINDOW, 0), tq)

    @pl.when(i <= NSA_WINDOW // tq)
    def _():
        for c in range(wk // ch):
            t_w = q0 + lax.broadcasted_iota(jnp.int32, (ch, tq), 1)
            k_w = start + c * ch + lax.broadcasted_iota(jnp.int32, (ch, tq), 0)
            ok = (k_w <= t_w) & (t_w - k_w < NSA_WINDOW)
            bias_scr[c * ch:(c + 1) * ch, :] = jnp.where(ok, 0.0, NEG_INF)

    sw_scr[0] = _dot(kw_ref[pl.ds(start, wk), :], qt[:, pair_cols(0)])
    sc_raw = _dot(kc_ref[...], qt)
    for pr in range(1, rep // PAIR):
        sw_scr[pr] = _dot(kw_ref[pl.ds(start, wk), :], qt[:, pair_cols(pr)])
    s0_scr[...] = _dot(ks_ref[0:tk, :], qt[:, pair_cols(0)])

    t_c = q0 + lax.broadcasted_iota(jnp.int32, (n_cmp, tq), 1)
    end_c = lax.broadcasted_iota(jnp.int32, (n_cmp, tq), 0) * CMP_STRIDE + (CMP_BLOCK - 1)
    mask_c = end_c <= t_c

    imp = jnp.zeros((n_slc, tq), F32)
    for pr in range(rep // PAIR):
        pcs = []
        for h in range(PAIR):
            cols = head_cols(pr * PAIR + h)
            chunked_softmax(lambda c, pr=pr, h=h: (sw_scr[pr, c * ch:(c + 1) * ch, head_cols(h)]
                                                   + bias_scr[c * ch:(c + 1) * ch, :]),
                            wk // ch, None, pr % 2, head_cols(h))
            sc = jnp.where(mask_c, sc_raw[:, cols], NEG_INF)
            ec = jnp.where(mask_c, jnp.exp2(sc - jnp.max(sc, axis=0, keepdims=True)), 0.0)
            pcs.append((ec / jnp.maximum(jnp.sum(ec, axis=0, keepdims=True), TINY)).astype(BF16))
        pc = jnp.concatenate(pcs, axis=1)
        o_win = _dot(vwt_ref[:, pl.ds(start, wk)], p_scr[pr % 2, 0:wk, :])
        o_cmp = _dot(vct_ref[...], pc)
        imp_pair = _dot(ovl_ref[...], pc)
        for h in range(PAIR):
            r = pr * PAIR + h
            hc = head_cols(h)
            imp = imp + imp_pair[:, hc]
            base_scr[:, head_cols(r)] = (gate_row(0, r) * o_cmp[:, hc]
                                         + (gate_row(2, r) / o_win[d:d + 1, hc]) * o_win[0:d, hc])

    jb = lax.broadcasted_iota(jnp.int32, (n_slc, tq), 0)
    tt = q0 + lax.broadcasted_iota(jnp.int32, (n_slc, tq), 1)
    cur = tt // SLC_BLOCK
    forced = (jb == 0) | (jb == cur) | (jb == cur - 1)
    score = jnp.where(forced, FORCE_SCORE, jnp.where(jb * SLC_BLOCK <= tt, imp, NEG_INF))
    sub = SUBLANES
    n_grp = n_slc // sub
    score_scr[...] = score
    cnt_scr[...] = jnp.zeros(cnt_scr.shape, F32)
    jsub = lax.broadcasted_iota(jnp.int32, (sub, tq), 0)
    for gi in range(n_grp):
        @pl.when(gi * sub * SLC_BLOCK <= q0 + tq - 1)
        def _(gi=gi):
            blocks = [score_scr[v * sub:(v + 1) * sub, :] for v in range(n_grp)]
            counts = [cnt_scr[v * sub:(v + 1) * sub, :] for v in range(n_grp)]
            for ii in range(gi * sub, (gi + 1) * sub):
                rowv = score_scr[ii:ii + 1, :]
                for v, blk in enumerate(blocks):
                    if (v + 1) * sub - 1 <= ii:
                        beat = jnp.where(rowv > blk, 1.0, 0.0)
                    elif v * sub > ii:
                        beat = jnp.where(rowv >= blk, 1.0, 0.0)
                    else:
                        beat = jnp.where(jsub + v * sub > ii, jnp.where(rowv >= blk, 1.0, 0.0),
                                         jnp.where(rowv > blk, 1.0, 0.0))
                    counts[v] = counts[v] + beat
            for v, cnt in enumerate(counts):
                cnt_scr[v * sub:(v + 1) * sub, :] = cnt
    n_top = min(SLC_TOP_N, n_slc)
    sel_scr[...] = jnp.where(cnt_scr[...] < n_top, 1.0, 0.0)

    acc_scr[...] = jnp.zeros(acc_scr.shape, F32)
    blk_per_tile = tk // ch
    t_b = q0 + lax.broadcasted_iota(jnp.int32, (ch, tq), 1)
    k_b = lax.broadcasted_iota(jnp.int32, (ch, tq), 0)
    n_full = q0 // tk

    def sel_step(kt, m_prev, diagonal):
        k0 = pl.multiple_of(kt * tk, tk)
        keys = ks_ref[pl.ds(k0, tk), :]
        sel_bias = [(sel_scr[pl.ds(kt * blk_per_tile + bb, 1), :] - 1.0) * (-NEG_INF)
                    for bb in range(blk_per_tile)]
        n_pairs = rep // PAIR
        scores = {1: _dot(keys, qt[:, pair_cols(1)])}
        m_out = []
        for pr in range(n_pairs):
            alphas = []
            for h in range(PAIR):
                r = pr * PAIR + h

                def chunk(bb, pr=pr, h=h):
                    rows_bb = slice(bb * ch, (bb + 1) * ch)
                    x = (s0_scr[rows_bb, head_cols(h)] if pr == 0 else scores[pr][rows_bb, head_cols(h)])
                    x = x + sel_bias[bb]
                    if diagonal:
                        x = jnp.where(k_b + (k0 + bb * ch) <= t_b, x, NEG_INF)
                    return x

                m_new = chunked_softmax(chunk, blk_per_tile, m_prev[r], pr % 2, head_cols(h))
                alphas.append(jnp.exp2(m_prev[r] - m_new))
                m_out.append(m_new)
            pv = _dot(vst_ref[:, pl.ds(k0, tk)], p_scr[pr % 2, 0:tk, :])
            for h in range(PAIR):
                cols = head_cols(pr * PAIR + h)
                acc_scr[:, cols] = alphas[h] * acc_scr[:, cols] + pv[:, head_cols(h)]
            if pr + 2 < n_pairs:
                scores[pr + 2] = _dot(keys, qt[:, pair_cols(pr + 2)])
            elif pr + 2 == n_pairs and not diagonal:
                k1 = pl.multiple_of(k0 + tk, tk)
                s0_scr[...] = _dot(ks_ref[pl.ds(k1, tk), :], qt[:, pair_cols(0)])
        return tuple(m_out)

    m_run = lax.fori_loop(0, n_full, lambda kt, m: sel_step(kt, m, False),
                          tuple(jnp.full((1, tq), NEG_INF, F32) for _ in range(rep)))
    sel_step(n_full, m_run, True)

    for r in range(rep):
        cols = slice(r * tq, (r + 1) * tq)
        out_t = base_scr[:, cols] + (gate_row(1, r) / acc_scr[d:d + 1, cols]) * acc_scr[0:d, cols]
        o_ref[:, r * d:(r + 1) * d] = _transpose_bf16(out_t.astype(BF16)).astype(o_ref.dtype)


def _nsa_attend(qt, gt, kc, vct, ks, vst, kw, vwt, ovl_t, batch, seq):
    d = NSA_HEAD_DIM
    g_n = NSA_KV_HEADS
    tq = NSA_TILE
    nq = seq // tq
    rep = qt.shape[3] // seq
    n_chunk = kc.shape[2]
    n_slc = ovl_t.shape[0]
    gate_rows = -(-3 * g_n * rep // SUBLANES) * SUBLANES

    def kv(shape):
        return pl.BlockSpec((None, None) + shape, lambda b, g, i: (b, g, 0, 0))

    return pl.pallas_call(
        _nsa_kernel,
        grid=(batch, g_n, nq),
        in_specs=[
            pl.BlockSpec((None, None, d, rep * tq), lambda b, g, i: (b, g, 0, i)),
            pl.BlockSpec((None, gate_rows, tq), lambda b, g, i: (b, 0, i)),
            kv((n_chunk, d)), kv((d, n_chunk)),
            kv((seq, d)), kv((NSA_V_ROWS, seq)), kv((seq, d)), kv((NSA_V_ROWS, seq)),
            pl.BlockSpec(ovl_t.shape, lambda b, g, i: (0, 0)),
        ],
        out_specs=pl.BlockSpec((tq, rep * d), lambda b, g, i: (b * nq + i, g)),
        out_shape=jax.ShapeDtypeStruct((batch * seq, g_n * rep * d), BF16),
        scratch_shapes=[
            pltpu.VMEM((n_slc, tq), F32),
            pltpu.VMEM((n_slc, tq), F32),
            pltpu.VMEM((n_slc, tq), F32),
            pltpu.VMEM((d, rep * tq), F32),
            pltpu.VMEM((NSA_V_ROWS, rep * tq), F32),
            pltpu.VMEM((SEL_KEY_TILE, PAIR * tq), F32),
            pltpu.VMEM((rep // PAIR, NSA_WINDOW + tq, PAIR * tq), F32),
            pltpu.VMEM((NSA_WINDOW + tq, tq), F32),
            pltpu.VMEM((2, max(NSA_WINDOW + tq, SEL_KEY_TILE), PAIR * tq), BF16),
        ],
        compiler_params=_params("parallel", "parallel", "arbitrary"),
        name="nsa_attention",
    )(qt, gt, kc, vct, ks, vst, kw, vwt, ovl_t)


def _rope_table(pos, head_dim):
    rd = head_dim // ROPE_FRACTION
    half = rd // 2
    inv = 1.0 / (ROPE_THETA ** (jnp.arange(half, dtype=F32) * (2.0 / rd)))
    ang = pos.astype(F32)[:, None] * inv
    cos, sin = jnp.cos(ang), jnp.sin(ang)
    n = pos.shape[0]
    ones = jnp.ones((n, head_dim - rd), F32)
    zeros = jnp.zeros((n, head_dim - rd), F32)
    zh = jnp.zeros((n, half), F32)
    c = jnp.concatenate([cos, cos, ones], axis=1)
    s1 = jnp.concatenate([-sin, zh, zeros], axis=1)
    s2 = jnp.concatenate([zh, sin, zeros], axis=1)
    reps = LANES // head_dim
    return jnp.concatenate([jnp.tile(c, (1, reps)), jnp.tile(s1, (1, reps)), jnp.tile(s2, (1, reps))], axis=1)


def _overlap_matrix(seq):
    n_cmp_pad = seq // CMP_STRIDE
    n_slc = seq // SLC_BLOCK
    cs = np.arange(n_cmp_pad)[None, :] * CMP_STRIDE
    ss = np.arange(n_slc)[:, None] * SLC_BLOCK
    ovl_t = ((cs < ss + SLC_BLOCK) & (cs + CMP_BLOCK > ss)).astype(np.float32)
    return jnp.asarray(ovl_t, BF16)


def kernel(x, positions, ffn1_norm, ffn1_w_in, ffn1_w_out, mix_norm, ffn2_norm, ffn2_w_in, ffn2_w_out,
           swa_w_in, swa_q_norm, swa_k_norm, swa_sinks, swa_w_out,
           nsa_w_in, nsa_q_norm, nsa_k_norm, nsa_cmp_pe_k, nsa_cmp_w1_k, nsa_cmp_w2_k,
           nsa_cmp_pe_v, nsa_cmp_w1_v, nsa_cmp_w2_v, nsa_w_out):
    batch, seq, d_model = x.shape
    depth = ffn1_norm.shape[0]
    m = batch * seq
    h = x.reshape(m, d_model)

    f1_in, f1_out = ffn1_w_in.astype(BF16), ffn1_w_out.astype(BF16)
    f2_in, f2_out = ffn2_w_in.astype(BF16), ffn2_w_out.astype(BF16)
    swa_in, swa_out = swa_w_in.astype(BF16), swa_w_out.astype(BF16)
    nsa_n = nsa_w_in.shape[2]
    nsa_tn = NSA_PROJ_TILE
    nsa_pad = -nsa_n % nsa_tn
    nsa_in = jnp.pad(nsa_w_in.astype(BF16), ((0, 0), (0, 0), (0, nsa_pad)))
    nsa_out = nsa_w_out.astype(BF16)
    w1_k, w2_k = nsa_cmp_w1_k.astype(BF16), nsa_cmp_w2_k.astype(BF16)
    w1_v, w2_v = nsa_cmp_w1_v.astype(BF16), nsa_cmp_w2_v.astype(BF16)

    n1 = ffn1_norm.reshape(depth, 1, d_model)
    nm = mix_norm.reshape(depth, 1, d_model)
    n2 = ffn2_norm.reshape(depth, 1, d_model)

    pos_flat = positions.reshape(m)
    rope_swa = _rope_table(pos_flat, SWA_HEAD_DIM)
    rope_nsa = _rope_table(pos_flat, NSA_HEAD_DIM)
    n_chunk = seq // CMP_STRIDE
    end_idx = jnp.minimum(jnp.arange(n_chunk) * CMP_STRIDE + (CMP_BLOCK - 1), seq - 1)
    rope_cmp = _rope_table(positions[:, end_idx].reshape(batch * n_chunk), NSA_HEAD_DIM)
    ovl_t = _overlap_matrix(seq)

    for i in range(depth):
        h = _ffn(h, n1, f1_in, f1_out, i)
        j = i // N_MIXERS
        if i % N_MIXERS == 0:
            p = _proj(h, nm, swa_in, i, j, SWA_PROJ_TILE)
            qn2 = jnp.tile(swa_q_norm[j], LANES // SWA_HEAD_DIM).reshape(1, LANES)
            kn2 = jnp.tile(swa_k_norm[j], LANES // SWA_HEAD_DIM).reshape(1, LANES)
            o = _swa_attention(p, rope_swa, qn2, kn2, swa_sinks[j], batch, seq)
            h = _oproj(h, o, swa_out, j)
        else:
            p = _proj(h, nm, nsa_in, i, j, nsa_tn)
            kn = nsa_k_norm[j].reshape(1, NSA_HEAD_DIM)
            qn = nsa_q_norm[j].reshape(1, NSA_HEAD_DIM)
            kc, vct = _nsa_compress(p, rope_cmp, kn, nsa_cmp_pe_k[j], w1_k[j], w2_k[j],
                                    nsa_cmp_pe_v[j], w1_v[j], w2_v[j], batch, seq)
            qt, ks, vst, kw, vwt = _nsa_qkv(p, rope_nsa, qn, kn, batch, seq)
            gt = _nsa_gates(p, batch, seq)
            o = _nsa_attend(qt, gt, kc, vct, ks, vst, kw, vwt, ovl_t, batch, seq)
            h = _oproj(h, o, nsa_out, j)
        h = _ffn(h, n2, f2_in, f2_out, i)
    return h.reshape(batch, seq, d_model)
```

```python
import math

import jax
import jax.numpy as jnp
import numpy as np
from jax import lax
from jax.experimental import pallas as pl
from jax.experimental.pallas import tpu as pltpu

F32 = jnp.float32
BF16 = jnp.bfloat16

N_MIXERS = 2
SWA_HEAD_DIM = 64
SWA_KV_HEADS = 4
SWA_WINDOW = 128
NSA_HEAD_DIM = 128
NSA_KV_HEADS = 4
CMP_BLOCK = 32
CMP_STRIDE = 16
SLC_BLOCK = 64
SLC_TOP_N = 16
NSA_WINDOW = 512
ROPE_THETA = 500000.0
ROPE_FRACTION = 4
NORM_EPS = 1e-6
NEG_INF = -1e30
FORCE_SCORE = 1e9
TINY = 1e-30
LOG2E = math.log2(math.e)

LANES = 128
SUBLANES = 8
VMEM_LIMIT = 56 * 1024 * 1024

ROW_TILE = 512
PROJ_ROW_TILE = 1024
FFN_ROW_TILE = 1024
SWA_PROJ_TILE = 1280
NSA_PROJ_TILE = 1792
FFN_TILE = 512
FFN_SPLIT = 2
SWA_TILE = 128
SWA_SKEW = 2
NSA_TILE = 256
NSA_PREP_TILE = 1024
NSA_V_ROWS = NSA_HEAD_DIM + 16
NSA_QSCALE = (NSA_HEAD_DIM ** -0.5) * LOG2E
PAIR = 2
SEL_KEY_TILE = 512


def _params(*sem):
    return pltpu.CompilerParams(dimension_semantics=sem, vmem_limit_bytes=VMEM_LIMIT)


def _dot(a, b):
    return jnp.dot(a, b, preferred_element_type=F32)


def _dot_nt(a, b):
    return lax.dot_general(a, b, (((1,), (1,)), ((), ())), preferred_element_type=F32)


def _eye(n):
    r = lax.broadcasted_iota(jnp.int32, (n, n), 0)
    c = lax.broadcasted_iota(jnp.int32, (n, n), 1)
    return jnp.where(r == c, 1.0, 0.0).astype(BF16)


def _transpose_bf16(x):
    return _dot_nt(_eye(x.shape[1]), x).astype(BF16)


def _software_pipeline(n_items, stages, skew):
    state = [{} for _ in range(n_items)]
    for t in range(n_items + skew * (len(stages) - 1)):
        for k, stage in enumerate(stages):
            c = t - k * skew
            if 0 <= c < n_items:
                stage(c, state[c])


def _rms(x, g):
    ms = jnp.mean(x * x, axis=-1, keepdims=True)
    return x * lax.rsqrt(ms + NORM_EPS) * g


def _sigmoid(x):
    return 1.0 / (1.0 + jnp.exp(-x))


def _rope(xn, rope, half):
    c, s1, s2 = rope[:, 0:LANES], rope[:, LANES:2 * LANES], rope[:, 2 * LANES:3 * LANES]
    return xn * c + pltpu.roll(xn, LANES - half, 1) * s1 + pltpu.roll(xn, half, 1) * s2


def _ffn_kernel(x_ref, g_ref, wg_ref, wu_ref, wo_ref, o_ref, xn_ref):
    def hidden_update(xn):
        w = wg_ref.shape[1] // FFN_SPLIT
        pre = [(_dot(xn, wg_ref[:, s * w:(s + 1) * w]), _dot(xn, wu_ref[:, s * w:(s + 1) * w]))
               for s in range(FFN_SPLIT)]
        upd = None
        for s, (gate, up) in enumerate(pre):
            hdn = (gate * _sigmoid(gate) * (0.5 * up)).astype(BF16)
            part = _dot(hdn, wo_ref[s * w:(s + 1) * w, :])
            upd = part if upd is None else upd + part
        return upd

    first = pl.program_id(1) == 0

    @pl.when(first)
    def _():
        x = x_ref[...]
        xn = _rms(x, g_ref[...]).astype(BF16)
        xn_ref[...] = xn
        o_ref[...] = x + hidden_update(xn)

    @pl.when(jnp.logical_not(first))
    def _():
        o_ref[...] += hidden_update(xn_ref[...])


def _ffn(h, norm, w_in, w_out, layer):
    m, d = h.shape
    f = w_out.shape[1]
    nf = f // FFN_TILE
    return pl.pallas_call(
        _ffn_kernel,
        grid=(m // FFN_ROW_TILE, nf),
        in_specs=[
            pl.BlockSpec((FFN_ROW_TILE, d), lambda i, j: (i, 0)),
            pl.BlockSpec((None, 1, d), lambda i, j: (layer, 0, 0)),
            pl.BlockSpec((None, d, FFN_TILE), lambda i, j: (layer, 0, j)),
            pl.BlockSpec((None, d, FFN_TILE), lambda i, j: (layer, 0, nf + j)),
            pl.BlockSpec((None, FFN_TILE, d), lambda i, j: (layer, j, 0)),
        ],
        out_specs=pl.BlockSpec((FFN_ROW_TILE, d), lambda i, j: (i, 0)),
        out_shape=jax.ShapeDtypeStruct((m, d), F32),
        scratch_shapes=[pltpu.VMEM((FFN_ROW_TILE, d), BF16)],
        compiler_params=_params("parallel", "arbitrary"),
        name="ffn",
    )(h, norm, w_in, w_in, w_out)


def _proj_kernel(x_ref, g_ref, w_ref, o_ref, xn_ref):
    first = pl.program_id(1) == 0

    @pl.when(first)
    def _():
        xn = _rms(x_ref[...], g_ref[...]).astype(BF16)
        xn_ref[...] = xn
        o_ref[...] = _dot(xn, w_ref[...])

    @pl.when(jnp.logical_not(first))
    def _():
        o_ref[...] = _dot(xn_ref[...], w_ref[...])


def _proj(h, norm, w, layer, wl, tn):
    m, d = h.shape
    n = w.shape[2]
    return pl.pallas_call(
        _proj_kernel,
        grid=(m // PROJ_ROW_TILE, n // tn),
        in_specs=[
            pl.BlockSpec((PROJ_ROW_TILE, d), lambda i, j: (i, 0)),
            pl.BlockSpec((None, 1, d), lambda i, j: (layer, 0, 0)),
            pl.BlockSpec((None, d, tn), lambda i, j: (wl, 0, j)),
        ],
        out_specs=pl.BlockSpec((PROJ_ROW_TILE, tn), lambda i, j: (i, j)),
        out_shape=jax.ShapeDtypeStruct((m, n), F32),
        scratch_shapes=[pltpu.VMEM((PROJ_ROW_TILE, d), BF16)],
        compiler_params=_params("parallel", "arbitrary"),
        name="mixer_in_proj",
    )(h, norm, w)


def _oproj_kernel(h_ref, o_ref, w_ref, out_ref):
    out_ref[...] = h_ref[...] + _dot(o_ref[...], w_ref[...])


def _oproj(h, o, w, wl):
    m, d = h.shape
    k = o.shape[1]
    return pl.pallas_call(
        _oproj_kernel,
        grid=(m // ROW_TILE,),
        in_specs=[
            pl.BlockSpec((ROW_TILE, d), lambda i: (i, 0)),
            pl.BlockSpec((ROW_TILE, k), lambda i: (i, 0)),
            pl.BlockSpec((None, k, d), lambda i: (wl, 0, 0)),
        ],
        out_specs=pl.BlockSpec((ROW_TILE, d), lambda i: (i, 0)),
        out_shape=jax.ShapeDtypeStruct((m, d), F32),
        compiler_params=_params("parallel"),
        name="mixer_out_proj",
    )(h, o, w)


def _swa_kernel(sink_ref, q_ref, kvc_ref, kvp_ref, rc_ref, rp_ref, qn_ref, kn_ref, o_ref):
    i = pl.program_id(1)
    tq = q_ref.shape[0]
    hd = SWA_HEAD_DIM
    half = hd // ROPE_FRACTION // 2
    n_chunks = q_ref.shape[1] // LANES
    heads_per_group = (2 * n_chunks) // SWA_KV_HEADS
    kv_w = SWA_KV_HEADS * hd
    n_kv_chunks = kv_w // LANES
    c1 = (hd ** -0.5) * LOG2E


    row = lax.broadcasted_iota(jnp.int32, (LANES, LANES), 0)
    col = lax.broadcasted_iota(jnp.int32, (LANES, LANES), 1)
    seg = jnp.where((row // hd) == (col // hd), 1.0 / hd, 0.0).astype(BF16)
    eye = _eye(LANES)

    def mean_sq(x):
        xx = x * x
        hi = xx.astype(BF16)
        lo = (xx - hi.astype(F32)).astype(BF16)
        return _dot(hi, seg) + _dot(lo, seg)

    def norm_rope(x, gn, rope):
        return _rope(x * lax.rsqrt(mean_sq(x) + NORM_EPS) * gn, rope, half)

    low = lax.broadcasted_iota(jnp.int32, (2 * tq, LANES), 1) < hd
    top = lax.broadcasted_iota(jnp.int32, (LANES, 2 * tq), 0) < hd
    kn = kn_ref[...]
    k_chunks = []
    for c in range(n_kv_chunks):
        sl = slice(c * LANES, (c + 1) * LANES)
        k_chunks.append(jnp.concatenate([norm_rope(kvp_ref[:, sl], kn, rp_ref[...]),
                                         norm_rope(kvc_ref[:, sl], kn, rc_ref[...])], axis=0))
    v_all = jnp.concatenate([kvp_ref[:, kv_w:2 * kv_w], kvc_ref[:, kv_w:2 * kv_w]], axis=0).astype(BF16)
    vt = _dot_nt(_eye(kv_w), v_all)

    kd, vlo, vhi = [], [], []
    for g in range(SWA_KV_HEADS):
        kc_ = k_chunks[g // 2]
        kc_sw = pltpu.roll(kc_, hd, 1)
        vc_ = vt[(g // 2) * LANES:(g // 2 + 1) * LANES, :]
        vc_sw = pltpu.roll(vc_, hd, 0)
        if g % 2 == 0:
            kd.append(jnp.where(low, kc_, kc_sw).astype(BF16))
            vlo.append(jnp.where(top, vc_, 1.0).astype(BF16))
            vhi.append(jnp.where(top, 1.0, vc_sw).astype(BF16))
        else:
            kd.append(jnp.where(low, kc_sw, kc_).astype(BF16))
            vlo.append(jnp.where(top, vc_sw, 1.0).astype(BF16))
            vhi.append(jnp.where(top, 1.0, vc_).astype(BF16))

    ki = lax.broadcasted_iota(jnp.int32, (2 * tq, tq), 0)
    qi = lax.broadcasted_iota(jnp.int32, (2 * tq, tq), 1)
    rel = qi + tq - ki
    mask = (rel >= 0) & (rel < SWA_WINDOW) & ((ki >= tq) | (i > 0))
    qtop = lax.broadcasted_iota(jnp.int32, (LANES, tq), 0) < hd
    qn = qn_ref[...]

    def st_load(c, st):
        st["x"] = q_ref[:, c * LANES:(c + 1) * LANES]
        st["ms"] = mean_sq(st["x"])

    def st_query(c, st):
        xq = (_rope(st["x"] * lax.rsqrt(st["ms"] + NORM_EPS) * qn, rc_ref[...], half) * c1).astype(BF16)
        st["qt"] = _dot_nt(eye, xq)

    def st_scores(c, st):
        g = (2 * c) // heads_per_group
        qt = st["qt"]
        st["s"] = [_dot(kd[g], jnp.where(qtop, qt, 0.0).astype(BF16)),
                   _dot(kd[g], jnp.where(qtop, 0.0, qt).astype(BF16))]

    def st_values(c, st):
        g = (2 * c) // heads_per_group
        st["pv"] = []
        for par, vpl in ((0, vlo[g]), (1, vhi[g])):
            s = jnp.where(mask, st["s"][par], NEG_INF)
            sink = sink_ref[2 * c + par] * LOG2E
            mx = jnp.maximum(jnp.max(s, axis=0, keepdims=True), sink)
            pv = _dot(vpl, jnp.exp2(s - mx).astype(BF16))
            den = (pv[hd:hd + 1, :] if par == 0 else pv[0:1, :]) + jnp.exp2(sink - mx)
            st["pv"].append(pv * (1.0 / den))

    def st_merge(c, st):
        ot = jnp.where(qtop, st["pv"][0], st["pv"][1]).astype(BF16)
        st["o"] = _dot_nt(eye, ot)

    def st_store(c, st):
        o_ref[:, c * LANES:(c + 1) * LANES] = st["o"].astype(o_ref.dtype)

    _software_pipeline(n_chunks, (st_load, st_query, st_scores, st_values, st_merge, st_store), SWA_SKEW)


def _swa_attention(p, rope, q_norm2, k_norm2, sinks, batch, seq):
    m = p.shape[0]
    tq = SWA_TILE
    nb = seq // tq
    n_q = sinks.shape[0] * SWA_HEAD_DIM
    kv_w = 2 * SWA_KV_HEADS * SWA_HEAD_DIM
    kv_blk = n_q // kv_w
    return pl.pallas_call(
        _swa_kernel,
        grid=(batch, nb),
        in_specs=[
            pl.BlockSpec(memory_space=pltpu.SMEM),
            pl.BlockSpec((tq, n_q), lambda b, i: (b * nb + i, 0)),
            pl.BlockSpec((tq, kv_w), lambda b, i: (b * nb + i, kv_blk)),
            pl.BlockSpec((tq, kv_w), lambda b, i: (b * nb + jnp.maximum(i - 1, 0), kv_blk)),
            pl.BlockSpec((tq, 3 * LANES), lambda b, i: (b * nb + i, 0)),
            pl.BlockSpec((tq, 3 * LANES), lambda b, i: (b * nb + jnp.maximum(i - 1, 0), 0)),
            pl.BlockSpec((1, LANES), lambda b, i: (0, 0)),
            pl.BlockSpec((1, LANES), lambda b, i: (0, 0)),
        ],
        out_specs=pl.BlockSpec((tq, n_q), lambda b, i: (b * nb + i, 0)),
        out_shape=jax.ShapeDtypeStruct((m, n_q), BF16),
        compiler_params=_params("parallel", "arbitrary"),
        name="swa_attention",
    )(sinks, p, p, p, rope, rope, q_norm2, k_norm2)


def _nsa_compress_kernel(kc_ref, vc_ref, ropec_ref, kn_ref, pek_ref, w1k_ref, w2k_ref,
                         pev_ref, w1v_ref, w2v_ref, kco_ref, vco_ref):
    half = NSA_HEAD_DIM // ROPE_FRACTION // 2
    n_chunk = kc_ref.shape[0] // CMP_STRIDE
    per = CMP_BLOCK // CMP_STRIDE

    def compress(t_ref, pe_ref, w1_ref, w2_ref):
        parts = []
        for h in range(per):
            acc = jnp.zeros((n_chunk, w1_ref.shape[2]), F32)
            for l in range(CMP_STRIDE):
                ll = h * CMP_STRIDE + l
                xl = t_ref[pl.ds(l, n_chunk, stride=CMP_STRIDE), :] + pe_ref[ll:ll + 1, :]
                acc = acc + _dot(xl.astype(BF16), w1_ref[ll])
            parts.append(acc)
        pre = parts[0]
        for h in range(1, per):
            pre = pre + pltpu.roll(parts[h], n_chunk - h, 0)
        hdn = pre * _sigmoid(pre)
        return _dot(hdn.astype(BF16), w2_ref[...])

    kcmp = compress(kc_ref, pek_ref, w1k_ref, w2k_ref)
    kco_ref[...] = _rope(_rms(kcmp, kn_ref[...]), ropec_ref[...], half).astype(BF16)
    vco_ref[...] = _transpose_bf16(compress(vc_ref, pev_ref, w1v_ref, w2v_ref).astype(BF16))


def _nsa_compress(p, rope_c, k_norm, pe_k, w1_k, w2_k, pe_v, w1_v, w2_v, batch, seq):
    d = NSA_HEAD_DIM
    g_n = NSA_KV_HEADS
    n_chunk = seq // CMP_STRIDE
    q_blocks = 4 * g_n

    def col(kind):
        return pl.BlockSpec((seq, d), lambda b, g, kind=kind: (b, q_blocks + kind * g_n + g))

    def full(a):
        return pl.BlockSpec(a.shape, lambda b, g, nd=a.ndim: (0,) * nd)

    return pl.pallas_call(
        _nsa_compress_kernel,
        grid=(batch, g_n),
        in_specs=[col(0), col(1),
                  pl.BlockSpec((n_chunk, 3 * LANES), lambda b, g: (b, 0)),
                  full(k_norm), full(pe_k), full(w1_k), full(w2_k), full(pe_v), full(w1_v), full(w2_v)],
        out_specs=[pl.BlockSpec((None, None, n_chunk, d), lambda b, g: (b, g, 0, 0)),
                   pl.BlockSpec((None, None, d, n_chunk), lambda b, g: (b, g, 0, 0))],
        out_shape=[jax.ShapeDtypeStruct((batch, g_n, n_chunk, d), BF16),
                   jax.ShapeDtypeStruct((batch, g_n, d, n_chunk), BF16)],
        compiler_params=_params("parallel", "arbitrary"),
        name="nsa_compress",
    )(p, p, rope_c, k_norm, pe_k, w1_k, w2_k, pe_v, w1_v, w2_v)


def _nsa_qkv_kernel(q_ref, ks_ref, vs_ref, kw_ref, vw_ref, rope_ref, qn_ref, kn_ref,
                    qt_ref, kso_ref, vst_ref, kwo_ref, vwt_ref):
    d = NSA_HEAD_DIM
    half = d // ROPE_FRACTION // 2
    rep = q_ref.shape[1] // d
    tq = NSA_TILE
    eye = _eye(d)
    for s in range(q_ref.shape[0] // tq):
        rows = slice(s * tq, (s + 1) * tq)
        for r in range(rep):
            qr = _rope(_rms(q_ref[rows, r * d:(r + 1) * d], qn_ref[...]), rope_ref[rows, :], half) * NSA_QSCALE
            qt_ref[:, (s * rep + r) * tq:(s * rep + r + 1) * tq] = _dot_nt(eye, qr.astype(BF16)).astype(BF16)
    kso_ref[...] = _rope(_rms(ks_ref[...], kn_ref[...]), rope_ref[...], half).astype(BF16)
    kwo_ref[...] = _rope(_rms(kw_ref[...], kn_ref[...]), rope_ref[...], half).astype(BF16)
    ones = jnp.ones((NSA_V_ROWS - d, vs_ref.shape[0]), BF16)
    vst_ref[0:d, :] = _dot_nt(eye, vs_ref[...].astype(BF16)).astype(BF16)
    vst_ref[d:NSA_V_ROWS, :] = ones
    vwt_ref[0:d, :] = _dot_nt(eye, vw_ref[...].astype(BF16)).astype(BF16)
    vwt_ref[d:NSA_V_ROWS, :] = ones


def _nsa_qkv(p, rope, q_norm, k_norm, batch, seq):
    d = NSA_HEAD_DIM
    g_n = NSA_KV_HEADS
    rep = 4
    tt = NSA_PREP_TILE
    nt = seq // tt
    q_blocks = rep * g_n

    def col(kind):
        return pl.BlockSpec((tt, d), lambda b, g, t, kind=kind: (b * nt + t, q_blocks + kind * g_n + g))

    nat = pl.BlockSpec((None, None, tt, d), lambda b, g, t: (b, g, t, 0))
    tra = pl.BlockSpec((None, None, NSA_V_ROWS, tt), lambda b, g, t: (b, g, 0, t))
    sh_nat = jax.ShapeDtypeStruct((batch, g_n, seq, d), BF16)
    sh_tra = jax.ShapeDtypeStruct((batch, g_n, NSA_V_ROWS, seq), BF16)
    return pl.pallas_call(
        _nsa_qkv_kernel,
        grid=(batch, g_n, nt),
        in_specs=[pl.BlockSpec((tt, rep * d), lambda b, g, t: (b * nt + t, g)),
                  col(2), col(3), col(4), col(5),
                  pl.BlockSpec((tt, 3 * LANES), lambda b, g, t: (b * nt + t, 0)),
                  pl.BlockSpec((1, d), lambda b, g, t: (0, 0)),
                  pl.BlockSpec((1, d), lambda b, g, t: (0, 0))],
        out_specs=[pl.BlockSpec((None, None, d, rep * tt), lambda b, g, t: (b, g, 0, t)), nat, tra, nat, tra],
        out_shape=[jax.ShapeDtypeStruct((batch, g_n, d, rep * seq), BF16), sh_nat, sh_tra, sh_nat, sh_tra],
        compiler_params=_params("parallel", "parallel", "arbitrary"),
        name="nsa_qkv_prep",
    )(p, p, p, p, p, rope, q_norm, k_norm)


def _nsa_gate_kernel(g_ref, o_ref):
    o_ref[...] = _sigmoid(g_ref[...]).T


def _nsa_gates(p, batch, seq):
    tt = NSA_PREP_TILE
    nt = seq // tt
    gate_blk = 10 * NSA_KV_HEADS
    return pl.pallas_call(
        _nsa_gate_kernel,
        grid=(batch, nt),
        in_specs=[pl.BlockSpec((tt, LANES), lambda b, t: (b * nt + t, gate_blk))],
        out_specs=pl.BlockSpec((None, LANES, tt), lambda b, t: (b, 0, t)),
        out_shape=jax.ShapeDtypeStruct((batch, LANES, seq), F32),
        compiler_params=_params("parallel", "arbitrary"),
        name="nsa_gates",
    )(p)


def _nsa_kernel(qt_ref, gt_ref, kc_ref, vct_ref, ks_ref, vst_ref, kw_ref, vwt_ref, ovl_ref, o_ref,
                sel_scr, score_scr, cnt_scr, base_scr, acc_scr, s0_scr, sw_scr, bias_scr, p_scr):
    g = pl.program_id(1)
    i = pl.program_id(2)
    d = NSA_HEAD_DIM
    rows = qt_ref.shape[1]
    tq = o_ref.shape[0]
    rep = rows // tq
    n_cmp = kc_ref.shape[0]
    n_slc = ovl_ref.shape[0]
    n_heads = NSA_KV_HEADS * rep
    tk = SEL_KEY_TILE
    ch = SLC_BLOCK
    q0 = i * tq
    qt = qt_ref[...]

    def head_cols(r):
        return slice(r * tq, (r + 1) * tq)

    def pair_cols(pr):
        return slice(pr * PAIR * tq, (pr + 1) * PAIR * tq)

    def gate_row(branch, r):
        return gt_ref[pl.ds(branch * n_heads + g * rep + r, 1), :]

    def fold_max(x):
        parts = [x[j * SUBLANES:(j + 1) * SUBLANES, :] for j in range(x.shape[0] // SUBLANES)]
        while len(parts) > 1:
            parts = [jnp.maximum(parts[j], parts[j + 1]) for j in range(0, len(parts), 2)]
        return parts[0]

    def chunked_softmax(chunk, n_chunks, m_floor, p_slot, p_cols):
        m8 = fold_max(chunk(0))
        for c in range(1, n_chunks):
            m8 = jnp.maximum(m8, fold_max(chunk(c)))
        m = jnp.max(m8, axis=0, keepdims=True)
        if m_floor is not None:
            m = jnp.maximum(m, m_floor)
        for c in range(n_chunks):
            p_scr[p_slot, c * ch:(c + 1) * ch, p_cols] = jnp.exp2(chunk(c) - m).astype(BF16)
        return m

    wk = NSA_WINDOW + tq
    start = pl.multiple_of(jnp.maximum(q0 - NSA_WINDOW, 0), tq)

    @pl.when(i <= NSA_WINDOW // tq)
    def _():
        for c in range(wk // ch):
            t_w = q0 + lax.broadcasted_iota(jnp.int32, (ch, tq), 1)
            k_w = start + c * ch + lax.broadcasted_iota(jnp.int32, (ch, tq), 0)
            ok = (k_w <= t_w) & (t_w - k_w < NSA_WINDOW)
            bias_scr[c * ch:(c + 1) * ch, :] = jnp.where(ok, 0.0, NEG_INF)

    sw_scr[0] = _dot(kw_ref[pl.ds(start, wk), :], qt[:, pair_cols(0)])
    sc_raw = _dot(kc_ref[...], qt)
    for pr in range(1, rep // PAIR):
        sw_scr[pr] = _dot(kw_ref[pl.ds(start, wk), :], qt[:, pair_cols(pr)])
    s0_scr[...] = _dot(ks_ref[0:tk, :], qt[:, pair_cols(0)])

    t_c = q0 + lax.broadcasted_iota(jnp.int32, (n_cmp, tq), 1)
    end_c = lax.broadcasted_iota(jnp.int32, (n_cmp, tq), 0) * CMP_STRIDE + (CMP_BLOCK - 1)
    mask_c = end_c <= t_c

    imp = jnp.zeros((n_slc, tq), F32)
    for pr in range(rep // PAIR):
        pcs = []
        for h in range(PAIR):
            cols = head_cols(pr * PAIR + h)
            chunked_softmax(lambda c, pr=pr, h=h: (sw_scr[pr, c * ch:(c + 1) * ch, head_cols(h)]
                                                   + bias_scr[c * ch:(c + 1) * ch, :]),
                            wk // ch, None, pr % 2, head_cols(h))
            sc = jnp.where(mask_c, sc_raw[:, cols], NEG_INF)
            ec = jnp.where(mask_c, jnp.exp2(sc - jnp.max(sc, axis=0, keepdims=True)), 0.0)
            pcs.append((ec / jnp.maximum(jnp.sum(ec, axis=0, keepdims=True), TINY)).astype(BF16))
        pc = jnp.concatenate(pcs, axis=1)
        o_win = _dot(vwt_ref[:, pl.ds(start, wk)], p_scr[pr % 2, 0:wk, :])
        o_cmp = _dot(vct_ref[...], pc)
        imp_pair = _dot(ovl_ref[...], pc)
        for h in range(PAIR):
            r = pr * PAIR + h
            hc = head_cols(h)
            imp = imp + imp_pair[:, hc]
            base_scr[:, head_cols(r)] = (gate_row(0, r) * o_cmp[:, hc]
                                         + (gate_row(2, r) / o_win[d:d + 1, hc]) * o_win[0:d, hc])

    jb = lax.broadcasted_iota(jnp.int32, (n_slc, tq), 0)
    tt = q0 + lax.broadcasted_iota(jnp.int32, (n_slc, tq), 1)
    cur = tt // SLC_BLOCK
    forced = (jb == 0) | (jb == cur) | (jb == cur - 1)
    score = jnp.where(forced, FORCE_SCORE, jnp.where(jb * SLC_BLOCK <= tt, imp, NEG_INF))
    sub = SUBLANES
    n_grp = n_slc // sub
    score_scr[...] = score
    cnt_scr[...] = jnp.zeros(cnt_scr.shape, F32)
    jsub = lax.broadcasted_iota(jnp.int32, (sub, tq), 0)
    for gi in range(n_grp):
        @pl.when(gi * sub * SLC_BLOCK <= q0 + tq - 1)
        def _(gi=gi):
            blocks = [score_scr[v * sub:(v + 1) * sub, :] for v in range(n_grp)]
            counts = [cnt_scr[v * sub:(v + 1) * sub, :] for v in range(n_grp)]
            for ii in range(gi * sub, (gi + 1) * sub):
                rowv = score_scr[ii:ii + 1, :]
                for v, blk in enumerate(blocks):
                    if (v + 1) * sub - 1 <= ii:
                        beat = jnp.where(rowv > blk, 1.0, 0.0)
                    elif v * sub > ii:
                        beat = jnp.where(rowv >= blk, 1.0, 0.0)
                    else:
                        beat = jnp.where(jsub + v * sub > ii, jnp.where(rowv >= blk, 1.0, 0.0),
                                         jnp.where(rowv > blk, 1.0, 0.0))
                    counts[v] = counts[v] + beat
            for v, cnt in enumerate(counts):
                cnt_scr[v * sub:(v + 1) * sub, :] = cnt
    n_top = min(SLC_TOP_N, n_slc)
    sel_scr[...] = jnp.where(cnt_scr[...] < n_top, 1.0, 0.0)

    acc_scr[...] = jnp.zeros(acc_scr.shape, F32)
    blk_per_tile = tk // ch
    t_b = q0 + lax.broadcasted_iota(jnp.int32, (ch, tq), 1)
    k_b = lax.broadcasted_iota(jnp.int32, (ch, tq), 0)
    n_full = q0 // tk

    def sel_step(kt, m_prev, diagonal):
        k0 = pl.multiple_of(kt * tk, tk)
        keys = ks_ref[pl.ds(k0, tk), :]
        sel_bias = [(sel_scr[pl.ds(kt * blk_per_tile + bb, 1), :] - 1.0) * (-NEG_INF)
                    for bb in range(blk_per_tile)]
        n_pairs = rep // PAIR
        scores = {1: _dot(keys, qt[:, pair_cols(1)])}
        m_out = []
        for pr in range(n_pairs):
            alphas = []
            for h in range(PAIR):
                r = pr * PAIR + h

                def chunk(bb, pr=pr, h=h):
                    rows_bb = slice(bb * ch, (bb + 1) * ch)
                    x = (s0_scr[rows_bb, head_cols(h)] if pr == 0 else scores[pr][rows_bb, head_cols(h)])
                    x = x + sel_bias[bb]
                    if diagonal:
                        x = jnp.where(k_b + (k0 + bb * ch) <= t_b, x, NEG_INF)
                    return x

                m_new = chunked_softmax(chunk, blk_per_tile, m_prev[r], pr % 2, head_cols(h))
                alphas.append(jnp.exp2(m_prev[r] - m_new))
                m_out.append(m_new)
            pv = _dot(vst_ref[:, pl.ds(k0, tk)], p_scr[pr % 2, 0:tk, :])
            for h in range(PAIR):
                cols = head_cols(pr * PAIR + h)
                acc_scr[:, cols] = alphas[h] * acc_scr[:, cols] + pv[:, head_cols(h)]
            if pr + 2 < n_pairs:
                scores[pr + 2] = _dot(keys, qt[:, pair_cols(pr + 2)])
            elif pr + 2 == n_pairs and not diagonal:
                k1 = pl.multiple_of(k0 + tk, tk)
                s0_scr[...] = _dot(ks_ref[pl.ds(k1, tk), :], qt[:, pair_cols(0)])
        return tuple(m_out)

    m_run = lax.fori_loop(0, n_full, lambda kt, m: sel_step(kt, m, False),
                          tuple(jnp.full((1, tq), NEG_INF, F32) for _ in range(rep)))
    sel_step(n_full, m_run, True)

    for r in range(rep):
        cols = slice(r * tq, (r + 1) * tq)
        out_t = base_scr[:, cols] + (gate_row(1, r) / acc_scr[d:d + 1, cols]) * acc_scr[0:d, cols]
        o_ref[:, r * d:(r + 1) * d] = _transpose_bf16(out_t.astype(BF16)).astype(o_ref.dtype)


def _nsa_attend(qt, gt, kc, vct, ks, vst, kw, vwt, ovl_t, batch, seq):
    d = NSA_HEAD_DIM
    g_n = NSA_KV_HEADS
    tq = NSA_TILE
    nq = seq // tq
    rep = qt.shape[3] // seq
    n_chunk = kc.shape[2]
    n_slc = ovl_t.shape[0]
    gate_rows = -(-3 * g_n * rep // SUBLANES) * SUBLANES

    def kv(shape):
        return pl.BlockSpec((None, None) + shape, lambda b, g, i: (b, g, 0, 0))

    return pl.pallas_call(
        _nsa_kernel,
        grid=(batch, g_n, nq),
        in_specs=[
            pl.BlockSpec((None, None, d, rep * tq), lambda b, g, i: (b, g, 0, i)),
            pl.BlockSpec((None, gate_rows, tq), lambda b, g, i: (b, 0, i)),
            kv((n_chunk, d)), kv((d, n_chunk)),
            kv((seq, d)), kv((NSA_V_ROWS, seq)), kv((seq, d)), kv((NSA_V_ROWS, seq)),
            pl.BlockSpec(ovl_t.shape, lambda b, g, i: (0, 0)),
        ],
        out_specs=pl.BlockSpec((tq, rep * d), lambda b, g, i: (b * nq + i, g)),
        out_shape=jax.ShapeDtypeStruct((batch * seq, g_n * rep * d), BF16),
        scratch_shapes=[
            pltpu.VMEM((n_slc, tq), F32),
            pltpu.VMEM((n_slc, tq), F32),
            pltpu.VMEM((n_slc, tq), F32),
            pltpu.VMEM((d, rep * tq), F32),
            pltpu.VMEM((NSA_V_ROWS, rep * tq), F32),
            pltpu.VMEM((SEL_KEY_TILE, PAIR * tq), F32),
            pltpu.VMEM((rep // PAIR, NSA_WINDOW + tq, PAIR * tq), F32),
            pltpu.VMEM((NSA_WINDOW + tq, tq), F32),
            pltpu.VMEM((2, max(NSA_WINDOW + tq, SEL_KEY_TILE), PAIR * tq), BF16),
        ],
        compiler_params=_params("parallel", "parallel", "arbitrary"),
        name="nsa_attention",
    )(qt, gt, kc, vct, ks, vst, kw, vwt, ovl_t)


def _rope_table(pos, head_dim):
    rd = head_dim // ROPE_FRACTION
    half = rd // 2
    inv = 1.0 / (ROPE_THETA ** (jnp.arange(half, dtype=F32) * (2.0 / rd)))
    ang = pos.astype(F32)[:, None] * inv
    cos, sin = jnp.cos(ang), jnp.sin(ang)
    n = pos.shape[0]
    ones = jnp.ones((n, head_dim - rd), F32)
    zeros = jnp.zeros((n, head_dim - rd), F32)
    zh = jnp.zeros((n, half), F32)
    c = jnp.concatenate([cos, cos, ones], axis=1)
    s1 = jnp.concatenate([-sin, zh, zeros], axis=1)
    s2 = jnp.concatenate([zh, sin, zeros], axis=1)
    reps = LANES // head_dim
    return jnp.concatenate([jnp.tile(c, (1, reps)), jnp.tile(s1, (1, reps)), jnp.tile(s2, (1, reps))], axis=1)


def _overlap_matrix(seq):
    n_cmp_pad = seq // CMP_STRIDE
    n_slc = seq // SLC_BLOCK
    cs = np.arange(n_cmp_pad)[None, :] * CMP_STRIDE
    ss = np.arange(n_slc)[:, None] * SLC_BLOCK
    ovl_t = ((cs < ss + SLC_BLOCK) & (cs + CMP_BLOCK > ss)).astype(np.float32)
    return jnp.asarray(ovl_t, BF16)


def kernel(x, positions, ffn1_norm, ffn1_w_in, ffn1_w_out, mix_norm, ffn2_norm, ffn2_w_in, ffn2_w_out,
           swa_w_in, swa_q_norm, swa_k_norm, swa_sinks, swa_w_out,
           nsa_w_in, nsa_q_norm, nsa_k_norm, nsa_cmp_pe_k, nsa_cmp_w1_k, nsa_cmp_w2_k,
           nsa_cmp_pe_v, nsa_cmp_w1_v, nsa_cmp_w2_v, nsa_w_out):
    batch, seq, d_model = x.shape
    depth = ffn1_norm.shape[0]
    m = batch * seq
    h = x.reshape(m, d_model)

    f1_in, f1_out = ffn1_w_in.astype(BF16), ffn1_w_out.astype(BF16)
    f2_in, f2_out = ffn2_w_in.astype(BF16), ffn2_w_out.astype(BF16)
    swa_in, swa_out = swa_w_in.astype(BF16), swa_w_out.astype(BF16)
    nsa_n = nsa_w_in.shape[2]
    nsa_tn = NSA_PROJ_TILE
    nsa_pad = -nsa_n % nsa_tn
    nsa_in = jnp.pad(nsa_w_in.astype(BF16), ((0, 0), (0, 0), (0, nsa_pad)))
    nsa_out = nsa_w_out.astype(BF16)
    w1_k, w2_k = nsa_cmp_w1_k.astype(BF16), nsa_cmp_w2_k.astype(BF16)
    w1_v, w2_v = nsa_cmp_w1_v.astype(BF16), nsa_cmp_w2_v.astype(BF16)

    n1 = ffn1_norm.reshape(depth, 1, d_model)
    nm = mix_norm.reshape(depth, 1, d_model)
    n2 = ffn2_norm.reshape(depth, 1, d_model)

    pos_flat = positions.reshape(m)
    rope_swa = _rope_table(pos_flat, SWA_HEAD_DIM)
    rope_nsa = _rope_table(pos_flat, NSA_HEAD_DIM)
    n_chunk = seq // CMP_STRIDE
    end_idx = jnp.minimum(jnp.arange(n_chunk) * CMP_STRIDE + (CMP_BLOCK - 1), seq - 1)
    rope_cmp = _rope_table(positions[:, end_idx].reshape(batch * n_chunk), NSA_HEAD_DIM)
    ovl_t = _overlap_matrix(seq)

    for i in range(depth):
        h = _ffn(h, n1, f1_in, f1_out, i)
        j = i // N_MIXERS
        if i % N_MIXERS == 0:
            p = _proj(h, nm, swa_in, i, j, SWA_PROJ_TILE)
            qn2 = jnp.tile(swa_q_norm[j], LANES // SWA_HEAD_DIM).reshape(1, LANES)
            kn2 = jnp.tile(swa_k_norm[j], LANES // SWA_HEAD_DIM).reshape(1, LANES)
            o = _swa_attention(p, rope_swa, qn2, kn2, swa_sinks[j], batch, seq)
            h = _oproj(h, o, swa_out, j)
        else:
            p = _proj(h, nm, nsa_in, i, j, nsa_tn)
            kn = nsa_k_norm[j].reshape(1, NSA_HEAD_DIM)
            qn = nsa_q_norm[j].reshape(1, NSA_HEAD_DIM)
            kc, vct = _nsa_compress(p, rope_cmp, kn, nsa_cmp_pe_k[j], w1_k[j], w2_k[j],
                                    nsa_cmp_pe_v[j], w1_v[j], w2_v[j], batch, seq)
            qt, ks, vst, kw, vwt = _nsa_qkv(p, rope_nsa, qn, kn, batch, seq)
            gt = _nsa_gates(p, batch, seq)
            o = _nsa_attend(qt, gt, kc, vct, ks, vst, kw, vwt, ovl_t, batch, seq)
            h = _oproj(h, o, nsa_out, j)
        h = _ffn(h, n2, f2_in, f2_out, i)
    return h.reshape(batch, seq, d_model)
```

```python
import math

import jax
import jax.numpy as jnp
import numpy as np
from jax import lax
from jax.experimental import pallas as pl
from jax.experimental.pallas import tpu as pltpu

F32 = jnp.float32
BF16 = jnp.bfloat16

N_MIXERS = 2
SWA_HEAD_DIM = 64
SWA_KV_HEADS = 4
SWA_WINDOW = 128
NSA_HEAD_DIM = 128
NSA_KV_HEADS = 4
CMP_BLOCK = 32
CMP_STRIDE = 16
SLC_BLOCK = 64
SLC_TOP_N = 16
NSA_WINDOW = 512
ROPE_THETA = 500000.0
ROPE_FRACTION = 4
NORM_EPS = 1e-6
NEG_INF = -1e30
FORCE_SCORE = 1e9
TINY = 1e-30
LOG2E = math.log2(math.e)

LANES = 128
SUBLANES = 8
VMEM_LIMIT = 56 * 1024 * 1024

ROW_TILE = 512
PROJ_ROW_TILE = 1024
FFN_ROW_TILE = 1024
SWA_PROJ_TILE = 1280
NSA_PROJ_TILE = 1792
FFN_TILE = 512
FFN_SPLIT = 2
SWA_TILE = 128
SWA_SKEW = 2
NSA_TILE = 256
NSA_PREP_TILE = 1024
NSA_V_ROWS = NSA_HEAD_DIM + 16
NSA_QSCALE = (NSA_HEAD_DIM ** -0.5) * LOG2E
PAIR = 2
SEL_KEY_TILE = 512


def _params(*sem):
    return pltpu.CompilerParams(dimension_semantics=sem, vmem_limit_bytes=VMEM_LIMIT)


def _dot(a, b):
    return jnp.dot(a, b, preferred_element_type=F32)


def _dot_nt(a, b):
    return lax.dot_general(a, b, (((1,), (1,)), ((), ())), preferred_element_type=F32)


def _eye(n):
    r = lax.broadcasted_iota(jnp.int32, (n, n), 0)
    c = lax.broadcasted_iota(jnp.int32, (n, n), 1)
    return jnp.where(r == c, 1.0, 0.0).astype(BF16)


def _transpose_bf16(x):
    return _dot_nt(_eye(x.shape[1]), x).astype(BF16)


def _software_pipeline(n_items, stages, skew):
    state = [{} for _ in range(n_items)]
    for t in range(n_items + skew * (len(stages) - 1)):
        for k, stage in enumerate(stages):
            c = t - k * skew
            if 0 <= c < n_items:
                stage(c, state[c])


def _rms(x, g):
    ms = jnp.mean(x * x, axis=-1, keepdims=True)
    return x * lax.rsqrt(ms + NORM_EPS) * g


def _sigmoid(x):
    return 1.0 / (1.0 + jnp.exp(-x))


def _rope(xn, rope, half):
    c, s1, s2 = rope[:, 0:LANES], rope[:, LANES:2 * LANES], rope[:, 2 * LANES:3 * LANES]
    return xn * c + pltpu.roll(xn, LANES - half, 1) * s1 + pltpu.roll(xn, half, 1) * s2


def _ffn_kernel(x_ref, g_ref, wg_ref, wu_ref, wo_ref, o_ref, xn_ref):
    def hidden_update(xn):
        w = wg_ref.shape[1] // FFN_SPLIT
        pre = [(_dot(xn, wg_ref[:, s * w:(s + 1) * w]), _dot(xn, wu_ref[:, s * w:(s + 1) * w]))
               for s in range(FFN_SPLIT)]
        upd = None
        for s, (gate, up) in enumerate(pre):
            hdn = (gate * _sigmoid(gate) * (0.5 * up)).astype(BF16)
            part = _dot(hdn, wo_ref[s * w:(s + 1) * w, :])
            upd = part if upd is None else upd + part
        return upd

    first = pl.program_id(1) == 0

    @pl.when(first)
    def _():
        x = x_ref[...]
        xn = _rms(x, g_ref[...]).astype(BF16)
        xn_ref[...] = xn
        o_ref[...] = x + hidden_update(xn)

    @pl.when(jnp.logical_not(first))
    def _():
        o_ref[...] += hidden_update(xn_ref[...])


def _ffn(h, norm, w_in, w_out, layer):
    m, d = h.shape
    f = w_out.shape[1]
    nf = f // FFN_TILE
    return pl.pallas_call(
        _ffn_kernel,
        grid=(m // FFN_ROW_TILE, nf),
        in_specs=[
            pl.BlockSpec((FFN_ROW_TILE, d), lambda i, j: (i, 0)),
            pl.BlockSpec((None, 1, d), lambda i, j: (layer, 0, 0)),
            pl.BlockSpec((None, d, FFN_TILE), lambda i, j: (layer, 0, j)),
            pl.BlockSpec((None, d, FFN_TILE), lambda i, j: (layer, 0, nf + j)),
            pl.BlockSpec((None, FFN_TILE, d), lambda i, j: (layer, j, 0)),
        ],
        out_specs=pl.BlockSpec((FFN_ROW_TILE, d), lambda i, j: (i, 0)),
        out_shape=jax.ShapeDtypeStruct((m, d), F32),
        scratch_shapes=[pltpu.VMEM((FFN_ROW_TILE, d), BF16)],
        compiler_params=_params("parallel", "arbitrary"),
        name="ffn",
    )(h, norm, w_in, w_in, w_out)


def _proj_kernel(x_ref, g_ref, w_ref, o_ref, xn_ref):
    first = pl.program_id(1) == 0

    @pl.when(first)
    def _():
        xn = _rms(x_ref[...], g_ref[...]).astype(BF16)
        xn_ref[...] = xn
        o_ref[...] = _dot(xn, w_ref[...])

    @pl.when(jnp.logical_not(first))
    def _():
        o_ref[...] = _dot(xn_ref[...], w_ref[...])


def _proj(h, norm, w, layer, wl, tn):
    m, d = h.shape
    n = w.shape[2]
    return pl.pallas_call(
        _proj_kernel,
        grid=(m // PROJ_ROW_TILE, n // tn),
        in_specs=[
            pl.BlockSpec((PROJ_ROW_TILE, d), lambda i, j: (i, 0)),
            pl.BlockSpec((None, 1, d), lambda i, j: (layer, 0, 0)),
            pl.BlockSpec((None, d, tn), lambda i, j: (wl, 0, j)),
        ],
        out_specs=pl.BlockSpec((PROJ_ROW_TILE, tn), lambda i, j: (i, j)),
        out_shape=jax.ShapeDtypeStruct((m, n), F32),
        scratch_shapes=[pltpu.VMEM((PROJ_ROW_TILE, d), BF16)],
        compiler_params=_params("parallel", "arbitrary"),
        name="mixer_in_proj",
    )(h, norm, w)


def _oproj_kernel(h_ref, o_ref, w_ref, out_ref):
    out_ref[...] = h_ref[...] + _dot(o_ref[...], w_ref[...])


def _oproj(h, o, w, wl):
    m, d = h.shape
    k = o.shape[1]
    return pl.pallas_call(
        _oproj_kernel,
        grid=(m // ROW_TILE,),
        in_specs=[
            pl.BlockSpec((ROW_TILE, d), lambda i: (i, 0)),
            pl.BlockSpec((ROW_TILE, k), lambda i: (i, 0)),
            pl.BlockSpec((None, k, d), lambda i: (wl, 0, 0)),
        ],
        out_specs=pl.BlockSpec((ROW_TILE, d), lambda i: (i, 0)),
        out_shape=jax.ShapeDtypeStruct((m, d), F32),
        compiler_params=_params("parallel"),
        name="mixer_out_proj",
    )(h, o, w)


def _swa_kernel(sink_ref, q_ref, kvc_ref, kvp_ref, rc_ref, rp_ref, qn_ref, kn_ref, o_ref):
    i = pl.program_id(1)
    tq = q_ref.shape[0]
    hd = SWA_HEAD_DIM
    half = hd // ROPE_FRACTION // 2
    n_chunks = q_ref.shape[1] // LANES
    heads_per_group = (2 * n_chunks) // SWA_KV_HEADS
    kv_w = SWA_KV_HEADS * hd
    n_kv_chunks = kv_w // LANES
    c1 = (hd ** -0.5) * LOG2E


    row = lax.broadcasted_iota(jnp.int32, (LANES, LANES), 0)
    col = lax.broadcasted_iota(jnp.int32, (LANES, LANES), 1)
    seg = jnp.where((row // hd) == (col // hd), 1.0 / hd, 0.0).astype(BF16)
    eye = _eye(LANES)

    def mean_sq(x):
        xx = x * x
        hi = xx.astype(BF16)
        lo = (xx - hi.astype(F32)).astype(BF16)
        return _dot(hi, seg) + _dot(lo, seg)

    def norm_rope(x, gn, rope):
        return _rope(x * lax.rsqrt(mean_sq(x) + NORM_EPS) * gn, rope, half)

    low = lax.broadcasted_iota(jnp.int32, (2 * tq, LANES), 1) < hd
    top = lax.broadcasted_iota(jnp.int32, (LANES, 2 * tq), 0) < hd
    kn = kn_ref[...]
    k_chunks = []
    for c in range(n_kv_chunks):
        sl = slice(c * LANES, (c + 1) * LANES)
        k_chunks.append(jnp.concatenate([norm_rope(kvp_ref[:, sl], kn, rp_ref[...]),
                                         norm_rope(kvc_ref[:, sl], kn, rc_ref[...])], axis=0))
    v_all = jnp.concatenate([kvp_ref[:, kv_w:2 * kv_w], kvc_ref[:, kv_w:2 * kv_w]], axis=0).astype(BF16)
    vt = _dot_nt(_eye(kv_w), v_all)

    kd, vlo, vhi = [], [], []
    for g in range(SWA_KV_HEADS):
        kc_ = k_chunks[g // 2]
        kc_sw = pltpu.roll(kc_, hd, 1)
        vc_ = vt[(g // 2) * LANES:(g // 2 + 1) * LANES, :]
        vc_sw = pltpu.roll(vc_, hd, 0)
        if g % 2 == 0:
            kd.append(jnp.where(low, kc_, kc_sw).astype(BF16))
            vlo.append(jnp.where(top, vc_, 1.0).astype(BF16))
            vhi.append(jnp.where(top, 1.0, vc_sw).astype(BF16))
        else:
            kd.append(jnp.where(low, kc_sw, kc_).astype(BF16))
            vlo.append(jnp.where(top, vc_sw, 1.0).astype(BF16))
            vhi.append(jnp.where(top, 1.0, vc_).astype(BF16))

    ki = lax.broadcasted_iota(jnp.int32, (2 * tq, tq), 0)
    qi = lax.broadcasted_iota(jnp.int32, (2 * tq, tq), 1)
    rel = qi + tq - ki
    mask = (rel >= 0) & (rel < SWA_WINDOW) & ((ki >= tq) | (i > 0))
    qtop = lax.broadcasted_iota(jnp.int32, (LANES, tq), 0) < hd
    qn = qn_ref[...]

    def st_load(c, st):
        st["x"] = q_ref[:, c * LANES:(c + 1) * LANES]
        st["ms"] = mean_sq(st["x"])

    def st_query(c, st):
        xq = (_rope(st["x"] * lax.rsqrt(st["ms"] + NORM_EPS) * qn, rc_ref[...], half) * c1).astype(BF16)
        st["qt"] = _dot_nt(eye, xq)

    def st_scores(c, st):
        g = (2 * c) // heads_per_group
        qt = st["qt"]
        st["s"] = [_dot(kd[g], jnp.where(qtop, qt, 0.0).astype(BF16)),
                   _dot(kd[g], jnp.where(qtop, 0.0, qt).astype(BF16))]

    def st_values(c, st):
        g = (2 * c) // heads_per_group
        st["pv"] = []
        for par, vpl in ((0, vlo[g]), (1, vhi[g])):
            s = jnp.where(mask, st["s"][par], NEG_INF)
            sink = sink_ref[2 * c + par] * LOG2E
            mx = jnp.maximum(jnp.max(s, axis=0, keepdims=True), sink)
            pv = _dot(vpl, jnp.exp2(s - mx).astype(BF16))
            den = (pv[hd:hd + 1, :] if par == 0 else pv[0:1, :]) + jnp.exp2(sink - mx)
            st["pv"].append(pv * (1.0 / den))

    def st_merge(c, st):
        ot = jnp.where(qtop, st["pv"][0], st["pv"][1]).astype(BF16)
        st["o"] = _dot_nt(eye, ot)

    def st_store(c, st):
        o_ref[:, c * LANES:(c + 1) * LANES] = st["o"].astype(o_ref.dtype)

    _software_pipeline(n_chunks, (st_load, st_query, st_scores, st_values, st_merge, st_store), SWA_SKEW)


def _swa_attention(p, rope, q_norm2, k_norm2, sinks, batch, seq):
    m = p.shape[0]
    tq = SWA_TILE
    nb = seq // tq
    n_q = sinks.shape[0] * SWA_HEAD_DIM
    kv_w = 2 * SWA_KV_HEADS * SWA_HEAD_DIM
    kv_blk = n_q // kv_w
    return pl.pallas_call(
        _swa_kernel,
        grid=(batch, nb),
        in_specs=[
            pl.BlockSpec(memory_space=pltpu.SMEM),
            pl.BlockSpec((tq, n_q), lambda b, i: (b * nb + i, 0)),
            pl.BlockSpec((tq, kv_w), lambda b, i: (b * nb + i, kv_blk)),
            pl.BlockSpec((tq, kv_w), lambda b, i: (b * nb + jnp.maximum(i - 1, 0), kv_blk)),
            pl.BlockSpec((tq, 3 * LANES), lambda b, i: (b * nb + i, 0)),
            pl.BlockSpec((tq, 3 * LANES), lambda b, i: (b * nb + jnp.maximum(i - 1, 0), 0)),
            pl.BlockSpec((1, LANES), lambda b, i: (0, 0)),
            pl.BlockSpec((1, LANES), lambda b, i: (0, 0)),
        ],
        out_specs=pl.BlockSpec((tq, n_q), lambda b, i: (b * nb + i, 0)),
        out_shape=jax.ShapeDtypeStruct((m, n_q), BF16),
        compiler_params=_params("parallel", "arbitrary"),
        name="swa_attention",
    )(sinks, p, p, p, rope, rope, q_norm2, k_norm2)


def _nsa_compress_kernel(kc_ref, vc_ref, ropec_ref, kn_ref, pek_ref, w1k_ref, w2k_ref,
                         pev_ref, w1v_ref, w2v_ref, kco_ref, vco_ref):
    half = NSA_HEAD_DIM // ROPE_FRACTION // 2
    n_chunk = kc_ref.shape[0] // CMP_STRIDE
    per = CMP_BLOCK // CMP_STRIDE

    def compress(t_ref, pe_ref, w1_ref, w2_ref):
        parts = []
        for h in range(per):
            acc = jnp.zeros((n_chunk, w1_ref.shape[2]), F32)
            for l in range(CMP_STRIDE):
                ll = h * CMP_STRIDE + l
                xl = t_ref[pl.ds(l, n_chunk, stride=CMP_STRIDE), :] + pe_ref[ll:ll + 1, :]
                acc = acc + _dot(xl.astype(BF16), w1_ref[ll])
            parts.append(acc)
        pre = parts[0]
        for h in range(1, per):
            pre = pre + pltpu.roll(parts[h], n_chunk - h, 0)
        hdn = pre * _sigmoid(pre)
        return _dot(hdn.astype(BF16), w2_ref[...])

    kcmp = compress(kc_ref, pek_ref, w1k_ref, w2k_ref)
    kco_ref[...] = _rope(_rms(kcmp, kn_ref[...]), ropec_ref[...], half).astype(BF16)
    vco_ref[...] = _transpose_bf16(compress(vc_ref, pev_ref, w1v_ref, w2v_ref).astype(BF16))


def _nsa_compress(p, rope_c, k_norm, pe_k, w1_k, w2_k, pe_v, w1_v, w2_v, batch, seq):
    d = NSA_HEAD_DIM
    g_n = NSA_KV_HEADS
    n_chunk = seq // CMP_STRIDE
    q_blocks = 4 * g_n

    def col(kind):
        return pl.BlockSpec((seq, d), lambda b, g, kind=kind: (b, q_blocks + kind * g_n + g))

    def full(a):
        return pl.BlockSpec(a.shape, lambda b, g, nd=a.ndim: (0,) * nd)

    return pl.pallas_call(
        _nsa_compress_kernel,
        grid=(batch, g_n),
        in_specs=[col(0), col(1),
                  pl.BlockSpec((n_chunk, 3 * LANES), lambda b, g: (b, 0)),
                  full(k_norm), full(pe_k), full(w1_k), full(w2_k), full(pe_v), full(w1_v), full(w2_v)],
        out_specs=[pl.BlockSpec((None, None, n_chunk, d), lambda b, g: (b, g, 0, 0)),
                   pl.BlockSpec((None, None, d, n_chunk), lambda b, g: (b, g, 0, 0))],
        out_shape=[jax.ShapeDtypeStruct((batch, g_n, n_chunk, d), BF16),
                   jax.ShapeDtypeStruct((batch, g_n, d, n_chunk), BF16)],
        compiler_params=_params("parallel", "arbitrary"),
        name="nsa_compress",
    )(p, p, rope_c, k_norm, pe_k, w1_k, w2_k, pe_v, w1_v, w2_v)


def _nsa_qkv_kernel(q_ref, ks_ref, vs_ref, kw_ref, vw_ref, rope_ref, ropet_ref, qnt_ref, kn_ref,
                    qt_ref, kso_ref, vst_ref, kwo_ref, vwt_ref):
    d = NSA_HEAD_DIM
    half = d // ROPE_FRACTION // 2
    rep = q_ref.shape[1] // d
    tq = NSA_TILE
    eye = _eye(d)
    for s in range(q_ref.shape[0] // tq):
        rows = slice(s * tq, (s + 1) * tq)
        cos = ropet_ref[0:half, rows]
        sin = ropet_ref[half:2 * half, rows]
        for r in range(rep):
            x = q_ref[rows, r * d:(r + 1) * d]
            hi = x.astype(BF16)
            rem = x - hi.astype(F32)
            mid = rem.astype(BF16)
            lo = (rem - mid.astype(F32)).astype(BF16)
            xt = _dot_nt(eye, hi) + _dot_nt(eye, mid) + _dot_nt(eye, lo)
            xn = xt * lax.rsqrt(jnp.mean(xt * xt, axis=0, keepdims=True) + NORM_EPS) * qnt_ref[...]
            x1, x2 = xn[0:half, :], xn[half:2 * half, :]
            rot = jnp.concatenate([x1 * cos - x2 * sin, x2 * cos + x1 * sin, xn[2 * half:, :]], axis=0)
            qt_ref[:, (s * rep + r) * tq:(s * rep + r + 1) * tq] = (rot * NSA_QSCALE).astype(BF16)
    kso_ref[...] = _rope(_rms(ks_ref[...], kn_ref[...]), rope_ref[...], half).astype(BF16)
    kwo_ref[...] = _rope(_rms(kw_ref[...], kn_ref[...]), rope_ref[...], half).astype(BF16)
    ones = jnp.ones((NSA_V_ROWS - d, vs_ref.shape[0]), BF16)
    vst_ref[0:d, :] = _dot_nt(eye, vs_ref[...].astype(BF16)).astype(BF16)
    vst_ref[d:NSA_V_ROWS, :] = ones
    vwt_ref[0:d, :] = _dot_nt(eye, vw_ref[...].astype(BF16)).astype(BF16)
    vwt_ref[d:NSA_V_ROWS, :] = ones


def _nsa_qkv(p, rope, rope_t, q_norm_t, k_norm, batch, seq):
    d = NSA_HEAD_DIM
    g_n = NSA_KV_HEADS
    rep = 4
    tt = NSA_PREP_TILE
    nt = seq // tt
    q_blocks = rep * g_n

    def col(kind):
        return pl.BlockSpec((tt, d), lambda b, g, t, kind=kind: (b * nt + t, q_blocks + kind * g_n + g))

    nat = pl.BlockSpec((None, None, tt, d), lambda b, g, t: (b, g, t, 0))
    tra = pl.BlockSpec((None, None, NSA_V_ROWS, tt), lambda b, g, t: (b, g, 0, t))
    sh_nat = jax.ShapeDtypeStruct((batch, g_n, seq, d), BF16)
    sh_tra = jax.ShapeDtypeStruct((batch, g_n, NSA_V_ROWS, seq), BF16)
    return pl.pallas_call(
        _nsa_qkv_kernel,
        grid=(batch, g_n, nt),
        in_specs=[pl.BlockSpec((tt, rep * d), lambda b, g, t: (b * nt + t, g)),
                  col(2), col(3), col(4), col(5),
                  pl.BlockSpec((tt, 3 * LANES), lambda b, g, t: (b * nt + t, 0)),
                  pl.BlockSpec((None, rope_t.shape[1], tt), lambda b, g, t: (b, 0, t)),
                  pl.BlockSpec(q_norm_t.shape, lambda b, g, t: (0, 0)),
                  pl.BlockSpec((1, d), lambda b, g, t: (0, 0))],
        out_specs=[pl.BlockSpec((None, None, d, rep * tt), lambda b, g, t: (b, g, 0, t)), nat, tra, nat, tra],
        out_shape=[jax.ShapeDtypeStruct((batch, g_n, d, rep * seq), BF16), sh_nat, sh_tra, sh_nat, sh_tra],
        compiler_params=_params("parallel", "parallel", "arbitrary"),
        name="nsa_qkv_prep",
    )(p, p, p, p, p, rope, rope_t, q_norm_t, k_norm)


def _nsa_gate_kernel(g_ref, o_ref):
    o_ref[...] = _sigmoid(g_ref[...]).T


def _nsa_gates(p, batch, seq):
    tt = NSA_PREP_TILE
    nt = seq // tt
    gate_blk = 10 * NSA_KV_HEADS
    return pl.pallas_call(
        _nsa_gate_kernel,
        grid=(batch, nt),
        in_specs=[pl.BlockSpec((tt, LANES), lambda b, t: (b * nt + t, gate_blk))],
        out_specs=pl.BlockSpec((None, LANES, tt), lambda b, t: (b, 0, t)),
        out_shape=jax.ShapeDtypeStruct((batch, LANES, seq), F32),
        compiler_params=_params("parallel", "arbitrary"),
        name="nsa_gates",
    )(p)


def _nsa_kernel(qt_ref, gt_ref, kc_ref, vct_ref, ks_ref, vst_ref, kw_ref, vwt_ref, ovl_ref, o_ref,
                sel_scr, score_scr, cnt_scr, base_scr, acc_scr, s0_scr, sw_scr, bias_scr, p_scr):
    g = pl.program_id(1)
    i = pl.program_id(2)
    d = NSA_HEAD_DIM
    rows = qt_ref.shape[1]
    tq = o_ref.shape[0]
    rep = rows // tq
    n_cmp = kc_ref.shape[0]
    n_slc = ovl_ref.shape[0]
    n_heads = NSA_KV_HEADS * rep
    tk = SEL_KEY_TILE
    ch = SLC_BLOCK
    q0 = i * tq
    qt = qt_ref[...]

    def head_cols(r):
        return slice(r * tq, (r + 1) * tq)

    def pair_cols(pr):
        return slice(pr * PAIR * tq, (pr + 1) * PAIR * tq)

    def gate_row(branch, r):
        return gt_ref[pl.ds(branch * n_heads + g * rep + r, 1), :]

    def fold_max(x):
        parts = [x[j * SUBLANES:(j + 1) * SUBLANES, :] for j in range(x.shape[0] // SUBLANES)]
        while len(parts) > 1:
            parts = [jnp.maximum(parts[j], parts[j + 1]) for j in range(0, len(parts), 2)]
        return parts[0]

    def chunked_softmax(chunk, n_chunks, m_floor, p_slot, p_cols):
        m8 = fold_max(chunk(0))
        for c in range(1, n_chunks):
            m8 = jnp.maximum(m8, fold_max(chunk(c)))
        m = jnp.max(m8, axis=0, keepdims=True)
        if m_floor is not None:
            m = jnp.maximum(m, m_floor)
        for c in range(n_chunks):
            p_scr[p_slot, c * ch:(c + 1) * ch, p_cols] = jnp.exp2(chunk(c) - m).astype(BF16)
        return m

    wk = NSA_WINDOW + tq
    start = pl.multiple_of(jnp.maximum(q0 - NSA_WINDOW, 0), tq)

    @pl.when(i <= NSA_WINDOW // tq)
    def _():
        for c in range(wk // ch):
            t_w = q0 + lax.broadcasted_iota(jnp.int32, (ch, tq), 1)
            k_w = start + c * ch + lax.broadcasted_iota(jnp.int32, (ch, tq), 0)
            ok = (k_w <= t_w) & (t_w - k_w < NSA_WINDOW)
            bias_scr[c * ch:(c + 1) * ch, :] = jnp.where(ok, 0.0, NEG_INF)

    sw_scr[0] = _dot(kw_ref[pl.ds(start, wk), :], qt[:, pair_cols(0)])
    sc_raw = _dot(kc_ref[...], qt)
    for pr in range(1, rep // PAIR):
        sw_scr[pr] = _dot(kw_ref[pl.ds(start, wk), :], qt[:, pair_cols(pr)])
    s0_scr[...] = _dot(ks_ref[0:tk, :], qt[:, pair_cols(0)])

    t_c = q0 + lax.broadcasted_iota(jnp.int32, (n_cmp, tq), 1)
    end_c = lax.broadcasted_iota(jnp.int32, (n_cmp, tq), 0) * CMP_STRIDE + (CMP_BLOCK - 1)
    mask_c = end_c <= t_c

    imp = jnp.zeros((n_slc, tq), F32)
    for pr in range(rep // PAIR):
        pcs = []
        for h in range(PAIR):
            cols = head_cols(pr * PAIR + h)
            chunked_softmax(lambda c, pr=pr, h=h: (sw_scr[pr, c * ch:(c + 1) * ch, head_cols(h)]
                                                   + bias_scr[c * ch:(c + 1) * ch, :]),
                            wk // ch, None, pr % 2, head_cols(h))
            sc = jnp.where(mask_c, sc_raw[:, cols], NEG_INF)
            ec = jnp.where(mask_c, jnp.exp2(sc - jnp.max(sc, axis=0, keepdims=True)), 0.0)
            pcs.append((ec / jnp.maximum(jnp.sum(ec, axis=0, keepdims=True), TINY)).astype(BF16))
        pc = jnp.concatenate(pcs, axis=1)
        o_win = _dot(vwt_ref[:, pl.ds(start, wk)], p_scr[pr % 2, 0:wk, :])
        o_cmp = _dot(vct_ref[...], pc)
        imp_pair = _dot(ovl_ref[...], pc)
        for h in range(PAIR):
            r = pr * PAIR + h
            hc = head_cols(h)
            imp = imp + imp_pair[:, hc]
            base_scr[:, head_cols(r)] = (gate_row(0, r) * o_cmp[:, hc]
                                         + (gate_row(2, r) / o_win[d:d + 1, hc]) * o_win[0:d, hc])

    jb = lax.broadcasted_iota(jnp.int32, (n_slc, tq), 0)
    tt = q0 + lax.broadcasted_iota(jnp.int32, (n_slc, tq), 1)
    cur = tt // SLC_BLOCK
    forced = (jb == 0) | (jb == cur) | (jb == cur - 1)
    score = jnp.where(forced, FORCE_SCORE, jnp.where(jb * SLC_BLOCK <= tt, imp, NEG_INF))
    sub = SUBLANES
    n_grp = n_slc // sub
    score_scr[...] = score
    cnt_scr[...] = jnp.zeros(cnt_scr.shape, F32)
    jsub = lax.broadcasted_iota(jnp.int32, (sub, tq), 0)
    for gi in range(n_grp):
        @pl.when(gi * sub * SLC_BLOCK <= q0 + tq - 1)
        def _(gi=gi):
            blocks = [score_scr[v * sub:(v + 1) * sub, :] for v in range(n_grp)]
            counts = [cnt_scr[v * sub:(v + 1) * sub, :] for v in range(n_grp)]
            for ii in range(gi * sub, (gi + 1) * sub):
                rowv = score_scr[ii:ii + 1, :]
                for v, blk in enumerate(blocks):
                    if (v + 1) * sub - 1 <= ii:
                        beat = jnp.where(rowv > blk, 1.0, 0.0)
                    elif v * sub > ii:
                        beat = jnp.where(rowv >= blk, 1.0, 0.0)
                    else:
                        beat = jnp.where(jsub + v * sub > ii, jnp.where(rowv >= blk, 1.0, 0.0),
                                         jnp.where(rowv > blk, 1.0, 0.0))
                    counts[v] = counts[v] + beat
            for v, cnt in enumerate(counts):
                cnt_scr[v * sub:(v + 1) * sub, :] = cnt
    n_top = min(SLC_TOP_N, n_slc)
    sel_scr[...] = jnp.where(cnt_scr[...] < n_top, 1.0, 0.0)

    acc_scr[...] = jnp.zeros(acc_scr.shape, F32)
    blk_per_tile = tk // ch
    t_b = q0 + lax.broadcasted_iota(jnp.int32, (ch, tq), 1)
    k_b = lax.broadcasted_iota(jnp.int32, (ch, tq), 0)
    n_full = q0 // tk

    def sel_step(kt, m_prev, diagonal):
        k0 = pl.multiple_of(kt * tk, tk)
        keys = ks_ref[pl.ds(k0, tk), :]
        sel_bias = [(sel_scr[pl.ds(kt * blk_per_tile + bb, 1), :] - 1.0) * (-NEG_INF)
                    for bb in range(blk_per_tile)]
        n_pairs = rep // PAIR
        scores = {1: _dot(keys, qt[:, pair_cols(1)])}
        m_out = []
        for pr in range(n_pairs):
            alphas = []
            for h in range(PAIR):
                r = pr * PAIR + h

                def chunk(bb, pr=pr, h=h):
                    rows_bb = slice(bb * ch, (bb + 1) * ch)
                    x = (s0_scr[rows_bb, head_cols(h)] if pr == 0 else scores[pr][rows_bb, head_cols(h)])
                    x = x + sel_bias[bb]
                    if diagonal:
                        x = jnp.where(k_b + (k0 + bb * ch) <= t_b, x, NEG_INF)
                    return x

                m_new = chunked_softmax(chunk, blk_per_tile, m_prev[r], pr % 2, head_cols(h))
                alphas.append(jnp.exp2(m_prev[r] - m_new))
                m_out.append(m_new)
            pv = _dot(vst_ref[:, pl.ds(k0, tk)], p_scr[pr % 2, 0:tk, :])
            for h in range(PAIR):
                cols = head_cols(pr * PAIR + h)
                acc_scr[:, cols] = alphas[h] * acc_scr[:, cols] + pv[:, head_cols(h)]
            if pr + 2 < n_pairs:
                scores[pr + 2] = _dot(keys, qt[:, pair_cols(pr + 2)])
            elif pr + 2 == n_pairs and not diagonal:
                k1 = pl.multiple_of(k0 + tk, tk)
                s0_scr[...] = _dot(ks_ref[pl.ds(k1, tk), :], qt[:, pair_cols(0)])
        return tuple(m_out)

    m_run = lax.fori_loop(0, n_full, lambda kt, m: sel_step(kt, m, False),
                          tuple(jnp.full((1, tq), NEG_INF, F32) for _ in range(rep)))
    sel_step(n_full, m_run, True)

    for r in range(rep):
        cols = slice(r * tq, (r + 1) * tq)
        out_t = base_scr[:, cols] + (gate_row(1, r) / acc_scr[d:d + 1, cols]) * acc_scr[0:d, cols]
        o_ref[:, r * d:(r + 1) * d] = _transpose_bf16(out_t.astype(BF16)).astype(o_ref.dtype)


def _nsa_attend(qt, gt, kc, vct, ks, vst, kw, vwt, ovl_t, batch, seq):
    d = NSA_HEAD_DIM
    g_n = NSA_KV_HEADS
    tq = NSA_TILE
    nq = seq // tq
    rep = qt.shape[3] // seq
    n_chunk = kc.shape[2]
    n_slc = ovl_t.shape[0]
    gate_rows = -(-3 * g_n * rep // SUBLANES) * SUBLANES

    def kv(shape):
        return pl.BlockSpec((None, None) + shape, lambda b, g, i: (b, g, 0, 0))

    return pl.pallas_call(
        _nsa_kernel,
        grid=(batch, g_n, nq),
        in_specs=[
            pl.BlockSpec((None, None, d, rep * tq), lambda b, g, i: (b, g, 0, i)),
            pl.BlockSpec((None, gate_rows, tq), lambda b, g, i: (b, 0, i)),
            kv((n_chunk, d)), kv((d, n_chunk)),
            kv((seq, d)), kv((NSA_V_ROWS, seq)), kv((seq, d)), kv((NSA_V_ROWS, seq)),
            pl.BlockSpec(ovl_t.shape, lambda b, g, i: (0, 0)),
        ],
        out_specs=pl.BlockSpec((tq, rep * d), lambda b, g, i: (b * nq + i, g)),
        out_shape=jax.ShapeDtypeStruct((batch * seq, g_n * rep * d), BF16),
        scratch_shapes=[
            pltpu.VMEM((n_slc, tq), F32),
            pltpu.VMEM((n_slc, tq), F32),
            pltpu.VMEM((n_slc, tq), F32),
            pltpu.VMEM((d, rep * tq), F32),
            pltpu.VMEM((NSA_V_ROWS, rep * tq), F32),
            pltpu.VMEM((SEL_KEY_TILE, PAIR * tq), F32),
            pltpu.VMEM((rep // PAIR, NSA_WINDOW + tq, PAIR * tq), F32),
            pltpu.VMEM((NSA_WINDOW + tq, tq), F32),
            pltpu.VMEM((2, max(NSA_WINDOW + tq, SEL_KEY_TILE), PAIR * tq), BF16),
        ],
        compiler_params=_params("parallel", "parallel", "arbitrary"),
        name="nsa_attention",
    )(qt, gt, kc, vct, ks, vst, kw, vwt, ovl_t)


def _rope_table(pos, head_dim):
    rd = head_dim // ROPE_FRACTION
    half = rd // 2
    inv = 1.0 / (ROPE_THETA ** (jnp.arange(half, dtype=F32) * (2.0 / rd)))
    ang = pos.astype(F32)[:, None] * inv
    cos, sin = jnp.cos(ang), jnp.sin(ang)
    n = pos.shape[0]
    ones = jnp.ones((n, head_dim - rd), F32)
    zeros = jnp.zeros((n, head_dim - rd), F32)
    zh = jnp.zeros((n, half), F32)
    c = jnp.concatenate([cos, cos, ones], axis=1)
    s1 = jnp.concatenate([-sin, zh, zeros], axis=1)
    s2 = jnp.concatenate([zh, sin, zeros], axis=1)
    reps = LANES // head_dim
    return jnp.concatenate([jnp.tile(c, (1, reps)), jnp.tile(s1, (1, reps)), jnp.tile(s2, (1, reps))], axis=1)


def _rope_table_t(positions, head_dim):
    rd = head_dim // ROPE_FRACTION
    half = rd // 2
    inv = 1.0 / (ROPE_THETA ** (jnp.arange(half, dtype=F32) * (2.0 / rd)))
    ang = positions.astype(F32)[:, None, :] * inv[None, :, None]
    return jnp.concatenate([jnp.cos(ang), jnp.sin(ang)], axis=1)


def _overlap_matrix(seq):
    n_cmp_pad = seq // CMP_STRIDE
    n_slc = seq // SLC_BLOCK
    cs = np.arange(n_cmp_pad)[None, :] * CMP_STRIDE
    ss = np.arange(n_slc)[:, None] * SLC_BLOCK
    ovl_t = ((cs < ss + SLC_BLOCK) & (cs + CMP_BLOCK > ss)).astype(np.float32)
    return jnp.asarray(ovl_t, BF16)


def kernel(x, positions, ffn1_norm, ffn1_w_in, ffn1_w_out, mix_norm, ffn2_norm, ffn2_w_in, ffn2_w_out,
           swa_w_in, swa_q_norm, swa_k_norm, swa_sinks, swa_w_out,
           nsa_w_in, nsa_q_norm, nsa_k_norm, nsa_cmp_pe_k, nsa_cmp_w1_k, nsa_cmp_w2_k,
           nsa_cmp_pe_v, nsa_cmp_w1_v, nsa_cmp_w2_v, nsa_w_out):
    batch, seq, d_model = x.shape
    depth = ffn1_norm.shape[0]
    m = batch * seq
    h = x.reshape(m, d_model)

    f1_in, f1_out = ffn1_w_in.astype(BF16), ffn1_w_out.astype(BF16)
    f2_in, f2_out = ffn2_w_in.astype(BF16), ffn2_w_out.astype(BF16)
    swa_in, swa_out = swa_w_in.astype(BF16), swa_w_out.astype(BF16)
    nsa_n = nsa_w_in.shape[2]
    nsa_tn = NSA_PROJ_TILE
    nsa_pad = -nsa_n % nsa_tn
    nsa_in = jnp.pad(nsa_w_in.astype(BF16), ((0, 0), (0, 0), (0, nsa_pad)))
    nsa_out = nsa_w_out.astype(BF16)
    w1_k, w2_k = nsa_cmp_w1_k.astype(BF16), nsa_cmp_w2_k.astype(BF16)
    w1_v, w2_v = nsa_cmp_w1_v.astype(BF16), nsa_cmp_w2_v.astype(BF16)

    n1 = ffn1_norm.reshape(depth, 1, d_model)
    nm = mix_norm.reshape(depth, 1, d_model)
    n2 = ffn2_norm.reshape(depth, 1, d_model)

    pos_flat = positions.reshape(m)
    rope_swa = _rope_table(pos_flat, SWA_HEAD_DIM)
    rope_nsa = _rope_table(pos_flat, NSA_HEAD_DIM)
    rope_nsa_t = _rope_table_t(positions, NSA_HEAD_DIM)
    n_chunk = seq // CMP_STRIDE
    end_idx = jnp.minimum(jnp.arange(n_chunk) * CMP_STRIDE + (CMP_BLOCK - 1), seq - 1)
    rope_cmp = _rope_table(positions[:, end_idx].reshape(batch * n_chunk), NSA_HEAD_DIM)
    ovl_t = _overlap_matrix(seq)

    for i in range(depth):
        h = _ffn(h, n1, f1_in, f1_out, i)
        j = i // N_MIXERS
        if i % N_MIXERS == 0:
            p = _proj(h, nm, swa_in, i, j, SWA_PROJ_TILE)
            qn2 = jnp.tile(swa_q_norm[j], LANES // SWA_HEAD_DIM).reshape(1, LANES)
            kn2 = jnp.tile(swa_k_norm[j], LANES // SWA_HEAD_DIM).reshape(1, LANES)
            o = _swa_attention(p, rope_swa, qn2, kn2, swa_sinks[j], batch, seq)
            h = _oproj(h, o, swa_out, j)
        else:
            p = _proj(h, nm, nsa_in, i, j, nsa_tn)
            kn = nsa_k_norm[j].reshape(1, NSA_HEAD_DIM)
            kc, vct = _nsa_compress(p, rope_cmp, kn, nsa_cmp_pe_k[j], w1_k[j], w2_k[j],
                                    nsa_cmp_pe_v[j], w1_v[j], w2_v[j], batch, seq)
            qn_t = jnp.broadcast_to(nsa_q_norm[j][:, None], (NSA_HEAD_DIM, NSA_TILE))
            qt, ks, vst, kw, vwt = _nsa_qkv(p, rope_nsa, rope_nsa_t, qn_t, kn, batch, seq)
            gt = _nsa_gates(p, batch, seq)
            o = _nsa_attend(qt, gt, kc, vct, ks, vst, kw, vwt, ovl_t, batch, seq)
            h = _oproj(h, o, nsa_out, j)
        h = _ffn(h, n2, f2_in, f2_out, i)
    return h.reshape(batch, seq, d_model)
```

```python
import math

import jax
import jax.numpy as jnp
import numpy as np
from jax import lax
from jax.experimental import pallas as pl
from jax.experimental.pallas import tpu as pltpu

F32 = jnp.float32
BF16 = jnp.bfloat16

N_MIXERS = 2
SWA_HEAD_DIM = 64
SWA_KV_HEADS = 4
SWA_WINDOW = 128
NSA_HEAD_DIM = 128
NSA_KV_HEADS = 4
CMP_BLOCK = 32
CMP_STRIDE = 16
SLC_BLOCK = 64
SLC_TOP_N = 16
NSA_WINDOW = 512
ROPE_THETA = 500000.0
ROPE_FRACTION = 4
NORM_EPS = 1e-6
NEG_INF = -1e30
FORCE_SCORE = 1e9
TINY = 1e-30
LOG2E = math.log2(math.e)

LANES = 128
SUBLANES = 8
VMEM_LIMIT = 56 * 1024 * 1024

ROW_TILE = 512
PROJ_ROW_TILE = 1024
FFN_ROW_TILE = 1024
SWA_PROJ_TILE = 1280
NSA_PROJ_TILE = 1792
FFN_TILE = 512
FFN_SPLIT = 2
SWA_TILE = 128
SWA_BLOCKS = 4
SWA_SKEW = 4
NSA_TILE = 256
NSA_PREP_TILE = 1024
NSA_V_ROWS = NSA_HEAD_DIM + 16
NSA_QSCALE = (NSA_HEAD_DIM ** -0.5) * LOG2E
PAIR = 2
SEL_KEY_TILE = 512


def _params(*sem):
    return pltpu.CompilerParams(dimension_semantics=sem, vmem_limit_bytes=VMEM_LIMIT)


def _dot(a, b):
    return jnp.dot(a, b, preferred_element_type=F32)


def _dot_nt(a, b):
    return lax.dot_general(a, b, (((1,), (1,)), ((), ())), preferred_element_type=F32)


def _eye(n):
    r = lax.broadcasted_iota(jnp.int32, (n, n), 0)
    c = lax.broadcasted_iota(jnp.int32, (n, n), 1)
    return jnp.where(r == c, 1.0, 0.0).astype(BF16)


def _transpose_bf16(x):
    return _dot_nt(_eye(x.shape[1]), x).astype(BF16)


def _software_pipeline(n_items, stages, skew):
    state = [{} for _ in range(n_items)]
    for t in range(n_items + skew * (len(stages) - 1)):
        for k, stage in enumerate(stages):
            c = t - k * skew
            if 0 <= c < n_items:
                stage(c, state[c])


def _rms(x, g):
    ms = jnp.mean(x * x, axis=-1, keepdims=True)
    return x * lax.rsqrt(ms + NORM_EPS) * g


def _sigmoid(x):
    return 1.0 / (1.0 + jnp.exp(-x))


def _rope(xn, rope, half):
    c, s1, s2 = rope[:, 0:LANES], rope[:, LANES:2 * LANES], rope[:, 2 * LANES:3 * LANES]
    return xn * c + pltpu.roll(xn, LANES - half, 1) * s1 + pltpu.roll(xn, half, 1) * s2


def _ffn_kernel(x_ref, g_ref, wg_ref, wu_ref, wo_ref, o_ref, xn_ref):
    def hidden_update(xn):
        w = wg_ref.shape[1] // FFN_SPLIT
        pre = [(_dot(xn, wg_ref[:, s * w:(s + 1) * w]), _dot(xn, wu_ref[:, s * w:(s + 1) * w]))
               for s in range(FFN_SPLIT)]
        upd = None
        for s, (gate, up) in enumerate(pre):
            hdn = (gate * _sigmoid(gate) * (0.5 * up)).astype(BF16)
            part = _dot(hdn, wo_ref[s * w:(s + 1) * w, :])
            upd = part if upd is None else upd + part
        return upd

    first = pl.program_id(1) == 0

    @pl.when(first)
    def _():
        x = x_ref[...]
        xn = _rms(x, g_ref[...]).astype(BF16)
        xn_ref[...] = xn
        o_ref[...] = x + hidden_update(xn)

    @pl.when(jnp.logical_not(first))
    def _():
        o_ref[...] += hidden_update(xn_ref[...])


def _ffn(h, norm, w_in, w_out, layer):
    m, d = h.shape
    f = w_out.shape[1]
    nf = f // FFN_TILE
    return pl.pallas_call(
        _ffn_kernel,
        grid=(m // FFN_ROW_TILE, nf),
        in_specs=[
            pl.BlockSpec((FFN_ROW_TILE, d), lambda i, j: (i, 0)),
            pl.BlockSpec((None, 1, d), lambda i, j: (layer, 0, 0)),
            pl.BlockSpec((None, d, FFN_TILE), lambda i, j: (layer, 0, j)),
            pl.BlockSpec((None, d, FFN_TILE), lambda i, j: (layer, 0, nf + j)),
            pl.BlockSpec((None, FFN_TILE, d), lambda i, j: (layer, j, 0)),
        ],
        out_specs=pl.BlockSpec((FFN_ROW_TILE, d), lambda i, j: (i, 0)),
        out_shape=jax.ShapeDtypeStruct((m, d), F32),
        scratch_shapes=[pltpu.VMEM((FFN_ROW_TILE, d), BF16)],
        compiler_params=_params("parallel", "arbitrary"),
        name="ffn",
    )(h, norm, w_in, w_in, w_out)


def _proj_kernel(x_ref, g_ref, w_ref, o_ref, xn_ref):
    first = pl.program_id(1) == 0

    @pl.when(first)
    def _():
        xn = _rms(x_ref[...], g_ref[...]).astype(BF16)
        xn_ref[...] = xn
        o_ref[...] = _dot(xn, w_ref[...])

    @pl.when(jnp.logical_not(first))
    def _():
        o_ref[...] = _dot(xn_ref[...], w_ref[...])


def _proj(h, norm, w, layer, wl, tn):
    m, d = h.shape
    n = w.shape[2]
    return pl.pallas_call(
        _proj_kernel,
        grid=(m // PROJ_ROW_TILE, n // tn),
        in_specs=[
            pl.BlockSpec((PROJ_ROW_TILE, d), lambda i, j: (i, 0)),
            pl.BlockSpec((None, 1, d), lambda i, j: (layer, 0, 0)),
            pl.BlockSpec((None, d, tn), lambda i, j: (wl, 0, j)),
        ],
        out_specs=pl.BlockSpec((PROJ_ROW_TILE, tn), lambda i, j: (i, j)),
        out_shape=jax.ShapeDtypeStruct((m, n), F32),
        scratch_shapes=[pltpu.VMEM((PROJ_ROW_TILE, d), BF16)],
        compiler_params=_params("parallel", "arbitrary"),
        name="mixer_in_proj",
    )(h, norm, w)


def _oproj_kernel(h_ref, o_ref, w_ref, out_ref):
    out_ref[...] = h_ref[...] + _dot(o_ref[...], w_ref[...])


def _oproj(h, o, w, wl):
    m, d = h.shape
    k = o.shape[1]
    return pl.pallas_call(
        _oproj_kernel,
        grid=(m // ROW_TILE,),
        in_specs=[
            pl.BlockSpec((ROW_TILE, d), lambda i: (i, 0)),
            pl.BlockSpec((ROW_TILE, k), lambda i: (i, 0)),
            pl.BlockSpec((None, k, d), lambda i: (wl, 0, 0)),
        ],
        out_specs=pl.BlockSpec((ROW_TILE, d), lambda i: (i, 0)),
        out_shape=jax.ShapeDtypeStruct((m, d), F32),
        compiler_params=_params("parallel"),
        name="mixer_out_proj",
    )(h, o, w)


def _swa_kernel(sink_ref, q_ref, kvc_ref, kvp_ref, rc_ref, rp_ref, qn_ref, kn_ref, o_ref):
    i = pl.program_id(1)
    tq = SWA_TILE
    n_blk = q_ref.shape[0] // tq
    n_keys = (n_blk + 1) * tq
    hd = SWA_HEAD_DIM
    half = hd // ROPE_FRACTION // 2
    n_chunks = q_ref.shape[1] // LANES
    heads_per_group = (2 * n_chunks) // SWA_KV_HEADS
    kv_w = SWA_KV_HEADS * hd
    n_kv_chunks = kv_w // LANES
    c1 = (hd ** -0.5) * LOG2E

    row = lax.broadcasted_iota(jnp.int32, (LANES, LANES), 0)
    col = lax.broadcasted_iota(jnp.int32, (LANES, LANES), 1)
    seg = jnp.where((row // hd) == (col // hd), 1.0 / hd, 0.0).astype(BF16)
    eye = _eye(LANES)

    def mean_sq(x):
        xx = x * x
        hi = xx.astype(BF16)
        lo = (xx - hi.astype(F32)).astype(BF16)
        return _dot(hi, seg) + _dot(lo, seg)

    def norm_rope(x, gn, rope):
        return _rope(x * lax.rsqrt(mean_sq(x) + NORM_EPS) * gn, rope, half)

    low = lax.broadcasted_iota(jnp.int32, (n_keys, LANES), 1) < hd
    kn = kn_ref[...]
    k_chunks = []
    for c in range(n_kv_chunks):
        sl = slice(c * LANES, (c + 1) * LANES)
        k_chunks.append(jnp.concatenate([norm_rope(kvp_ref[:, sl], kn, rp_ref[...]),
                                         norm_rope(kvc_ref[:, sl], kn, rc_ref[...])], axis=0))
    v_all = jnp.concatenate([kvp_ref[:, kv_w:2 * kv_w], kvc_ref[:, kv_w:2 * kv_w]], axis=0).astype(BF16)
    vt = _dot_nt(_eye(kv_w), v_all)

    ones_rows = jnp.ones((2 * SUBLANES, n_keys), F32)
    kd, vext = [], []
    for g in range(SWA_KV_HEADS):
        kc_ = k_chunks[g // 2]
        kc_sw = pltpu.roll(kc_, hd, 1)
        kd.append((jnp.where(low, kc_, kc_sw) if g % 2 == 0 else jnp.where(low, kc_sw, kc_)).astype(BF16))
        vext.append(jnp.concatenate([vt[g * hd:(g + 1) * hd, :], ones_rows], axis=0).astype(BF16))

    ki = lax.broadcasted_iota(jnp.int32, (2 * tq, tq), 0)
    qi = lax.broadcasted_iota(jnp.int32, (2 * tq, tq), 1)
    rel = qi + tq - ki
    band = (rel >= 0) & (rel < SWA_WINDOW)
    masks = [band & ((ki >= tq) | (i > 0))] + [band] * (n_blk - 1)
    qtop = lax.broadcasted_iota(jnp.int32, (LANES, tq), 0) < hd
    qn = qn_ref[...]

    def where(it):
        blk, c = divmod(it, n_chunks)
        return blk, c, slice(blk * tq, (blk + 1) * tq), slice(c * LANES, (c + 1) * LANES)

    def st_load(it, st):
        _, _, rows, lanes = where(it)
        st["x"] = q_ref[rows, lanes]
        st["ms"] = mean_sq(st["x"])

    def st_query(it, st):
        _, _, rows, _ = where(it)
        xq = (_rope(st["x"] * lax.rsqrt(st["ms"] + NORM_EPS) * qn, rc_ref[rows, :], half) * c1).astype(BF16)
        st["qt"] = _dot_nt(eye, xq)

    def st_scores(it, st):
        blk, c, _, _ = where(it)
        g = (2 * c) // heads_per_group
        qt = st["qt"]
        q_pair = jnp.concatenate([jnp.where(qtop, qt, 0.0), jnp.where(qtop, 0.0, qt)], axis=1).astype(BF16)
        st["s"] = _dot(kd[g][blk * tq:(blk + 2) * tq, :], q_pair)

    def st_values(it, st):
        blk, c, _, _ = where(it)
        g = (2 * c) // heads_per_group
        es, corr = [], []
        for par in range(2):
            s = jnp.where(masks[blk], st["s"][:, par * tq:(par + 1) * tq], NEG_INF)
            sink = sink_ref[2 * c + par] * LOG2E
            mx = jnp.maximum(jnp.max(s, axis=0, keepdims=True), sink)
            es.append(jnp.exp2(s - mx).astype(BF16))
            corr.append(jnp.exp2(sink - mx))
        st["pv"] = _dot(vext[g][:, blk * tq:(blk + 2) * tq], jnp.concatenate(es, axis=1))
        st["corr"] = corr

    def st_merge(it, st):
        pv = st["pv"]
        parts = []
        for par in range(2):
            cols = slice(par * tq, (par + 1) * tq)
            parts.append(pv[0:hd, cols] * (1.0 / (pv[hd:hd + 1, cols] + st["corr"][par])))
        ot = jnp.concatenate(parts, axis=0).astype(BF16)
        st["o"] = _dot_nt(eye, ot)

    def st_store(it, st):
        _, _, rows, lanes = where(it)
        o_ref[rows, lanes] = st["o"].astype(o_ref.dtype)

    _software_pipeline(n_blk * n_chunks, (st_load, st_query, st_scores, st_values, st_merge, st_store), SWA_SKEW)


def _swa_attention(p, rope, q_norm2, k_norm2, sinks, batch, seq):
    m = p.shape[0]
    tq = SWA_TILE
    ts = SWA_BLOCKS * tq
    ns = seq // ts
    n_q = sinks.shape[0] * SWA_HEAD_DIM
    kv_w = 2 * SWA_KV_HEADS * SWA_HEAD_DIM
    kv_blk = n_q // kv_w

    def prev_blk(b, i):
        return (b * ns + i) * SWA_BLOCKS - jnp.minimum(i, 1)

    return pl.pallas_call(
        _swa_kernel,
        grid=(batch, ns),
        in_specs=[
            pl.BlockSpec(memory_space=pltpu.SMEM),
            pl.BlockSpec((ts, n_q), lambda b, i: (b * ns + i, 0)),
            pl.BlockSpec((ts, kv_w), lambda b, i: (b * ns + i, kv_blk)),
            pl.BlockSpec((tq, kv_w), lambda b, i: (prev_blk(b, i), kv_blk)),
            pl.BlockSpec((ts, 3 * LANES), lambda b, i: (b * ns + i, 0)),
            pl.BlockSpec((tq, 3 * LANES), lambda b, i: (prev_blk(b, i), 0)),
            pl.BlockSpec((1, LANES), lambda b, i: (0, 0)),
            pl.BlockSpec((1, LANES), lambda b, i: (0, 0)),
        ],
        out_specs=pl.BlockSpec((ts, n_q), lambda b, i: (b * ns + i, 0)),
        out_shape=jax.ShapeDtypeStruct((m, n_q), BF16),
        compiler_params=_params("parallel", "arbitrary"),
        name="swa_attention",
    )(sinks, p, p, p, rope, rope, q_norm2, k_norm2)


def _nsa_compress_kernel(kc_ref, vc_ref, ropec_ref, kn_ref, pek_ref, w1k_ref, w2k_ref,
                         pev_ref, w1v_ref, w2v_ref, kco_ref, vco_ref):
    half = NSA_HEAD_DIM // ROPE_FRACTION // 2
    n_chunk = kc_ref.shape[0] // CMP_STRIDE
    per = CMP_BLOCK // CMP_STRIDE

    def compress(t_ref, pe_ref, w1_ref, w2_ref):
        parts = []
        for h in range(per):
            acc = jnp.zeros((n_chunk, w1_ref.shape[2]), F32)
            for l in range(CMP_STRIDE):
                ll = h * CMP_STRIDE + l
                xl = t_ref[pl.ds(l, n_chunk, stride=CMP_STRIDE), :] + pe_ref[ll:ll + 1, :]
                acc = acc + _dot(xl.astype(BF16), w1_ref[ll])
            parts.append(acc)
        pre = parts[0]
        for h in range(1, per):
            pre = pre + pltpu.roll(parts[h], n_chunk - h, 0)
        hdn = pre * _sigmoid(pre)
        return _dot(hdn.astype(BF16), w2_ref[...])

    kcmp = compress(kc_ref, pek_ref, w1k_ref, w2k_ref)
    kco_ref[...] = _rope(_rms(kcmp, kn_ref[...]), ropec_ref[...], half).astype(BF16)
    vco_ref[...] = _transpose_bf16(compress(vc_ref, pev_ref, w1v_ref, w2v_ref).astype(BF16))


def _nsa_compress(p, rope_c, k_norm, pe_k, w1_k, w2_k, pe_v, w1_v, w2_v, batch, seq):
    d = NSA_HEAD_DIM
    g_n = NSA_KV_HEADS
    n_chunk = seq // CMP_STRIDE
    q_blocks = 4 * g_n

    def col(kind):
        return pl.BlockSpec((seq, d), lambda b, g, kind=kind: (b, q_blocks + kind * g_n + g))

    def full(a):
        return pl.BlockSpec(a.shape, lambda b, g, nd=a.ndim: (0,) * nd)

    return pl.pallas_call(
        _nsa_compress_kernel,
        grid=(batch, g_n),
        in_specs=[col(0), col(1),
                  pl.BlockSpec((n_chunk, 3 * LANES), lambda b, g: (b, 0)),
                  full(k_norm), full(pe_k), full(w1_k), full(w2_k), full(pe_v), full(w1_v), full(w2_v)],
        out_specs=[pl.BlockSpec((None, None, n_chunk, d), lambda b, g: (b, g, 0, 0)),
                   pl.BlockSpec((None, None, d, n_chunk), lambda b, g: (b, g, 0, 0))],
        out_shape=[jax.ShapeDtypeStruct((batch, g_n, n_chunk, d), BF16),
                   jax.ShapeDtypeStruct((batch, g_n, d, n_chunk), BF16)],
        compiler_params=_params("parallel", "arbitrary"),
        name="nsa_compress",
    )(p, p, rope_c, k_norm, pe_k, w1_k, w2_k, pe_v, w1_v, w2_v)


def _nsa_qkv_kernel(q_ref, ks_ref, vs_ref, kw_ref, vw_ref, rope_ref, ropet_ref, qnt_ref, kn_ref,
                    qt_ref, kso_ref, vst_ref, kwo_ref, vwt_ref):
    d = NSA_HEAD_DIM
    half = d // ROPE_FRACTION // 2
    rep = q_ref.shape[1] // d
    tq = NSA_TILE
    eye = _eye(d)
    for s in range(q_ref.shape[0] // tq):
        rows = slice(s * tq, (s + 1) * tq)
        cos = ropet_ref[0:half, rows]
        sin = ropet_ref[half:2 * half, rows]
        for r in range(rep):
            x = q_ref[rows, r * d:(r + 1) * d]
            hi = x.astype(BF16)
            rem = x - hi.astype(F32)
            mid = rem.astype(BF16)
            lo = (rem - mid.astype(F32)).astype(BF16)
            xt = _dot_nt(eye, hi) + _dot_nt(eye, mid) + _dot_nt(eye, lo)
            xn = xt * lax.rsqrt(jnp.mean(xt * xt, axis=0, keepdims=True) + NORM_EPS) * qnt_ref[...]
            x1, x2 = xn[0:half, :], xn[half:2 * half, :]
            rot = jnp.concatenate([x1 * cos - x2 * sin, x2 * cos + x1 * sin, xn[2 * half:, :]], axis=0)
            qt_ref[:, (s * rep + r) * tq:(s * rep + r + 1) * tq] = (rot * NSA_QSCALE).astype(BF16)
    kso_ref[...] = _rope(_rms(ks_ref[...], kn_ref[...]), rope_ref[...], half).astype(BF16)
    kwo_ref[...] = _rope(_rms(kw_ref[...], kn_ref[...]), rope_ref[...], half).astype(BF16)
    ones = jnp.ones((NSA_V_ROWS - d, vs_ref.shape[0]), BF16)
    vst_ref[0:d, :] = _dot_nt(eye, vs_ref[...].astype(BF16)).astype(BF16)
    vst_ref[d:NSA_V_ROWS, :] = ones
    vwt_ref[0:d, :] = _dot_nt(eye, vw_ref[...].astype(BF16)).astype(BF16)
    vwt_ref[d:NSA_V_ROWS, :] = ones


def _nsa_qkv(p, rope, rope_t, q_norm_t, k_norm, batch, seq):
    d = NSA_HEAD_DIM
    g_n = NSA_KV_HEADS
    rep = 4
    tt = NSA_PREP_TILE
    nt = seq // tt
    q_blocks = rep * g_n

    def col(kind):
        return pl.BlockSpec((tt, d), lambda b, g, t, kind=kind: (b * nt + t, q_blocks + kind * g_n + g))

    nat = pl.BlockSpec((None, None, tt, d), lambda b, g, t: (b, g, t, 0))
    tra = pl.BlockSpec((None, None, NSA_V_ROWS, tt), lambda b, g, t: (b, g, 0, t))
    sh_nat = jax.ShapeDtypeStruct((batch, g_n, seq, d), BF16)
    sh_tra = jax.ShapeDtypeStruct((batch, g_n, NSA_V_ROWS, seq), BF16)
    return pl.pallas_call(
        _nsa_qkv_kernel,
        grid=(batch, g_n, nt),
        in_specs=[pl.BlockSpec((tt, rep * d), lambda b, g, t: (b * nt + t, g)),
                  col(2), col(3), col(4), col(5),
                  pl.BlockSpec((tt, 3 * LANES), lambda b, g, t: (b * nt + t, 0)),
                  pl.BlockSpec((None, rope_t.shape[1], tt), lambda b, g, t: (b, 0, t)),
                  pl.BlockSpec(q_norm_t.shape, lambda b, g, t: (0, 0)),
                  pl.BlockSpec((1, d), lambda b, g, t: (0, 0))],
        out_specs=[pl.BlockSpec((None, None, d, rep * tt), lambda b, g, t: (b, g, 0, t)), nat, tra, nat, tra],
        out_shape=[jax.ShapeDtypeStruct((batch, g_n, d, rep * seq), BF16), sh_nat, sh_tra, sh_nat, sh_tra],
        compiler_params=_params("parallel", "parallel", "arbitrary"),
        name="nsa_qkv_prep",
    )(p, p, p, p, p, rope, rope_t, q_norm_t, k_norm)


def _nsa_gate_kernel(g_ref, o_ref):
    o_ref[...] = _sigmoid(g_ref[...]).T


def _nsa_gates(p, batch, seq):
    tt = NSA_PREP_TILE
    nt = seq // tt
    gate_blk = 10 * NSA_KV_HEADS
    return pl.pallas_call(
        _nsa_gate_kernel,
        grid=(batch, nt),
        in_specs=[pl.BlockSpec((tt, LANES), lambda b, t: (b * nt + t, gate_blk))],
        out_specs=pl.BlockSpec((None, LANES, tt), lambda b, t: (b, 0, t)),
        out_shape=jax.ShapeDtypeStruct((batch, LANES, seq), F32),
        compiler_params=_params("parallel", "arbitrary"),
        name="nsa_gates",
    )(p)


def _nsa_kernel(qt_ref, gt_ref, kc_ref, vct_ref, ks_ref, vst_ref, kw_ref, vwt_ref, ovl_ref, o_ref,
                sel_scr, score_scr, cnt_scr, base_scr, acc_scr, s0_scr, sw_scr, bias_scr, p_scr):
    g = pl.program_id(1)
    i = pl.program_id(2)
    d = NSA_HEAD_DIM
    rows = qt_ref.shape[1]
    tq = o_ref.shape[0]
    rep = rows // tq
    n_cmp = kc_ref.shape[0]
    n_slc = ovl_ref.shape[0]
    n_heads = NSA_KV_HEADS * rep
    tk = SEL_KEY_TILE
    ch = SLC_BLOCK
    q0 = i * tq
    qt = qt_ref[...]

    def head_cols(r):
        return slice(r * tq, (r + 1) * tq)

    def pair_cols(pr):
        return slice(pr * PAIR * tq, (pr + 1) * PAIR * tq)

    def gate_row(branch, r):
        return gt_ref[pl.ds(branch * n_heads + g * rep + r, 1), :]

    def fold_max(x):
        parts = [x[j * SUBLANES:(j + 1) * SUBLANES, :] for j in range(x.shape[0] // SUBLANES)]
        while len(parts) > 1:
            parts = [jnp.maximum(parts[j], parts[j + 1]) for j in range(0, len(parts), 2)]
        return parts[0]

    def chunked_softmax(chunk, n_chunks, m_floor, p_slot, p_cols):
        m8 = fold_max(chunk(0))
        for c in range(1, n_chunks):
            m8 = jnp.maximum(m8, fold_max(chunk(c)))
        m = jnp.max(m8, axis=0, keepdims=True)
        if m_floor is not None:
            m = jnp.maximum(m, m_floor)
        for c in range(n_chunks):
            p_scr[p_slot, c * ch:(c + 1) * ch, p_cols] = jnp.exp2(chunk(c) - m).astype(BF16)
        return m

    wk = NSA_WINDOW + tq
    start = pl.multiple_of(jnp.maximum(q0 - NSA_WINDOW, 0), tq)

    @pl.when(i <= NSA_WINDOW // tq)
    def _():
        for c in range(wk // ch):
            t_w = q0 + lax.broadcasted_iota(jnp.int32, (ch, tq), 1)
            k_w = start + c * ch + lax.broadcasted_iota(jnp.int32, (ch, tq), 0)
            ok = (k_w <= t_w) & (t_w - k_w < NSA_WINDOW)
            bias_scr[c * ch:(c + 1) * ch, :] = jnp.where(ok, 0.0, NEG_INF)

    sw_scr[0] = _dot(kw_ref[pl.ds(start, wk), :], qt[:, pair_cols(0)])
    sc_raw = _dot(kc_ref[...], qt)
    for pr in range(1, rep // PAIR):
        sw_scr[pr] = _dot(kw_ref[pl.ds(start, wk), :], qt[:, pair_cols(pr)])
    s0_scr[...] = _dot(ks_ref[0:tk, :], qt[:, pair_cols(0)])

    t_c = q0 + lax.broadcasted_iota(jnp.int32, (n_cmp, tq), 1)
    end_c = lax.broadcasted_iota(jnp.int32, (n_cmp, tq), 0) * CMP_STRIDE + (CMP_BLOCK - 1)
    mask_c = end_c <= t_c

    imp = jnp.zeros((n_slc, tq), F32)
    for pr in range(rep // PAIR):
        pcs = []
        for h in range(PAIR):
            cols = head_cols(pr * PAIR + h)
            chunked_softmax(lambda c, pr=pr, h=h: (sw_scr[pr, c * ch:(c + 1) * ch, head_cols(h)]
                                                   + bias_scr[c * ch:(c + 1) * ch, :]),
                            wk // ch, None, pr % 2, head_cols(h))
            sc = jnp.where(mask_c, sc_raw[:, cols], NEG_INF)
            ec = jnp.where(mask_c, jnp.exp2(sc - jnp.max(sc, axis=0, keepdims=True)), 0.0)
            pcs.append((ec / jnp.maximum(jnp.sum(ec, axis=0, keepdims=True), TINY)).astype(BF16))
        pc = jnp.concatenate(pcs, axis=1)
        o_win = _dot(vwt_ref[:, pl.ds(start, wk)], p_scr[pr % 2, 0:wk, :])
        o_cmp = _dot(vct_ref[...], pc)
        imp_pair = _dot(ovl_ref[...], pc)
        for h in range(PAIR):
            r = pr * PAIR + h
            hc = head_cols(h)
            imp = imp + imp_pair[:, hc]
            base_scr[:, head_cols(r)] = (gate_row(0, r) * o_cmp[:, hc]
                                         + (gate_row(2, r) / o_win[d:d + 1, hc]) * o_win[0:d, hc])

    jb = lax.broadcasted_iota(jnp.int32, (n_slc, tq), 0)
    tt = q0 + lax.broadcasted_iota(jnp.int32, (n_slc, tq), 1)
    cur = tt // SLC_BLOCK
    forced = (jb == 0) | (jb == cur) | (jb == cur - 1)
    score = jnp.where(forced, FORCE_SCORE, jnp.where(jb * SLC_BLOCK <= tt, imp, NEG_INF))
    sub = SUBLANES
    n_grp = n_slc // sub
    score_scr[...] = score
    cnt_scr[...] = jnp.zeros(cnt_scr.shape, F32)
    jsub = lax.broadcasted_iota(jnp.int32, (sub, tq), 0)
    for gi in range(n_grp):
        @pl.when(gi * sub * SLC_BLOCK <= q0 + tq - 1)
        def _(gi=gi):
            blocks = [score_scr[v * sub:(v + 1) * sub, :] for v in range(n_grp)]
            counts = [cnt_scr[v * sub:(v + 1) * sub, :] for v in range(n_grp)]
            for ii in range(gi * sub, (gi + 1) * sub):
                rowv = score_scr[ii:ii + 1, :]
                for v, blk in enumerate(blocks):
                    if (v + 1) * sub - 1 <= ii:
                        beat = jnp.where(rowv > blk, 1.0, 0.0)
                    elif v * sub > ii:
                        beat = jnp.where(rowv >= blk, 1.0, 0.0)
                    else:
                        beat = jnp.where(jsub + v * sub > ii, jnp.where(rowv >= blk, 1.0, 0.0),
                                         jnp.where(rowv > blk, 1.0, 0.0))
                    counts[v] = counts[v] + beat
            for v, cnt in enumerate(counts):
                cnt_scr[v * sub:(v + 1) * sub, :] = cnt
    n_top = min(SLC_TOP_N, n_slc)
    sel_scr[...] = jnp.where(cnt_scr[...] < n_top, 1.0, 0.0)

    acc_scr[...] = jnp.zeros(acc_scr.shape, F32)
    blk_per_tile = tk // ch
    t_b = q0 + lax.broadcasted_iota(jnp.int32, (ch, tq), 1)
    k_b = lax.broadcasted_iota(jnp.int32, (ch, tq), 0)
    n_full = q0 // tk

    def sel_step(kt, m_prev, diagonal):
        k0 = pl.multiple_of(kt * tk, tk)
        keys = ks_ref[pl.ds(k0, tk), :]
        sel_bias = [(sel_scr[pl.ds(kt * blk_per_tile + bb, 1), :] - 1.0) * (-NEG_INF)
                    for bb in range(blk_per_tile)]
        n_pairs = rep // PAIR
        scores = {1: _dot(keys, qt[:, pair_cols(1)])}
        m_out = []
        for pr in range(n_pairs):
            alphas = []
            for h in range(PAIR):
                r = pr * PAIR + h

                def chunk(bb, pr=pr, h=h):
                    rows_bb = slice(bb * ch, (bb + 1) * ch)
                    x = (s0_scr[rows_bb, head_cols(h)] if pr == 0 else scores[pr][rows_bb, head_cols(h)])
                    x = x + sel_bias[bb]
                    if diagonal:
                        x = jnp.where(k_b + (k0 + bb * ch) <= t_b, x, NEG_INF)
                    return x

                m_new = chunked_softmax(chunk, blk_per_tile, m_prev[r], pr % 2, head_cols(h))
                alphas.append(jnp.exp2(m_prev[r] - m_new))
                m_out.append(m_new)
            pv = _dot(vst_ref[:, pl.ds(k0, tk)], p_scr[pr % 2, 0:tk, :])
            for h in range(PAIR):
                cols = head_cols(pr * PAIR + h)
                acc_scr[:, cols] = alphas[h] * acc_scr[:, cols] + pv[:, head_cols(h)]
            if pr + 2 < n_pairs:
                scores[pr + 2] = _dot(keys, qt[:, pair_cols(pr + 2)])
            elif pr + 2 == n_pairs and not diagonal:
                k1 = pl.multiple_of(k0 + tk, tk)
                s0_scr[...] = _dot(ks_ref[pl.ds(k1, tk), :], qt[:, pair_cols(0)])
        return tuple(m_out)

    m_run = lax.fori_loop(0, n_full, lambda kt, m: sel_step(kt, m, False),
                          tuple(jnp.full((1, tq), NEG_INF, F32) for _ in range(rep)))
    sel_step(n_full, m_run, True)

    for r in range(rep):
        cols = slice(r * tq, (r + 1) * tq)
        out_t = base_scr[:, cols] + (gate_row(1, r) / acc_scr[d:d + 1, cols]) * acc_scr[0:d, cols]
        o_ref[:, r * d:(r + 1) * d] = _transpose_bf16(out_t.astype(BF16)).astype(o_ref.dtype)


def _nsa_attend(qt, gt, kc, vct, ks, vst, kw, vwt, ovl_t, batch, seq):
    d = NSA_HEAD_DIM
    g_n = NSA_KV_HEADS
    tq = NSA_TILE
    nq = seq // tq
    rep = qt.shape[3] // seq
    n_chunk = kc.shape[2]
    n_slc = ovl_t.shape[0]
    gate_rows = -(-3 * g_n * rep // SUBLANES) * SUBLANES

    def kv(shape):
        return pl.BlockSpec((None, None) + shape, lambda b, g, i: (b, g, 0, 0))

    return pl.pallas_call(
        _nsa_kernel,
        grid=(batch, g_n, nq),
        in_specs=[
            pl.BlockSpec((None, None, d, rep * tq), lambda b, g, i: (b, g, 0, i)),
            pl.BlockSpec((None, gate_rows, tq), lambda b, g, i: (b, 0, i)),
            kv((n_chunk, d)), kv((d, n_chunk)),
            kv((seq, d)), kv((NSA_V_ROWS, seq)), kv((seq, d)), kv((NSA_V_ROWS, seq)),
            pl.BlockSpec(ovl_t.shape, lambda b, g, i: (0, 0)),
        ],
        out_specs=pl.BlockSpec((tq, rep * d), lambda b, g, i: (b * nq + i, g)),
        out_shape=jax.ShapeDtypeStruct((batch * seq, g_n * rep * d), BF16),
        scratch_shapes=[
            pltpu.VMEM((n_slc, tq), F32),
            pltpu.VMEM((n_slc, tq), F32),
            pltpu.VMEM((n_slc, tq), F32),
            pltpu.VMEM((d, rep * tq), F32),
            pltpu.VMEM((NSA_V_ROWS, rep * tq), F32),
            pltpu.VMEM((SEL_KEY_TILE, PAIR * tq), F32),
            pltpu.VMEM((rep // PAIR, NSA_WINDOW + tq, PAIR * tq), F32),
            pltpu.VMEM((NSA_WINDOW + tq, tq), F32),
            pltpu.VMEM((2, max(NSA_WINDOW + tq, SEL_KEY_TILE), PAIR * tq), BF16),
        ],
        compiler_params=_params("parallel", "parallel", "arbitrary"),
        name="nsa_attention",
    )(qt, gt, kc, vct, ks, vst, kw, vwt, ovl_t)


def _rope_table(pos, head_dim):
    rd = head_dim // ROPE_FRACTION
    half = rd // 2
    inv = 1.0 / (ROPE_THETA ** (jnp.arange(half, dtype=F32) * (2.0 / rd)))
    ang = pos.astype(F32)[:, None] * inv
    cos, sin = jnp.cos(ang), jnp.sin(ang)
    n = pos.shape[0]
    ones = jnp.ones((n, head_dim - rd), F32)
    zeros = jnp.zeros((n, head_dim - rd), F32)
    zh = jnp.zeros((n, half), F32)
    c = jnp.concatenate([cos, cos, ones], axis=1)
    s1 = jnp.concatenate([-sin, zh, zeros], axis=1)
    s2 = jnp.concatenate([zh, sin, zeros], axis=1)
    reps = LANES // head_dim
    return jnp.concatenate([jnp.tile(c, (1, reps)), jnp.tile(s1, (1, reps)), jnp.tile(s2, (1, reps))], axis=1)


def _rope_table_t(positions, head_dim):
    rd = head_dim // ROPE_FRACTION
    half = rd // 2
    inv = 1.0 / (ROPE_THETA ** (jnp.arange(half, dtype=F32) * (2.0 / rd)))
    ang = positions.astype(F32)[:, None, :] * inv[None, :, None]
    return jnp.concatenate([jnp.cos(ang), jnp.sin(ang)], axis=1)


def _overlap_matrix(seq):
    n_cmp_pad = seq // CMP_STRIDE
    n_slc = seq // SLC_BLOCK
    cs = np.arange(n_cmp_pad)[None, :] * CMP_STRIDE
    ss = np.arange(n_slc)[:, None] * SLC_BLOCK
    ovl_t = ((cs < ss + SLC_BLOCK) & (cs + CMP_BLOCK > ss)).astype(np.float32)
    return jnp.asarray(ovl_t, BF16)


def kernel(x, positions, ffn1_norm, ffn1_w_in, ffn1_w_out, mix_norm, ffn2_norm, ffn2_w_in, ffn2_w_out,
           swa_w_in, swa_q_norm, swa_k_norm, swa_sinks, swa_w_out,
           nsa_w_in, nsa_q_norm, nsa_k_norm, nsa_cmp_pe_k, nsa_cmp_w1_k, nsa_cmp_w2_k,
           nsa_cmp_pe_v, nsa_cmp_w1_v, nsa_cmp_w2_v, nsa_w_out):
    batch, seq, d_model = x.shape
    depth = ffn1_norm.shape[0]
    m = batch * seq
    h = x.reshape(m, d_model)

    f1_in, f1_out = ffn1_w_in.astype(BF16), ffn1_w_out.astype(BF16)
    f2_in, f2_out = ffn2_w_in.astype(BF16), ffn2_w_out.astype(BF16)
    swa_in, swa_out = swa_w_in.astype(BF16), swa_w_out.astype(BF16)
    nsa_n = nsa_w_in.shape[2]
    nsa_tn = NSA_PROJ_TILE
    nsa_pad = -nsa_n % nsa_tn
    nsa_in = jnp.pad(nsa_w_in.astype(BF16), ((0, 0), (0, 0), (0, nsa_pad)))
    nsa_out = nsa_w_out.astype(BF16)
    w1_k, w2_k = nsa_cmp_w1_k.astype(BF16), nsa_cmp_w2_k.astype(BF16)
    w1_v, w2_v = nsa_cmp_w1_v.astype(BF16), nsa_cmp_w2_v.astype(BF16)

    n1 = ffn1_norm.reshape(depth, 1, d_model)
    nm = mix_norm.reshape(depth, 1, d_model)
    n2 = ffn2_norm.reshape(depth, 1, d_model)

    pos_flat = positions.reshape(m)
    rope_swa = _rope_table(pos_flat, SWA_HEAD_DIM)
    rope_nsa = _rope_table(pos_flat, NSA_HEAD_DIM)
    rope_nsa_t = _rope_table_t(positions, NSA_HEAD_DIM)
    n_chunk = seq // CMP_STRIDE
    end_idx = jnp.minimum(jnp.arange(n_chunk) * CMP_STRIDE + (CMP_BLOCK - 1), seq - 1)
    rope_cmp = _rope_table(positions[:, end_idx].reshape(batch * n_chunk), NSA_HEAD_DIM)
    ovl_t = _overlap_matrix(seq)

    for i in range(depth):
        h = _ffn(h, n1, f1_in, f1_out, i)
        j = i // N_MIXERS
        if i % N_MIXERS == 0:
            p = _proj(h, nm, swa_in, i, j, SWA_PROJ_TILE)
            qn2 = jnp.tile(swa_q_norm[j], LANES // SWA_HEAD_DIM).reshape(1, LANES)
            kn2 = jnp.tile(swa_k_norm[j], LANES // SWA_HEAD_DIM).reshape(1, LANES)
            o = _swa_attention(p, rope_swa, qn2, kn2, swa_sinks[j], batch, seq)
            h = _oproj(h, o, swa_out, j)
        else:
            p = _proj(h, nm, nsa_in, i, j, nsa_tn)
            kn = nsa_k_norm[j].reshape(1, NSA_HEAD_DIM)
            kc, vct = _nsa_compress(p, rope_cmp, kn, nsa_cmp_pe_k[j], w1_k[j], w2_k[j],
                                    nsa_cmp_pe_v[j], w1_v[j], w2_v[j], batch, seq)
            qn_t = jnp.broadcast_to(nsa_q_norm[j][:, None], (NSA_HEAD_DIM, NSA_TILE))
            qt, ks, vst, kw, vwt = _nsa_qkv(p, rope_nsa, rope_nsa_t, qn_t, kn, batch, seq)
            gt = _nsa_gates(p, batch, seq)
            o = _nsa_attend(qt, gt, kc, vct, ks, vst, kw, vwt, ovl_t, batch, seq)
            h = _oproj(h, o, nsa_out, j)
        h = _ffn(h, n2, f2_in, f2_out, i)
    return h.reshape(batch, seq, d_model)
```

```python
import math

import jax
import jax.numpy as jnp
import numpy as np
from jax import lax
from jax.experimental import pallas as pl
from jax.experimental.pallas import tpu as pltpu

F32 = jnp.float32
BF16 = jnp.bfloat16

N_MIXERS = 2
SWA_HEAD_DIM = 64
SWA_KV_HEADS = 4
SWA_WINDOW = 128
NSA_HEAD_DIM = 128
NSA_KV_HEADS = 4
CMP_BLOCK = 32
CMP_STRIDE = 16
SLC_BLOCK = 64
SLC_TOP_N = 16
NSA_WINDOW = 512
ROPE_THETA = 500000.0
ROPE_FRACTION = 4
NORM_EPS = 1e-6
NEG_INF = -1e30
FORCE_SCORE = 1e9
TINY = 1e-30
LOG2E = math.log2(math.e)

LANES = 128
SUBLANES = 8
VMEM_LIMIT = 56 * 1024 * 1024

ROW_TILE = 512
PROJ_ROW_TILE = 1024
FFN_ROW_TILE = 1024
SWA_PROJ_TILE = 1280
NSA_PROJ_TILE = 1792
FFN_TILE = 512
FFN_SPLIT = 2
SWA_TILE = 128
SWA_BLOCKS = 4
SWA_SKEW = 4
NSA_TILE = 256
NSA_PREP_TILE = 1024
NSA_V_ROWS = NSA_HEAD_DIM + 16
NSA_QSCALE = (NSA_HEAD_DIM ** -0.5) * LOG2E
NSA_GROUPS = 2
PAIR = 2
SEL_KEY_TILE = 512


def _params(*sem):
    return pltpu.CompilerParams(dimension_semantics=sem, vmem_limit_bytes=VMEM_LIMIT)


def _dot(a, b):
    return jnp.dot(a, b, preferred_element_type=F32)


def _dot_nt(a, b):
    return lax.dot_general(a, b, (((1,), (1,)), ((), ())), preferred_element_type=F32)


def _eye(n):
    r = lax.broadcasted_iota(jnp.int32, (n, n), 0)
    c = lax.broadcasted_iota(jnp.int32, (n, n), 1)
    return jnp.where(r == c, 1.0, 0.0).astype(BF16)


def _transpose_bf16(x):
    return _dot_nt(_eye(x.shape[1]), x).astype(BF16)


def _software_pipeline(n_items, stages, skew):
    state = [{} for _ in range(n_items)]
    for t in range(n_items + skew * (len(stages) - 1)):
        for k, stage in enumerate(stages):
            c = t - k * skew
            if 0 <= c < n_items:
                stage(c, state[c])


def _rms(x, g):
    ms = jnp.mean(x * x, axis=-1, keepdims=True)
    return x * lax.rsqrt(ms + NORM_EPS) * g


def _sigmoid(x):
    return 1.0 / (1.0 + jnp.exp(-x))


def _rope(xn, rope, half):
    c, s1, s2 = rope[:, 0:LANES], rope[:, LANES:2 * LANES], rope[:, 2 * LANES:3 * LANES]
    return xn * c + pltpu.roll(xn, LANES - half, 1) * s1 + pltpu.roll(xn, half, 1) * s2


def _ffn_kernel(x_ref, g_ref, wg_ref, wu_ref, wo_ref, o_ref, xn_ref):
    def hidden_update(xn):
        w = wg_ref.shape[1] // FFN_SPLIT
        pre = [(_dot(xn, wg_ref[:, s * w:(s + 1) * w]), _dot(xn, wu_ref[:, s * w:(s + 1) * w]))
               for s in range(FFN_SPLIT)]
        upd = None
        for s, (gate, up) in enumerate(pre):
            hdn = (gate * _sigmoid(gate) * (0.5 * up)).astype(BF16)
            part = _dot(hdn, wo_ref[s * w:(s + 1) * w, :])
            upd = part if upd is None else upd + part
        return upd

    first = pl.program_id(1) == 0

    @pl.when(first)
    def _():
        x = x_ref[...]
        xn = _rms(x, g_ref[...]).astype(BF16)
        xn_ref[...] = xn
        o_ref[...] = x + hidden_update(xn)

    @pl.when(jnp.logical_not(first))
    def _():
        o_ref[...] += hidden_update(xn_ref[...])


def _ffn(h, norm, w_in, w_out, layer):
    m, d = h.shape
    f = w_out.shape[1]
    nf = f // FFN_TILE
    return pl.pallas_call(
        _ffn_kernel,
        grid=(m // FFN_ROW_TILE, nf),
        in_specs=[
            pl.BlockSpec((FFN_ROW_TILE, d), lambda i, j: (i, 0)),
            pl.BlockSpec((None, 1, d), lambda i, j: (layer, 0, 0)),
            pl.BlockSpec((None, d, FFN_TILE), lambda i, j: (layer, 0, j)),
            pl.BlockSpec((None, d, FFN_TILE), lambda i, j: (layer, 0, nf + j)),
            pl.BlockSpec((None, FFN_TILE, d), lambda i, j: (layer, j, 0)),
        ],
        out_specs=pl.BlockSpec((FFN_ROW_TILE, d), lambda i, j: (i, 0)),
        out_shape=jax.ShapeDtypeStruct((m, d), F32),
        scratch_shapes=[pltpu.VMEM((FFN_ROW_TILE, d), BF16)],
        compiler_params=_params("parallel", "arbitrary"),
        name="ffn",
    )(h, norm, w_in, w_in, w_out)


def _proj_kernel(x_ref, g_ref, w_ref, o_ref, xn_ref):
    first = pl.program_id(1) == 0

    @pl.when(first)
    def _():
        xn = _rms(x_ref[...], g_ref[...]).astype(BF16)
        xn_ref[...] = xn
        o_ref[...] = _dot(xn, w_ref[...])

    @pl.when(jnp.logical_not(first))
    def _():
        o_ref[...] = _dot(xn_ref[...], w_ref[...])


def _proj(h, norm, w, layer, wl, tn):
    m, d = h.shape
    n = w.shape[2]
    return pl.pallas_call(
        _proj_kernel,
        grid=(m // PROJ_ROW_TILE, n // tn),
        in_specs=[
            pl.BlockSpec((PROJ_ROW_TILE, d), lambda i, j: (i, 0)),
            pl.BlockSpec((None, 1, d), lambda i, j: (layer, 0, 0)),
            pl.BlockSpec((None, d, tn), lambda i, j: (wl, 0, j)),
        ],
        out_specs=pl.BlockSpec((PROJ_ROW_TILE, tn), lambda i, j: (i, j)),
        out_shape=jax.ShapeDtypeStruct((m, n), F32),
        scratch_shapes=[pltpu.VMEM((PROJ_ROW_TILE, d), BF16)],
        compiler_params=_params("parallel", "arbitrary"),
        name="mixer_in_proj",
    )(h, norm, w)


def _oproj_kernel(h_ref, o_ref, w_ref, out_ref):
    out_ref[...] = h_ref[...] + _dot(o_ref[...], w_ref[...])


def _oproj(h, o, w, wl):
    m, d = h.shape
    k = o.shape[1]
    return pl.pallas_call(
        _oproj_kernel,
        grid=(m // ROW_TILE,),
        in_specs=[
            pl.BlockSpec((ROW_TILE, d), lambda i: (i, 0)),
            pl.BlockSpec((ROW_TILE, k), lambda i: (i, 0)),
            pl.BlockSpec((None, k, d), lambda i: (wl, 0, 0)),
        ],
        out_specs=pl.BlockSpec((ROW_TILE, d), lambda i: (i, 0)),
        out_shape=jax.ShapeDtypeStruct((m, d), F32),
        compiler_params=_params("parallel"),
        name="mixer_out_proj",
    )(h, o, w)


def _swa_kernel(sink_ref, q_ref, kvc_ref, kvp_ref, rc_ref, rp_ref, qn_ref, kn_ref, o_ref):
    i = pl.program_id(1)
    tq = SWA_TILE
    n_blk = q_ref.shape[0] // tq
    n_keys = (n_blk + 1) * tq
    hd = SWA_HEAD_DIM
    half = hd // ROPE_FRACTION // 2
    n_chunks = q_ref.shape[1] // LANES
    heads_per_group = (2 * n_chunks) // SWA_KV_HEADS
    kv_w = SWA_KV_HEADS * hd
    n_kv_chunks = kv_w // LANES
    c1 = (hd ** -0.5) * LOG2E

    row = lax.broadcasted_iota(jnp.int32, (LANES, LANES), 0)
    col = lax.broadcasted_iota(jnp.int32, (LANES, LANES), 1)
    seg = jnp.where((row // hd) == (col // hd), 1.0 / hd, 0.0).astype(BF16)
    eye = _eye(LANES)

    def mean_sq(x):
        xx = x * x
        hi = xx.astype(BF16)
        lo = (xx - hi.astype(F32)).astype(BF16)
        return _dot(hi, seg) + _dot(lo, seg)

    def norm_rope(x, gn, rope):
        return _rope(x * lax.rsqrt(mean_sq(x) + NORM_EPS) * gn, rope, half)

    low = lax.broadcasted_iota(jnp.int32, (n_keys, LANES), 1) < hd
    kn = kn_ref[...]
    k_chunks = []
    for c in range(n_kv_chunks):
        sl = slice(c * LANES, (c + 1) * LANES)
        k_chunks.append(jnp.concatenate([norm_rope(kvp_ref[:, sl], kn, rp_ref[...]),
                                         norm_rope(kvc_ref[:, sl], kn, rc_ref[...])], axis=0))
    v_all = jnp.concatenate([kvp_ref[:, kv_w:2 * kv_w], kvc_ref[:, kv_w:2 * kv_w]], axis=0).astype(BF16)
    vt = _dot_nt(_eye(kv_w), v_all)

    ones_rows = jnp.ones((2 * SUBLANES, n_keys), F32)
    kd, vext = [], []
    for g in range(SWA_KV_HEADS):
        kc_ = k_chunks[g // 2]
        kc_sw = pltpu.roll(kc_, hd, 1)
        kd.append((jnp.where(low, kc_, kc_sw) if g % 2 == 0 else jnp.where(low, kc_sw, kc_)).astype(BF16))
        vext.append(jnp.concatenate([vt[g * hd:(g + 1) * hd, :], ones_rows], axis=0).astype(BF16))

    ki = lax.broadcasted_iota(jnp.int32, (2 * tq, tq), 0)
    qi = lax.broadcasted_iota(jnp.int32, (2 * tq, tq), 1)
    rel = qi + tq - ki
    band = (rel >= 0) & (rel < SWA_WINDOW)
    masks = [band & ((ki >= tq) | (i > 0))] + [band] * (n_blk - 1)
    qtop = lax.broadcasted_iota(jnp.int32, (LANES, tq), 0) < hd
    qn = qn_ref[...]

    def where(it):
        blk, c = divmod(it, n_chunks)
        return blk, c, slice(blk * tq, (blk + 1) * tq), slice(c * LANES, (c + 1) * LANES)

    def st_load(it, st):
        _, _, rows, lanes = where(it)
        st["x"] = q_ref[rows, lanes]
        st["ms"] = mean_sq(st["x"])

    def st_query(it, st):
        _, _, rows, _ = where(it)
        xq = (_rope(st["x"] * lax.rsqrt(st["ms"] + NORM_EPS) * qn, rc_ref[rows, :], half) * c1).astype(BF16)
        st["qt"] = _dot_nt(eye, xq)

    def st_scores(it, st):
        blk, c, _, _ = where(it)
        g = (2 * c) // heads_per_group
        qt = st["qt"]
        q_pair = jnp.concatenate([jnp.where(qtop, qt, 0.0), jnp.where(qtop, 0.0, qt)], axis=1).astype(BF16)
        st["s"] = _dot(kd[g][blk * tq:(blk + 2) * tq, :], q_pair)

    def st_values(it, st):
        blk, c, _, _ = where(it)
        g = (2 * c) // heads_per_group
        es, corr = [], []
        for par in range(2):
            s = jnp.where(masks[blk], st["s"][:, par * tq:(par + 1) * tq], NEG_INF)
            sink = sink_ref[2 * c + par] * LOG2E
            mx = jnp.maximum(jnp.max(s, axis=0, keepdims=True), sink)
            es.append(jnp.exp2(s - mx).astype(BF16))
            corr.append(jnp.exp2(sink - mx))
        st["pv"] = _dot(vext[g][:, blk * tq:(blk + 2) * tq], jnp.concatenate(es, axis=1))
        st["corr"] = corr

    def st_merge(it, st):
        pv = st["pv"]
        parts = []
        for par in range(2):
            cols = slice(par * tq, (par + 1) * tq)
            parts.append(pv[0:hd, cols] * (1.0 / (pv[hd:hd + 1, cols] + st["corr"][par])))
        ot = jnp.concatenate(parts, axis=0).astype(BF16)
        st["o"] = _dot_nt(eye, ot)

    def st_store(it, st):
        _, _, rows, lanes = where(it)
        o_ref[rows, lanes] = st["o"].astype(o_ref.dtype)

    _software_pipeline(n_blk * n_chunks, (st_load, st_query, st_scores, st_values, st_merge, st_store), SWA_SKEW)


def _swa_attention(p, rope, q_norm2, k_norm2, sinks, batch, seq):
    m = p.shape[0]
    tq = SWA_TILE
    ts = SWA_BLOCKS * tq
    ns = seq // ts
    n_q = sinks.shape[0] * SWA_HEAD_DIM
    kv_w = 2 * SWA_KV_HEADS * SWA_HEAD_DIM
    kv_blk = n_q // kv_w

    def prev_blk(b, i):
        return (b * ns + i) * SWA_BLOCKS - jnp.minimum(i, 1)

    return pl.pallas_call(
        _swa_kernel,
        grid=(batch, ns),
        in_specs=[
            pl.BlockSpec(memory_space=pltpu.SMEM),
            pl.BlockSpec((ts, n_q), lambda b, i: (b * ns + i, 0)),
            pl.BlockSpec((ts, kv_w), lambda b, i: (b * ns + i, kv_blk)),
            pl.BlockSpec((tq, kv_w), lambda b, i: (prev_blk(b, i), kv_blk)),
            pl.BlockSpec((ts, 3 * LANES), lambda b, i: (b * ns + i, 0)),
            pl.BlockSpec((tq, 3 * LANES), lambda b, i: (prev_blk(b, i), 0)),
            pl.BlockSpec((1, LANES), lambda b, i: (0, 0)),
            pl.BlockSpec((1, LANES), lambda b, i: (0, 0)),
        ],
        out_specs=pl.BlockSpec((ts, n_q), lambda b, i: (b * ns + i, 0)),
        out_shape=jax.ShapeDtypeStruct((m, n_q), BF16),
        compiler_params=_params("parallel", "arbitrary"),
        name="swa_attention",
    )(sinks, p, p, p, rope, rope, q_norm2, k_norm2)


def _nsa_compress_kernel(kc_ref, vc_ref, ropec_ref, kn_ref, pek_ref, w1k_ref, w2k_ref,
                         pev_ref, w1v_ref, w2v_ref, kco_ref, vco_ref):
    half = NSA_HEAD_DIM // ROPE_FRACTION // 2
    n_chunk = kc_ref.shape[0] // CMP_STRIDE
    per = CMP_BLOCK // CMP_STRIDE

    def compress(t_ref, pe_ref, w1_ref, w2_ref):
        parts = []
        for h in range(per):
            acc = jnp.zeros((n_chunk, w1_ref.shape[2]), F32)
            for l in range(CMP_STRIDE):
                ll = h * CMP_STRIDE + l
                xl = t_ref[pl.ds(l, n_chunk, stride=CMP_STRIDE), :] + pe_ref[ll:ll + 1, :]
                acc = acc + _dot(xl.astype(BF16), w1_ref[ll])
            parts.append(acc)
        pre = parts[0]
        for h in range(1, per):
            pre = pre + pltpu.roll(parts[h], n_chunk - h, 0)
        hdn = pre * _sigmoid(pre)
        return _dot(hdn.astype(BF16), w2_ref[...])

    kcmp = compress(kc_ref, pek_ref, w1k_ref, w2k_ref)
    kco_ref[...] = _rope(_rms(kcmp, kn_ref[...]), ropec_ref[...], half).astype(BF16)
    vco_ref[...] = _transpose_bf16(compress(vc_ref, pev_ref, w1v_ref, w2v_ref).astype(BF16))


def _nsa_compress(p, rope_c, k_norm, pe_k, w1_k, w2_k, pe_v, w1_v, w2_v, batch, seq):
    d = NSA_HEAD_DIM
    g_n = NSA_KV_HEADS
    n_chunk = seq // CMP_STRIDE
    q_blocks = 4 * g_n

    def col(kind):
        return pl.BlockSpec((seq, d), lambda b, g, kind=kind: (b, q_blocks + kind * g_n + g))

    def full(a):
        return pl.BlockSpec(a.shape, lambda b, g, nd=a.ndim: (0,) * nd)

    return pl.pallas_call(
        _nsa_compress_kernel,
        grid=(batch, g_n),
        in_specs=[col(0), col(1),
                  pl.BlockSpec((n_chunk, 3 * LANES), lambda b, g: (b, 0)),
                  full(k_norm), full(pe_k), full(w1_k), full(w2_k), full(pe_v), full(w1_v), full(w2_v)],
        out_specs=[pl.BlockSpec((None, None, n_chunk, d), lambda b, g: (b, g, 0, 0)),
                   pl.BlockSpec((None, None, d, n_chunk), lambda b, g: (b, g, 0, 0))],
        out_shape=[jax.ShapeDtypeStruct((batch, g_n, n_chunk, d), BF16),
                   jax.ShapeDtypeStruct((batch, g_n, d, n_chunk), BF16)],
        compiler_params=_params("parallel", "arbitrary"),
        name="nsa_compress",
    )(p, p, rope_c, k_norm, pe_k, w1_k, w2_k, pe_v, w1_v, w2_v)


def _nsa_qkv_kernel(q_ref, ks_ref, vs_ref, kw_ref, vw_ref, rope_ref, ropet_ref, qnt_ref, kn_ref,
                    qt_ref, kso_ref, vst_ref, kwo_ref, vwt_ref):
    d = NSA_HEAD_DIM
    half = d // ROPE_FRACTION // 2
    rep = q_ref.shape[1] // d
    tq = NSA_TILE
    eye = _eye(d)
    for s in range(q_ref.shape[0] // tq):
        rows = slice(s * tq, (s + 1) * tq)
        cos = ropet_ref[0:half, rows]
        sin = ropet_ref[half:2 * half, rows]
        for r in range(rep):
            x = q_ref[rows, r * d:(r + 1) * d]
            hi = x.astype(BF16)
            rem = x - hi.astype(F32)
            mid = rem.astype(BF16)
            lo = (rem - mid.astype(F32)).astype(BF16)
            xt = _dot_nt(eye, hi) + _dot_nt(eye, mid) + _dot_nt(eye, lo)
            xn = xt * lax.rsqrt(jnp.mean(xt * xt, axis=0, keepdims=True) + NORM_EPS) * qnt_ref[...]
            x1, x2 = xn[0:half, :], xn[half:2 * half, :]
            rot = jnp.concatenate([x1 * cos - x2 * sin, x2 * cos + x1 * sin, xn[2 * half:, :]], axis=0)
            qt_ref[:, (s * rep + r) * tq:(s * rep + r + 1) * tq] = (rot * NSA_QSCALE).astype(BF16)
    kso_ref[...] = _rope(_rms(ks_ref[...], kn_ref[...]), rope_ref[...], half).astype(BF16)
    kwo_ref[...] = _rope(_rms(kw_ref[...], kn_ref[...]), rope_ref[...], half).astype(BF16)
    ones = jnp.ones((NSA_V_ROWS - d, vs_ref.shape[0]), BF16)
    vst_ref[0:d, :] = _dot_nt(eye, vs_ref[...].astype(BF16)).astype(BF16)
    vst_ref[d:NSA_V_ROWS, :] = ones
    vwt_ref[0:d, :] = _dot_nt(eye, vw_ref[...].astype(BF16)).astype(BF16)
    vwt_ref[d:NSA_V_ROWS, :] = ones


def _nsa_qkv(p, rope, rope_t, q_norm_t, k_norm, batch, seq):
    d = NSA_HEAD_DIM
    g_n = NSA_KV_HEADS
    rep = 4
    tt = NSA_PREP_TILE
    nt = seq // tt
    q_blocks = rep * g_n

    def col(kind):
        return pl.BlockSpec((tt, d), lambda b, g, t, kind=kind: (b * nt + t, q_blocks + kind * g_n + g))

    nat = pl.BlockSpec((None, None, tt, d), lambda b, g, t: (b, g, t, 0))
    tra = pl.BlockSpec((None, None, NSA_V_ROWS, tt), lambda b, g, t: (b, g, 0, t))
    sh_nat = jax.ShapeDtypeStruct((batch, g_n, seq, d), BF16)
    sh_tra = jax.ShapeDtypeStruct((batch, g_n, NSA_V_ROWS, seq), BF16)
    return pl.pallas_call(
        _nsa_qkv_kernel,
        grid=(batch, g_n, nt),
        in_specs=[pl.BlockSpec((tt, rep * d), lambda b, g, t: (b * nt + t, g)),
                  col(2), col(3), col(4), col(5),
                  pl.BlockSpec((tt, 3 * LANES), lambda b, g, t: (b * nt + t, 0)),
                  pl.BlockSpec((None, rope_t.shape[1], tt), lambda b, g, t: (b, 0, t)),
                  pl.BlockSpec(q_norm_t.shape, lambda b, g, t: (0, 0)),
                  pl.BlockSpec((1, d), lambda b, g, t: (0, 0))],
        out_specs=[pl.BlockSpec((None, None, d, rep * tt), lambda b, g, t: (b, g, 0, t)), nat, tra, nat, tra],
        out_shape=[jax.ShapeDtypeStruct((batch, g_n, d, rep * seq), BF16), sh_nat, sh_tra, sh_nat, sh_tra],
        compiler_params=_params("parallel", "parallel", "arbitrary"),
        name="nsa_qkv_prep",
    )(p, p, p, p, p, rope, rope_t, q_norm_t, k_norm)


def _nsa_gate_kernel(g_ref, o_ref):
    o_ref[...] = _sigmoid(g_ref[...]).T


def _nsa_gates(p, batch, seq):
    tt = NSA_PREP_TILE
    nt = seq // tt
    gate_blk = 10 * NSA_KV_HEADS
    return pl.pallas_call(
        _nsa_gate_kernel,
        grid=(batch, nt),
        in_specs=[pl.BlockSpec((tt, LANES), lambda b, t: (b * nt + t, gate_blk))],
        out_specs=pl.BlockSpec((None, LANES, tt), lambda b, t: (b, 0, t)),
        out_shape=jax.ShapeDtypeStruct((batch, LANES, seq), F32),
        compiler_params=_params("parallel", "arbitrary"),
        name="nsa_gates",
    )(p)


def _interleave(gens):
    results = [None] * len(gens)
    live = list(range(len(gens)))
    while live:
        for k in list(live):
            try:
                next(gens[k])
            except StopIteration as stop:
                results[k] = stop.value
                live.remove(k)
    return results


def _nsa_kernel(qt_ref, gt_ref, kc_ref, vct_ref, ks_ref, vst_ref, kw_ref, vwt_ref, ovl_ref, o_ref,
                sel_scr, score_scr, cnt_scr, base_scr, acc_scr, s0_scr, sw_scr, bias_scr, p_scr):
    gp = pl.program_id(1)
    i = pl.program_id(2)
    n_grp_step = qt_ref.shape[0]
    groups = range(n_grp_step)
    d = NSA_HEAD_DIM
    rows = qt_ref.shape[2]
    tq = o_ref.shape[0]
    rep = rows // tq
    n_slc = ovl_ref.shape[0]
    n_heads = NSA_KV_HEADS * rep
    n_pairs = rep // PAIR
    tk = SEL_KEY_TILE
    ch = SLC_BLOCK
    q0 = i * tq
    qts = [qt_ref[gg] for gg in groups]

    def head_cols(r):
        return slice(r * tq, (r + 1) * tq)

    def pair_cols(pr):
        return slice(pr * PAIR * tq, (pr + 1) * PAIR * tq)

    def gate_row(gg, branch, r):
        g = gp * n_grp_step + gg
        return gt_ref[pl.ds(branch * n_heads + g * rep + r, 1), :]

    def fold_max(x):
        parts = [x[j * SUBLANES:(j + 1) * SUBLANES, :] for j in range(x.shape[0] // SUBLANES)]
        while len(parts) > 1:
            parts = [jnp.maximum(parts[j], parts[j + 1]) for j in range(0, len(parts), 2)]
        return parts[0]

    def chunked_softmax(chunk, n_chunks, m_floor, gg, p_slot, p_cols):
        m8 = fold_max(chunk(0))
        for c in range(1, n_chunks):
            m8 = jnp.maximum(m8, fold_max(chunk(c)))
        m = jnp.max(m8, axis=0, keepdims=True)
        if m_floor is not None:
            m = jnp.maximum(m, m_floor)
        for c in range(n_chunks):
            p_scr[gg, p_slot, c * ch:(c + 1) * ch, p_cols] = jnp.exp2(chunk(c) - m).astype(BF16)
        return m

    wk = NSA_WINDOW + tq
    start = pl.multiple_of(jnp.maximum(q0 - NSA_WINDOW, 0), tq)

    @pl.when(i <= NSA_WINDOW // tq)
    def _():
        for c in range(wk // ch):
            t_w = q0 + lax.broadcasted_iota(jnp.int32, (ch, tq), 1)
            k_w = start + c * ch + lax.broadcasted_iota(jnp.int32, (ch, tq), 0)
            ok = (k_w <= t_w) & (t_w - k_w < NSA_WINDOW)
            bias_scr[c * ch:(c + 1) * ch, :] = jnp.where(ok, 0.0, NEG_INF)

    n_cmp = kc_ref.shape[1]
    t_c = q0 + lax.broadcasted_iota(jnp.int32, (n_cmp, tq), 1)
    end_c = lax.broadcasted_iota(jnp.int32, (n_cmp, tq), 0) * CMP_STRIDE + (CMP_BLOCK - 1)
    mask_c = end_c <= t_c

    def front(gg):
        qt = qts[gg]
        sw_scr[gg, 0] = _dot(kw_ref[gg, pl.ds(start, wk), :], qt[:, pair_cols(0)])
        yield
        sc_raw = _dot(kc_ref[gg], qt)
        for pr in range(1, n_pairs):
            sw_scr[gg, pr] = _dot(kw_ref[gg, pl.ds(start, wk), :], qt[:, pair_cols(pr)])
        yield
        s0_scr[gg] = _dot(ks_ref[gg, 0:tk, :], qt[:, pair_cols(0)])
        yield
        imp = jnp.zeros((n_slc, tq), F32)
        for pr in range(n_pairs):
            pcs = []
            for h in range(PAIR):
                cols = head_cols(pr * PAIR + h)
                chunked_softmax(lambda c, pr=pr, h=h: (sw_scr[gg, pr, c * ch:(c + 1) * ch, head_cols(h)]
                                                       + bias_scr[c * ch:(c + 1) * ch, :]),
                                wk // ch, None, gg, pr % 2, head_cols(h))
                sc = jnp.where(mask_c, sc_raw[:, cols], NEG_INF)
                ec = jnp.where(mask_c, jnp.exp2(sc - jnp.max(sc, axis=0, keepdims=True)), 0.0)
                pcs.append((ec / jnp.maximum(jnp.sum(ec, axis=0, keepdims=True), TINY)).astype(BF16))
            pc = jnp.concatenate(pcs, axis=1)
            o_win = _dot(vwt_ref[gg, :, pl.ds(start, wk)], p_scr[gg, pr % 2, 0:wk, :])
            o_cmp = _dot(vct_ref[gg], pc)
            imp_pair = _dot(ovl_ref[...], pc)
            yield
            for h in range(PAIR):
                r = pr * PAIR + h
                hc = head_cols(h)
                imp = imp + imp_pair[:, hc]
                base_scr[gg, :, head_cols(r)] = (gate_row(gg, 0, r) * o_cmp[:, hc]
                                                 + (gate_row(gg, 2, r) / o_win[d:d + 1, hc]) * o_win[0:d, hc])
        return imp

    imps = _interleave([front(gg) for gg in groups])

    jb = lax.broadcasted_iota(jnp.int32, (n_slc, tq), 0)
    tt = q0 + lax.broadcasted_iota(jnp.int32, (n_slc, tq), 1)
    cur = tt // SLC_BLOCK
    forced = (jb == 0) | (jb == cur) | (jb == cur - 1)
    for gg in groups:
        score_scr[gg] = jnp.where(forced, FORCE_SCORE, jnp.where(jb * SLC_BLOCK <= tt, imps[gg], NEG_INF))
    sub = SUBLANES
    n_grp = n_slc // sub
    cnt_scr[...] = jnp.zeros(cnt_scr.shape, F32)
    jsub = lax.broadcasted_iota(jnp.int32, (sub, tq), 0)
    for gi in range(n_grp):
        @pl.when(gi * sub * SLC_BLOCK <= q0 + tq - 1)
        def _(gi=gi):
            for gg in groups:
                blocks = [score_scr[gg, v * sub:(v + 1) * sub, :] for v in range(n_grp)]
                counts = [cnt_scr[gg, v * sub:(v + 1) * sub, :] for v in range(n_grp)]
                for ii in range(gi * sub, (gi + 1) * sub):
                    rowv = score_scr[gg, ii:ii + 1, :]
                    for v, blk in enumerate(blocks):
                        if (v + 1) * sub - 1 <= ii:
                            beat = jnp.where(rowv > blk, 1.0, 0.0)
                        elif v * sub > ii:
                            beat = jnp.where(rowv >= blk, 1.0, 0.0)
                        else:
                            beat = jnp.where(jsub + v * sub > ii, jnp.where(rowv >= blk, 1.0, 0.0),
                                             jnp.where(rowv > blk, 1.0, 0.0))
                        counts[v] = counts[v] + beat
                for v, cnt in enumerate(counts):
                    cnt_scr[gg, v * sub:(v + 1) * sub, :] = cnt
    n_top = min(SLC_TOP_N, n_slc)
    sel_scr[...] = jnp.where(cnt_scr[...] < n_top, 1.0, 0.0)

    acc_scr[...] = jnp.zeros(acc_scr.shape, F32)
    blk_per_tile = tk // ch
    t_b = q0 + lax.broadcasted_iota(jnp.int32, (ch, tq), 1)
    k_b = lax.broadcasted_iota(jnp.int32, (ch, tq), 0)
    n_full = q0 // tk

    def sel_step(gg, kt, m_prev, diagonal):
        qt = qts[gg]
        k0 = pl.multiple_of(kt * tk, tk)
        keys = ks_ref[gg, pl.ds(k0, tk), :]
        sel_bias = [(sel_scr[gg, pl.ds(kt * blk_per_tile + bb, 1), :] - 1.0) * (-NEG_INF)
                    for bb in range(blk_per_tile)]
        scores = {1: _dot(keys, qt[:, pair_cols(1)])}
        yield
        m_out = []
        for pr in range(n_pairs):
            alphas = []
            for h in range(PAIR):
                r = pr * PAIR + h

                def chunk(bb, pr=pr, h=h):
                    rows_bb = slice(bb * ch, (bb + 1) * ch)
                    x = (s0_scr[gg, rows_bb, head_cols(h)] if pr == 0 else scores[pr][rows_bb, head_cols(h)])
                    x = x + sel_bias[bb]
                    if diagonal:
                        x = jnp.where(k_b + (k0 + bb * ch) <= t_b, x, NEG_INF)
                    return x

                m_new = chunked_softmax(chunk, blk_per_tile, m_prev[r], gg, pr % 2, head_cols(h))
                alphas.append(jnp.exp2(m_prev[r] - m_new))
                m_out.append(m_new)
            pv = _dot(vst_ref[gg, :, pl.ds(k0, tk)], p_scr[gg, pr % 2, 0:tk, :])
            yield
            for h in range(PAIR):
                cols = head_cols(pr * PAIR + h)
                acc_scr[gg, :, cols] = alphas[h] * acc_scr[gg, :, cols] + pv[:, head_cols(h)]
            if pr + 2 < n_pairs:
                scores[pr + 2] = _dot(keys, qt[:, pair_cols(pr + 2)])
                yield
            elif pr + 2 == n_pairs and not diagonal:
                k1 = pl.multiple_of(k0 + tk, tk)
                s0_scr[gg] = _dot(ks_ref[gg, pl.ds(k1, tk), :], qt[:, pair_cols(0)])
                yield
        return tuple(m_out)

    def both_groups(kt, m_all, diagonal):
        return tuple(_interleave([sel_step(gg, kt, m_all[gg], diagonal) for gg in groups]))

    m_init = tuple(tuple(jnp.full((1, tq), NEG_INF, F32) for _ in range(rep)) for _ in groups)
    m_run = lax.fori_loop(0, n_full, lambda kt, m: both_groups(kt, m, False), m_init)
    both_groups(n_full, m_run, True)

    for gg in groups:
        for r in range(rep):
            cols = head_cols(r)
            out_t = (base_scr[gg, :, cols]
                     + (gate_row(gg, 1, r) / acc_scr[gg, d:d + 1, cols]) * acc_scr[gg, 0:d, cols])
            o_ref[:, (gg * rep + r) * d:(gg * rep + r + 1) * d] = (
                _transpose_bf16(out_t.astype(BF16)).astype(o_ref.dtype))


def _nsa_attend(qt, gt, kc, vct, ks, vst, kw, vwt, ovl_t, batch, seq):
    d = NSA_HEAD_DIM
    g_n = NSA_KV_HEADS
    gs = NSA_GROUPS
    tq = NSA_TILE
    nq = seq // tq
    rep = qt.shape[3] // seq
    n_chunk = kc.shape[2]
    n_slc = ovl_t.shape[0]
    gate_rows = -(-3 * g_n * rep // SUBLANES) * SUBLANES
    wk = NSA_WINDOW + tq

    def kv(shape):
        return pl.BlockSpec((None, gs) + shape, lambda b, g, i: (b, g, 0, 0))

    return pl.pallas_call(
        _nsa_kernel,
        grid=(batch, g_n // gs, nq),
        in_specs=[
            pl.BlockSpec((None, gs, d, rep * tq), lambda b, g, i: (b, g, 0, i)),
            pl.BlockSpec((None, gate_rows, tq), lambda b, g, i: (b, 0, i)),
            kv((n_chunk, d)), kv((d, n_chunk)),
            kv((seq, d)), kv((NSA_V_ROWS, seq)), kv((seq, d)), kv((NSA_V_ROWS, seq)),
            pl.BlockSpec(ovl_t.shape, lambda b, g, i: (0, 0)),
        ],
        out_specs=pl.BlockSpec((tq, gs * rep * d), lambda b, g, i: (b * nq + i, g)),
        out_shape=jax.ShapeDtypeStruct((batch * seq, g_n * rep * d), BF16),
        scratch_shapes=[
            pltpu.VMEM((gs, n_slc, tq), F32),
            pltpu.VMEM((gs, n_slc, tq), F32),
            pltpu.VMEM((gs, n_slc, tq), F32),
            pltpu.VMEM((gs, d, rep * tq), F32),
            pltpu.VMEM((gs, NSA_V_ROWS, rep * tq), F32),
            pltpu.VMEM((gs, SEL_KEY_TILE, PAIR * tq), F32),
            pltpu.VMEM((gs, rep // PAIR, wk, PAIR * tq), F32),
            pltpu.VMEM((wk, tq), F32),
            pltpu.VMEM((gs, 2, max(wk, SEL_KEY_TILE), PAIR * tq), BF16),
        ],
        compiler_params=_params("parallel", "parallel", "arbitrary"),
        name="nsa_attention",
    )(qt, gt, kc, vct, ks, vst, kw, vwt, ovl_t)


def _rope_table(pos, head_dim):
    rd = head_dim // ROPE_FRACTION
    half = rd // 2
    inv = 1.0 / (ROPE_THETA ** (jnp.arange(half, dtype=F32) * (2.0 / rd)))
    ang = pos.astype(F32)[:, None] * inv
    cos, sin = jnp.cos(ang), jnp.sin(ang)
    n = pos.shape[0]
    ones = jnp.ones((n, head_dim - rd), F32)
    zeros = jnp.zeros((n, head_dim - rd), F32)
    zh = jnp.zeros((n, half), F32)
    c = jnp.concatenate([cos, cos, ones], axis=1)
    s1 = jnp.concatenate([-sin, zh, zeros], axis=1)
    s2 = jnp.concatenate([zh, sin, zeros], axis=1)
    reps = LANES // head_dim
    return jnp.concatenate([jnp.tile(c, (1, reps)), jnp.tile(s1, (1, reps)), jnp.tile(s2, (1, reps))], axis=1)


def _rope_table_t(positions, head_dim):
    rd = head_dim // ROPE_FRACTION
    half = rd // 2
    inv = 1.0 / (ROPE_THETA ** (jnp.arange(half, dtype=F32) * (2.0 / rd)))
    ang = positions.astype(F32)[:, None, :] * inv[None, :, None]
    return jnp.concatenate([jnp.cos(ang), jnp.sin(ang)], axis=1)


def _overlap_matrix(seq):
    n_cmp_pad = seq // CMP_STRIDE
    n_slc = seq // SLC_BLOCK
    cs = np.arange(n_cmp_pad)[None, :] * CMP_STRIDE
    ss = np.arange(n_slc)[:, None] * SLC_BLOCK
    ovl_t = ((cs < ss + SLC_BLOCK) & (cs + CMP_BLOCK > ss)).astype(np.float32)
    return jnp.asarray(ovl_t, BF16)


def kernel(x, positions, ffn1_norm, ffn1_w_in, ffn1_w_out, mix_norm, ffn2_norm, ffn2_w_in, ffn2_w_out,
           swa_w_in, swa_q_norm, swa_k_norm, swa_sinks, swa_w_out,
           nsa_w_in, nsa_q_norm, nsa_k_norm, nsa_cmp_pe_k, nsa_cmp_w1_k, nsa_cmp_w2_k,
           nsa_cmp_pe_v, nsa_cmp_w1_v, nsa_cmp_w2_v, nsa_w_out):
    batch, seq, d_model = x.shape
    depth = ffn1_norm.shape[0]
    m = batch * seq
    h = x.reshape(m, d_model)

    f1_in, f1_out = ffn1_w_in.astype(BF16), ffn1_w_out.astype(BF16)
    f2_in, f2_out = ffn2_w_in.astype(BF16), ffn2_w_out.astype(BF16)
    swa_in, swa_out = swa_w_in.astype(BF16), swa_w_out.astype(BF16)
    nsa_n = nsa_w_in.shape[2]
    nsa_tn = NSA_PROJ_TILE
    nsa_pad = -nsa_n % nsa_tn
    nsa_in = jnp.pad(nsa_w_in.astype(BF16), ((0, 0), (0, 0), (0, nsa_pad)))
    nsa_out = nsa_w_out.astype(BF16)
    w1_k, w2_k = nsa_cmp_w1_k.astype(BF16), nsa_cmp_w2_k.astype(BF16)
    w1_v, w2_v = nsa_cmp_w1_v.astype(BF16), nsa_cmp_w2_v.astype(BF16)

    n1 = ffn1_norm.reshape(depth, 1, d_model)
    nm = mix_norm.reshape(depth, 1, d_model)
    n2 = ffn2_norm.reshape(depth, 1, d_model)

    pos_flat = positions.reshape(m)
    rope_swa = _rope_table(pos_flat, SWA_HEAD_DIM)
    rope_nsa = _rope_table(pos_flat, NSA_HEAD_DIM)
    rope_nsa_t = _rope_table_t(positions, NSA_HEAD_DIM)
    n_chunk = seq // CMP_STRIDE
    end_idx = jnp.minimum(jnp.arange(n_chunk) * CMP_STRIDE + (CMP_BLOCK - 1), seq - 1)
    rope_cmp = _rope_table(positions[:, end_idx].reshape(batch * n_chunk), NSA_HEAD_DIM)
    ovl_t = _overlap_matrix(seq)

    for i in range(depth):
        h = _ffn(h, n1, f1_in, f1_out, i)
        j = i // N_MIXERS
        if i % N_MIXERS == 0:
            p = _proj(h, nm, swa_in, i, j, SWA_PROJ_TILE)
            qn2 = jnp.tile(swa_q_norm[j], LANES // SWA_HEAD_DIM).reshape(1, LANES)
            kn2 = jnp.tile(swa_k_norm[j], LANES // SWA_HEAD_DIM).reshape(1, LANES)
            o = _swa_attention(p, rope_swa, qn2, kn2, swa_sinks[j], batch, seq)
            h = _oproj(h, o, swa_out, j)
        else:
            p = _proj(h, nm, nsa_in, i, j, nsa_tn)
            kn = nsa_k_norm[j].reshape(1, NSA_HEAD_DIM)
            kc, vct = _nsa_compress(p, rope_cmp, kn, nsa_cmp_pe_k[j], w1_k[j], w2_k[j],
                                    nsa_cmp_pe_v[j], w1_v[j], w2_v[j], batch, seq)
            qn_t = jnp.broadcast_to(nsa_q_norm[j][:, None], (NSA_HEAD_DIM, NSA_TILE))
            qt, ks, vst, kw, vwt = _nsa_qkv(p, rope_nsa, rope_nsa_t, qn_t, kn, batch, seq)
            gt = _nsa_gates(p, batch, seq)
            o = _nsa_attend(qt, gt, kc, vct, ks, vst, kw, vwt, ovl_t, batch, seq)
            h = _oproj(h, o, nsa_out, j)
        h = _ffn(h, n2, f2_in, f2_out, i)
    return h.reshape(batch, seq, d_model)
```

```python
import math

import jax
import jax.numpy as jnp
import numpy as np
from jax import lax
from jax.experimental import pallas as pl
from jax.experimental.pallas import tpu as pltpu

F32 = jnp.float32
BF16 = jnp.bfloat16

N_MIXERS = 2
SWA_HEAD_DIM = 64
SWA_KV_HEADS = 4
SWA_WINDOW = 128
NSA_HEAD_DIM = 128
NSA_KV_HEADS = 4
CMP_BLOCK = 32
CMP_STRIDE = 16
SLC_BLOCK = 64
SLC_TOP_N = 16
NSA_WINDOW = 512
ROPE_THETA = 500000.0
ROPE_FRACTION = 4
NORM_EPS = 1e-6
NEG_INF = -1e30
FORCE_SCORE = 1e9
TINY = 1e-30
LOG2E = math.log2(math.e)

LANES = 128
SUBLANES = 8
VMEM_LIMIT = 56 * 1024 * 1024

ROW_TILE = 512
PROJ_ROW_TILE = 1024
FFN_ROW_TILE = 1024
SWA_PROJ_TILE = 1280
NSA_PROJ_TILE = 1792
FFN_TILE = 512
FFN_SPLIT = 2
SWA_TILE = 128
SWA_BLOCKS = 4
SWA_SKEW = 4
NSA_TILE = 256
NSA_PREP_TILE = 1024
NSA_V_ROWS = NSA_HEAD_DIM + 16
NSA_QSCALE = (NSA_HEAD_DIM ** -0.5) * LOG2E
NSA_GROUPS = 2
PAIR = 2
SEL_KEY_TILE = 512


def _params(*sem):
    return pltpu.CompilerParams(dimension_semantics=sem, vmem_limit_bytes=VMEM_LIMIT)


def _dot(a, b):
    return jnp.dot(a, b, preferred_element_type=F32)


def _dot_nt(a, b):
    return lax.dot_general(a, b, (((1,), (1,)), ((), ())), preferred_element_type=F32)


def _eye(n):
    r = lax.broadcasted_iota(jnp.int32, (n, n), 0)
    c = lax.broadcasted_iota(jnp.int32, (n, n), 1)
    return jnp.where(r == c, 1.0, 0.0).astype(BF16)


def _transpose_bf16(x):
    return _dot_nt(_eye(x.shape[1]), x).astype(BF16)


def _software_pipeline(n_items, stages, skew):
    state = [{} for _ in range(n_items)]
    for t in range(n_items + skew * (len(stages) - 1)):
        for k, stage in enumerate(stages):
            c = t - k * skew
            if 0 <= c < n_items:
                stage(c, state[c])


def _rms(x, g):
    ms = jnp.mean(x * x, axis=-1, keepdims=True)
    return x * lax.rsqrt(ms + NORM_EPS) * g


def _sigmoid(x):
    return 1.0 / (1.0 + jnp.exp(-x))


def _rope(xn, rope, half):
    c, s1, s2 = rope[:, 0:LANES], rope[:, LANES:2 * LANES], rope[:, 2 * LANES:3 * LANES]
    return xn * c + pltpu.roll(xn, LANES - half, 1) * s1 + pltpu.roll(xn, half, 1) * s2


def _ffn_kernel(x_ref, g_ref, wg_ref, wu_ref, wo_ref, o_ref, xn_ref):
    def hidden_update(xn):
        w = wg_ref.shape[1] // FFN_SPLIT
        pre = [(_dot(xn, wg_ref[:, s * w:(s + 1) * w]), _dot(xn, wu_ref[:, s * w:(s + 1) * w]))
               for s in range(FFN_SPLIT)]
        upd = None
        for s, (gate, up) in enumerate(pre):
            hdn = (gate * _sigmoid(gate) * (0.5 * up)).astype(BF16)
            part = _dot(hdn, wo_ref[s * w:(s + 1) * w, :])
            upd = part if upd is None else upd + part
        return upd

    first = pl.program_id(1) == 0

    @pl.when(first)
    def _():
        x = x_ref[...]
        xn = _rms(x, g_ref[...]).astype(BF16)
        xn_ref[...] = xn
        o_ref[...] = x + hidden_update(xn)

    @pl.when(jnp.logical_not(first))
    def _():
        o_ref[...] += hidden_update(xn_ref[...])


def _ffn(h, norm, w_in, w_out, layer):
    m, d = h.shape
    f = w_out.shape[1]
    nf = f // FFN_TILE
    return pl.pallas_call(
        _ffn_kernel,
        grid=(m // FFN_ROW_TILE, nf),
        in_specs=[
            pl.BlockSpec((FFN_ROW_TILE, d), lambda i, j: (i, 0)),
            pl.BlockSpec((None, 1, d), lambda i, j: (layer, 0, 0)),
            pl.BlockSpec((None, d, FFN_TILE), lambda i, j: (layer, 0, j)),
            pl.BlockSpec((None, d, FFN_TILE), lambda i, j: (layer, 0, nf + j)),
            pl.BlockSpec((None, FFN_TILE, d), lambda i, j: (layer, j, 0)),
        ],
        out_specs=pl.BlockSpec((FFN_ROW_TILE, d), lambda i, j: (i, 0)),
        out_shape=jax.ShapeDtypeStruct((m, d), F32),
        scratch_shapes=[pltpu.VMEM((FFN_ROW_TILE, d), BF16)],
        compiler_params=_params("parallel", "arbitrary"),
        name="ffn",
    )(h, norm, w_in, w_in, w_out)


def _proj_kernel(x_ref, g_ref, w_ref, o_ref, xn_ref):
    first = pl.program_id(1) == 0

    @pl.when(first)
    def _():
        xn = _rms(x_ref[...], g_ref[...]).astype(BF16)
        xn_ref[...] = xn
        o_ref[...] = _dot(xn, w_ref[...])

    @pl.when(jnp.logical_not(first))
    def _():
        o_ref[...] = _dot(xn_ref[...], w_ref[...])


def _proj(h, norm, w, layer, wl, tn):
    m, d = h.shape
    n = w.shape[2]
    return pl.pallas_call(
        _proj_kernel,
        grid=(m // PROJ_ROW_TILE, n // tn),
        in_specs=[
            pl.BlockSpec((PROJ_ROW_TILE, d), lambda i, j: (i, 0)),
            pl.BlockSpec((None, 1, d), lambda i, j: (layer, 0, 0)),
            pl.BlockSpec((None, d, tn), lambda i, j: (wl, 0, j)),
        ],
        out_specs=pl.BlockSpec((PROJ_ROW_TILE, tn), lambda i, j: (i, j)),
        out_shape=jax.ShapeDtypeStruct((m, n), F32),
        scratch_shapes=[pltpu.VMEM((PROJ_ROW_TILE, d), BF16)],
        compiler_params=_params("parallel", "arbitrary"),
        name="mixer_in_proj",
    )(h, norm, w)


def _oproj_kernel(h_ref, o_ref, w_ref, out_ref):
    out_ref[...] = h_ref[...] + _dot(o_ref[...], w_ref[...])


def _oproj(h, o, w, wl):
    m, d = h.shape
    k = o.shape[1]
    return pl.pallas_call(
        _oproj_kernel,
        grid=(m // ROW_TILE,),
        in_specs=[
            pl.BlockSpec((ROW_TILE, d), lambda i: (i, 0)),
            pl.BlockSpec((ROW_TILE, k), lambda i: (i, 0)),
            pl.BlockSpec((None, k, d), lambda i: (wl, 0, 0)),
        ],
        out_specs=pl.BlockSpec((ROW_TILE, d), lambda i: (i, 0)),
        out_shape=jax.ShapeDtypeStruct((m, d), F32),
        compiler_params=_params("parallel"),
        name="mixer_out_proj",
    )(h, o, w)


def _swa_kernel(sink_ref, q_ref, kvc_ref, kvp_ref, rc_ref, rp_ref, qn_ref, kn_ref, o_ref):
    i = pl.program_id(1)
    tq = SWA_TILE
    n_blk = q_ref.shape[0] // tq
    n_keys = (n_blk + 1) * tq
    hd = SWA_HEAD_DIM
    half = hd // ROPE_FRACTION // 2
    n_chunks = q_ref.shape[1] // LANES
    heads_per_group = (2 * n_chunks) // SWA_KV_HEADS
    kv_w = SWA_KV_HEADS * hd
    n_kv_chunks = kv_w // LANES
    c1 = (hd ** -0.5) * LOG2E

    row = lax.broadcasted_iota(jnp.int32, (LANES, LANES), 0)
    col = lax.broadcasted_iota(jnp.int32, (LANES, LANES), 1)
    seg = jnp.where((row // hd) == (col // hd), 1.0 / hd, 0.0).astype(BF16)
    eye = _eye(LANES)

    def mean_sq(x):
        xx = x * x
        hi = xx.astype(BF16)
        lo = (xx - hi.astype(F32)).astype(BF16)
        return _dot(hi, seg) + _dot(lo, seg)

    def norm_rope(x, gn, rope):
        return _rope(x * lax.rsqrt(mean_sq(x) + NORM_EPS) * gn, rope, half)

    low = lax.broadcasted_iota(jnp.int32, (n_keys, LANES), 1) < hd
    kn = kn_ref[...]
    k_chunks = []
    for c in range(n_kv_chunks):
        sl = slice(c * LANES, (c + 1) * LANES)
        k_chunks.append(jnp.concatenate([norm_rope(kvp_ref[:, sl], kn, rp_ref[...]),
                                         norm_rope(kvc_ref[:, sl], kn, rc_ref[...])], axis=0))
    v_all = jnp.concatenate([kvp_ref[:, kv_w:2 * kv_w], kvc_ref[:, kv_w:2 * kv_w]], axis=0).astype(BF16)
    vt = _dot_nt(_eye(kv_w), v_all)

    ones_rows = jnp.ones((2 * SUBLANES, n_keys), F32)
    kd, vext = [], []
    for g in range(SWA_KV_HEADS):
        kc_ = k_chunks[g // 2]
        kc_sw = pltpu.roll(kc_, hd, 1)
        kd.append((jnp.where(low, kc_, kc_sw) if g % 2 == 0 else jnp.where(low, kc_sw, kc_)).astype(BF16))
        vext.append(jnp.concatenate([vt[g * hd:(g + 1) * hd, :], ones_rows], axis=0).astype(BF16))

    ki = lax.broadcasted_iota(jnp.int32, (2 * tq, tq), 0)
    qi = lax.broadcasted_iota(jnp.int32, (2 * tq, tq), 1)
    rel = qi + tq - ki
    band = (rel >= 0) & (rel < SWA_WINDOW)
    masks = [band & ((ki >= tq) | (i > 0))] + [band] * (n_blk - 1)
    qtop = lax.broadcasted_iota(jnp.int32, (LANES, tq), 0) < hd
    qn = qn_ref[...]

    def where(it):
        blk, c = divmod(it, n_chunks)
        return blk, c, slice(blk * tq, (blk + 1) * tq), slice(c * LANES, (c + 1) * LANES)

    def st_load(it, st):
        _, _, rows, lanes = where(it)
        st["x"] = q_ref[rows, lanes]
        st["ms"] = mean_sq(st["x"])

    def st_query(it, st):
        _, _, rows, _ = where(it)
        xq = (_rope(st["x"] * lax.rsqrt(st["ms"] + NORM_EPS) * qn, rc_ref[rows, :], half) * c1).astype(BF16)
        st["qt"] = _dot_nt(eye, xq)

    def st_scores(it, st):
        blk, c, _, _ = where(it)
        g = (2 * c) // heads_per_group
        qt = st["qt"]
        q_pair = jnp.concatenate([jnp.where(qtop, qt, 0.0), jnp.where(qtop, 0.0, qt)], axis=1).astype(BF16)
        st["s"] = _dot(kd[g][blk * tq:(blk + 2) * tq, :], q_pair)

    def st_values(it, st):
        blk, c, _, _ = where(it)
        g = (2 * c) // heads_per_group
        es, corr = [], []
        for par in range(2):
            s = jnp.where(masks[blk], st["s"][:, par * tq:(par + 1) * tq], NEG_INF)
            sink = sink_ref[2 * c + par] * LOG2E
            mx = jnp.maximum(jnp.max(s, axis=0, keepdims=True), sink)
            es.append(jnp.exp2(s - mx).astype(BF16))
            corr.append(jnp.exp2(sink - mx))
        st["pv"] = _dot(vext[g][:, blk * tq:(blk + 2) * tq], jnp.concatenate(es, axis=1))
        st["corr"] = corr

    def st_merge(it, st):
        pv = st["pv"]
        parts = []
        for par in range(2):
            cols = slice(par * tq, (par + 1) * tq)
            parts.append(pv[0:hd, cols] * (1.0 / (pv[hd:hd + 1, cols] + st["corr"][par])))
        ot = jnp.concatenate(parts, axis=0).astype(BF16)
        st["o"] = _dot_nt(eye, ot)

    def st_store(it, st):
        _, _, rows, lanes = where(it)
        o_ref[rows, lanes] = st["o"].astype(o_ref.dtype)

    _software_pipeline(n_blk * n_chunks, (st_load, st_query, st_scores, st_values, st_merge, st_store), SWA_SKEW)


def _swa_attention(p, rope, q_norm2, k_norm2, sinks, batch, seq):
    m = p.shape[0]
    tq = SWA_TILE
    ts = SWA_BLOCKS * tq
    ns = seq // ts
    n_q = sinks.shape[0] * SWA_HEAD_DIM
    kv_w = 2 * SWA_KV_HEADS * SWA_HEAD_DIM
    kv_blk = n_q // kv_w

    def prev_blk(b, i):
        return (b * ns + i) * SWA_BLOCKS - jnp.minimum(i, 1)

    return pl.pallas_call(
        _swa_kernel,
        grid=(batch, ns),
        in_specs=[
            pl.BlockSpec(memory_space=pltpu.SMEM),
            pl.BlockSpec((ts, n_q), lambda b, i: (b * ns + i, 0)),
            pl.BlockSpec((ts, kv_w), lambda b, i: (b * ns + i, kv_blk)),
            pl.BlockSpec((tq, kv_w), lambda b, i: (prev_blk(b, i), kv_blk)),
            pl.BlockSpec((ts, 3 * LANES), lambda b, i: (b * ns + i, 0)),
            pl.BlockSpec((tq, 3 * LANES), lambda b, i: (prev_blk(b, i), 0)),
            pl.BlockSpec((1, LANES), lambda b, i: (0, 0)),
            pl.BlockSpec((1, LANES), lambda b, i: (0, 0)),
        ],
        out_specs=pl.BlockSpec((ts, n_q), lambda b, i: (b * ns + i, 0)),
        out_shape=jax.ShapeDtypeStruct((m, n_q), BF16),
        compiler_params=_params("parallel", "arbitrary"),
        name="swa_attention",
    )(sinks, p, p, p, rope, rope, q_norm2, k_norm2)


def _nsa_compress_kernel(kc_ref, vc_ref, ropec_ref, kn_ref, pek_ref, w1k_ref, w2k_ref,
                         pev_ref, w1v_ref, w2v_ref, kco_ref, vco_ref):
    half = NSA_HEAD_DIM // ROPE_FRACTION // 2
    n_chunk = kc_ref.shape[0] // CMP_STRIDE
    per = CMP_BLOCK // CMP_STRIDE

    def compress(t_ref, pe_ref, w1_ref, w2_ref):
        parts = []
        for h in range(per):
            acc = jnp.zeros((n_chunk, w1_ref.shape[2]), F32)
            for l in range(CMP_STRIDE):
                ll = h * CMP_STRIDE + l
                xl = t_ref[pl.ds(l, n_chunk, stride=CMP_STRIDE), :] + pe_ref[ll:ll + 1, :]
                acc = acc + _dot(xl.astype(BF16), w1_ref[ll])
            parts.append(acc)
        pre = parts[0]
        for h in range(1, per):
            pre = pre + pltpu.roll(parts[h], n_chunk - h, 0)
        hdn = pre * _sigmoid(pre)
        return _dot(hdn.astype(BF16), w2_ref[...])

    kcmp = compress(kc_ref, pek_ref, w1k_ref, w2k_ref)
    kco_ref[...] = _rope(_rms(kcmp, kn_ref[...]), ropec_ref[...], half).astype(BF16)
    vco_ref[...] = _transpose_bf16(compress(vc_ref, pev_ref, w1v_ref, w2v_ref).astype(BF16))


def _nsa_compress(p, rope_c, k_norm, pe_k, w1_k, w2_k, pe_v, w1_v, w2_v, batch, seq):
    d = NSA_HEAD_DIM
    g_n = NSA_KV_HEADS
    n_chunk = seq // CMP_STRIDE
    q_blocks = 4 * g_n

    def col(kind):
        return pl.BlockSpec((seq, d), lambda b, g, kind=kind: (b, q_blocks + kind * g_n + g))

    def full(a):
        return pl.BlockSpec(a.shape, lambda b, g, nd=a.ndim: (0,) * nd)

    return pl.pallas_call(
        _nsa_compress_kernel,
        grid=(batch, g_n),
        in_specs=[col(0), col(1),
                  pl.BlockSpec((n_chunk, 3 * LANES), lambda b, g: (b, 0)),
                  full(k_norm), full(pe_k), full(w1_k), full(w2_k), full(pe_v), full(w1_v), full(w2_v)],
        out_specs=[pl.BlockSpec((None, None, n_chunk, d), lambda b, g: (b, g, 0, 0)),
                   pl.BlockSpec((None, None, d, n_chunk), lambda b, g: (b, g, 0, 0))],
        out_shape=[jax.ShapeDtypeStruct((batch, g_n, n_chunk, d), BF16),
                   jax.ShapeDtypeStruct((batch, g_n, d, n_chunk), BF16)],
        compiler_params=_params("parallel", "arbitrary"),
        name="nsa_compress",
    )(p, p, rope_c, k_norm, pe_k, w1_k, w2_k, pe_v, w1_v, w2_v)


def _nsa_qkv_kernel(q_ref, ks_ref, vs_ref, kw_ref, vw_ref, rope_ref, ropet_ref, qnt_ref, kn_ref,
                    qt_ref, kso_ref, vst_ref, kwo_ref, vwt_ref):
    d = NSA_HEAD_DIM
    half = d // ROPE_FRACTION // 2
    rep = q_ref.shape[1] // d
    tq = NSA_TILE
    eye = _eye(d)
    for s in range(q_ref.shape[0] // tq):
        rows = slice(s * tq, (s + 1) * tq)
        cos = ropet_ref[0:half, rows]
        sin = ropet_ref[half:2 * half, rows]
        for r in range(rep):
            x = q_ref[rows, r * d:(r + 1) * d]
            hi = x.astype(BF16)
            rem = x - hi.astype(F32)
            mid = rem.astype(BF16)
            lo = (rem - mid.astype(F32)).astype(BF16)
            xt = _dot_nt(eye, hi) + _dot_nt(eye, mid) + _dot_nt(eye, lo)
            xn = xt * lax.rsqrt(jnp.mean(xt * xt, axis=0, keepdims=True) + NORM_EPS) * qnt_ref[...]
            x1, x2 = xn[0:half, :], xn[half:2 * half, :]
            rot = jnp.concatenate([x1 * cos - x2 * sin, x2 * cos + x1 * sin, xn[2 * half:, :]], axis=0)
            qt_ref[:, (s * rep + r) * tq:(s * rep + r + 1) * tq] = (rot * NSA_QSCALE).astype(BF16)
    kso_ref[...] = _rope(_rms(ks_ref[...], kn_ref[...]), rope_ref[...], half).astype(BF16)
    kwo_ref[...] = _rope(_rms(kw_ref[...], kn_ref[...]), rope_ref[...], half).astype(BF16)
    ones = jnp.ones((NSA_V_ROWS - d, vs_ref.shape[0]), BF16)
    vst_ref[0:d, :] = _dot_nt(eye, vs_ref[...].astype(BF16)).astype(BF16)
    vst_ref[d:NSA_V_ROWS, :] = ones
    vwt_ref[0:d, :] = _dot_nt(eye, vw_ref[...].astype(BF16)).astype(BF16)
    vwt_ref[d:NSA_V_ROWS, :] = ones


def _nsa_qkv(p, rope, rope_t, q_norm_t, k_norm, batch, seq):
    d = NSA_HEAD_DIM
    g_n = NSA_KV_HEADS
    rep = 4
    tt = NSA_PREP_TILE
    nt = seq // tt
    q_blocks = rep * g_n

    def col(kind):
        return pl.BlockSpec((tt, d), lambda b, g, t, kind=kind: (b * nt + t, q_blocks + kind * g_n + g))

    nat = pl.BlockSpec((None, None, tt, d), lambda b, g, t: (b, g, t, 0))
    tra = pl.BlockSpec((None, None, NSA_V_ROWS, tt), lambda b, g, t: (b, g, 0, t))
    sh_nat = jax.ShapeDtypeStruct((batch, g_n, seq, d), BF16)
    sh_tra = jax.ShapeDtypeStruct((batch, g_n, NSA_V_ROWS, seq), BF16)
    return pl.pallas_call(
        _nsa_qkv_kernel,
        grid=(batch, g_n, nt),
        in_specs=[pl.BlockSpec((tt, rep * d), lambda b, g, t: (b * nt + t, g)),
                  col(2), col(3), col(4), col(5),
                  pl.BlockSpec((tt, 3 * LANES), lambda b, g, t: (b * nt + t, 0)),
                  pl.BlockSpec((None, rope_t.shape[1], tt), lambda b, g, t: (b, 0, t)),
                  pl.BlockSpec(q_norm_t.shape, lambda b, g, t: (0, 0)),
                  pl.BlockSpec((1, d), lambda b, g, t: (0, 0))],
        out_specs=[pl.BlockSpec((None, None, d, rep * tt), lambda b, g, t: (b, g, 0, t)), nat, tra, nat, tra],
        out_shape=[jax.ShapeDtypeStruct((batch, g_n, d, rep * seq), BF16), sh_nat, sh_tra, sh_nat, sh_tra],
        compiler_params=_params("parallel", "parallel", "arbitrary"),
        name="nsa_qkv_prep",
    )(p, p, p, p, p, rope, rope_t, q_norm_t, k_norm)


def _nsa_gate_kernel(g_ref, o_ref):
    o_ref[...] = _sigmoid(g_ref[...]).T


def _nsa_gates(p, batch, seq):
    tt = NSA_PREP_TILE
    nt = seq // tt
    gate_blk = 10 * NSA_KV_HEADS
    return pl.pallas_call(
        _nsa_gate_kernel,
        grid=(batch, nt),
        in_specs=[pl.BlockSpec((tt, LANES), lambda b, t: (b * nt + t, gate_blk))],
        out_specs=pl.BlockSpec((None, LANES, tt), lambda b, t: (b, 0, t)),
        out_shape=jax.ShapeDtypeStruct((batch, LANES, seq), F32),
        compiler_params=_params("parallel", "arbitrary"),
        name="nsa_gates",
    )(p)


def _nsa_kernel(qt_ref, gt_ref, kc_ref, vct_ref, ks_ref, vst_ref, kw_ref, vwt_ref, ovl_ref, o_ref,
                sel_scr, score_scr, cnt_scr, base_scr, acc_scr, s0_scr, sw_scr, bias_scr, p_scr):
    gp = pl.program_id(1)
    i = pl.program_id(2)
    n_grp_step = qt_ref.shape[0]
    groups = range(n_grp_step)
    d = NSA_HEAD_DIM
    rows = qt_ref.shape[2]
    tq = o_ref.shape[0]
    rep = rows // tq
    n_slc = ovl_ref.shape[0]
    n_heads = NSA_KV_HEADS * rep
    n_pairs = rep // PAIR
    tk = SEL_KEY_TILE
    ch = SLC_BLOCK
    q0 = i * tq
    qts = [qt_ref[gg] for gg in groups]

    def head_cols(r):
        return slice(r * tq, (r + 1) * tq)

    def pair_cols(pr):
        return slice(pr * PAIR * tq, (pr + 1) * PAIR * tq)

    def gate_row(gg, branch, r):
        g = gp * n_grp_step + gg
        return gt_ref[pl.ds(branch * n_heads + g * rep + r, 1), :]

    def fold_max(x):
        parts = [x[j * SUBLANES:(j + 1) * SUBLANES, :] for j in range(x.shape[0] // SUBLANES)]
        while len(parts) > 1:
            parts = [jnp.maximum(parts[j], parts[j + 1]) for j in range(0, len(parts), 2)]
        return parts[0]

    def chunked_softmax(chunk, n_chunks, m_floor, gg, p_slot, p_cols):
        m8 = fold_max(chunk(0))
        for c in range(1, n_chunks):
            m8 = jnp.maximum(m8, fold_max(chunk(c)))
        m = jnp.max(m8, axis=0, keepdims=True)
        if m_floor is not None:
            m = jnp.maximum(m, m_floor)
        for c in range(n_chunks):
            p_scr[gg, p_slot, c * ch:(c + 1) * ch, p_cols] = jnp.exp2(chunk(c) - m).astype(BF16)
        return m

    wk = NSA_WINDOW + tq
    start = pl.multiple_of(jnp.maximum(q0 - NSA_WINDOW, 0), tq)

    @pl.when(i <= NSA_WINDOW // tq)
    def _():
        for c in range(wk // ch):
            t_w = q0 + lax.broadcasted_iota(jnp.int32, (ch, tq), 1)
            k_w = start + c * ch + lax.broadcasted_iota(jnp.int32, (ch, tq), 0)
            ok = (k_w <= t_w) & (t_w - k_w < NSA_WINDOW)
            bias_scr[c * ch:(c + 1) * ch, :] = jnp.where(ok, 0.0, NEG_INF)

    n_cmp = kc_ref.shape[1]
    t_c = q0 + lax.broadcasted_iota(jnp.int32, (n_cmp, tq), 1)
    end_c = lax.broadcasted_iota(jnp.int32, (n_cmp, tq), 0) * CMP_STRIDE + (CMP_BLOCK - 1)
    mask_c = end_c <= t_c

    def front(gg):
        qt = qts[gg]
        sw_scr[gg, 0] = _dot(kw_ref[gg, pl.ds(start, wk), :], qt[:, pair_cols(0)])
        sc_raw = _dot(kc_ref[gg], qt)
        for pr in range(1, n_pairs):
            sw_scr[gg, pr] = _dot(kw_ref[gg, pl.ds(start, wk), :], qt[:, pair_cols(pr)])
        s0_scr[gg] = _dot(ks_ref[gg, 0:tk, :], qt[:, pair_cols(0)])
        imp = jnp.zeros((n_slc, tq), F32)
        for pr in range(n_pairs):
            pcs = []
            for h in range(PAIR):
                cols = head_cols(pr * PAIR + h)
                chunked_softmax(lambda c, pr=pr, h=h: (sw_scr[gg, pr, c * ch:(c + 1) * ch, head_cols(h)]
                                                       + bias_scr[c * ch:(c + 1) * ch, :]),
                                wk // ch, None, gg, pr % 2, head_cols(h))
                sc = jnp.where(mask_c, sc_raw[:, cols], NEG_INF)
                ec = jnp.where(mask_c, jnp.exp2(sc - jnp.max(sc, axis=0, keepdims=True)), 0.0)
                pcs.append((ec / jnp.maximum(jnp.sum(ec, axis=0, keepdims=True), TINY)).astype(BF16))
            pc = jnp.concatenate(pcs, axis=1)
            o_win = _dot(vwt_ref[gg, :, pl.ds(start, wk)], p_scr[gg, pr % 2, 0:wk, :])
            o_cmp = _dot(vct_ref[gg], pc)
            imp_pair = _dot(ovl_ref[...], pc)
            for h in range(PAIR):
                r = pr * PAIR + h
                hc = head_cols(h)
                imp = imp + imp_pair[:, hc]
                base_scr[gg, :, head_cols(r)] = (gate_row(gg, 0, r) * o_cmp[:, hc]
                                                 + (gate_row(gg, 2, r) / o_win[d:d + 1, hc]) * o_win[0:d, hc])
        return imp

    imps = [front(gg) for gg in groups]

    jb = lax.broadcasted_iota(jnp.int32, (n_slc, tq), 0)
    tt = q0 + lax.broadcasted_iota(jnp.int32, (n_slc, tq), 1)
    cur = tt // SLC_BLOCK
    forced = (jb == 0) | (jb == cur) | (jb == cur - 1)
    for gg in groups:
        score_scr[gg] = jnp.where(forced, FORCE_SCORE, jnp.where(jb * SLC_BLOCK <= tt, imps[gg], NEG_INF))
    sub = SUBLANES
    n_grp = n_slc // sub
    cnt_scr[...] = jnp.zeros(cnt_scr.shape, F32)
    jsub = lax.broadcasted_iota(jnp.int32, (sub, tq), 0)
    for gi in range(n_grp):
        @pl.when(gi * sub * SLC_BLOCK <= q0 + tq - 1)
        def _(gi=gi):
            for gg in groups:
                blocks = [score_scr[gg, v * sub:(v + 1) * sub, :] for v in range(n_grp)]
                counts = [cnt_scr[gg, v * sub:(v + 1) * sub, :] for v in range(n_grp)]
                for ii in range(gi * sub, (gi + 1) * sub):
                    rowv = score_scr[gg, ii:ii + 1, :]
                    for v, blk in enumerate(blocks):
                        if (v + 1) * sub - 1 <= ii:
                            beat = jnp.where(rowv > blk, 1.0, 0.0)
                        elif v * sub > ii:
                            beat = jnp.where(rowv >= blk, 1.0, 0.0)
                        else:
                            beat = jnp.where(jsub + v * sub > ii, jnp.where(rowv >= blk, 1.0, 0.0),
                                             jnp.where(rowv > blk, 1.0, 0.0))
                        counts[v] = counts[v] + beat
                for v, cnt in enumerate(counts):
                    cnt_scr[gg, v * sub:(v + 1) * sub, :] = cnt
    n_top = min(SLC_TOP_N, n_slc)
    sel_scr[...] = jnp.where(cnt_scr[...] < n_top, 1.0, 0.0)

    acc_scr[...] = jnp.zeros(acc_scr.shape, F32)
    blk_per_tile = tk // ch
    t_b = q0 + lax.broadcasted_iota(jnp.int32, (ch, tq), 1)
    k_b = lax.broadcasted_iota(jnp.int32, (ch, tq), 0)
    n_full = q0 // tk

    def sel_step(gg, kt, m_prev, diagonal):
        qt = qts[gg]
        k0 = pl.multiple_of(kt * tk, tk)
        keys = ks_ref[gg, pl.ds(k0, tk), :]
        sel_bias = [(sel_scr[gg, pl.ds(kt * blk_per_tile + bb, 1), :] - 1.0) * (-NEG_INF)
                    for bb in range(blk_per_tile)]
        scores = {1: _dot(keys, qt[:, pair_cols(1)])}
        m_out = []
        for pr in range(n_pairs):
            alphas = []
            for h in range(PAIR):
                r = pr * PAIR + h

                def chunk(bb, pr=pr, h=h):
                    rows_bb = slice(bb * ch, (bb + 1) * ch)
                    x = (s0_scr[gg, rows_bb, head_cols(h)] if pr == 0 else scores[pr][rows_bb, head_cols(h)])
                    x = x + sel_bias[bb]
                    if diagonal:
                        x = jnp.where(k_b + (k0 + bb * ch) <= t_b, x, NEG_INF)
                    return x

                m_new = chunked_softmax(chunk, blk_per_tile, m_prev[r], gg, pr % 2, head_cols(h))
                alphas.append(jnp.exp2(m_prev[r] - m_new))
                m_out.append(m_new)
            pv = _dot(vst_ref[gg, :, pl.ds(k0, tk)], p_scr[gg, pr % 2, 0:tk, :])
            for h in range(PAIR):
                cols = head_cols(pr * PAIR + h)
                acc_scr[gg, :, cols] = alphas[h] * acc_scr[gg, :, cols] + pv[:, head_cols(h)]
            if pr + 2 < n_pairs:
                scores[pr + 2] = _dot(keys, qt[:, pair_cols(pr + 2)])
            elif pr + 2 == n_pairs and not diagonal:
                k1 = pl.multiple_of(k0 + tk, tk)
                s0_scr[gg] = _dot(ks_ref[gg, pl.ds(k1, tk), :], qt[:, pair_cols(0)])
        return tuple(m_out)

    def both_groups(kt, m_all, diagonal):
        return tuple(sel_step(gg, kt, m_all[gg], diagonal) for gg in groups)

    m_init = tuple(tuple(jnp.full((1, tq), NEG_INF, F32) for _ in range(rep)) for _ in groups)
    m_run = lax.fori_loop(0, n_full, lambda kt, m: both_groups(kt, m, False), m_init)
    both_groups(n_full, m_run, True)

    for gg in groups:
        for r in range(rep):
            cols = head_cols(r)
            out_t = (base_scr[gg, :, cols]
                     + (gate_row(gg, 1, r) / acc_scr[gg, d:d + 1, cols]) * acc_scr[gg, 0:d, cols])
            o_ref[:, (gg * rep + r) * d:(gg * rep + r + 1) * d] = (
                _transpose_bf16(out_t.astype(BF16)).astype(o_ref.dtype))


def _nsa_attend(qt, gt, kc, vct, ks, vst, kw, vwt, ovl_t, batch, seq):
    d = NSA_HEAD_DIM
    g_n = NSA_KV_HEADS
    gs = NSA_GROUPS
    tq = NSA_TILE
    nq = seq // tq
    rep = qt.shape[3] // seq
    n_chunk = kc.shape[2]
    n_slc = ovl_t.shape[0]
    gate_rows = -(-3 * g_n * rep // SUBLANES) * SUBLANES
    wk = NSA_WINDOW + tq

    def kv(shape):
        return pl.BlockSpec((None, gs) + shape, lambda b, g, i: (b, g, 0, 0))

    return pl.pallas_call(
        _nsa_kernel,
        grid=(batch, g_n // gs, nq),
        in_specs=[
            pl.BlockSpec((None, gs, d, rep * tq), lambda b, g, i: (b, g, 0, i)),
            pl.BlockSpec((None, gate_rows, tq), lambda b, g, i: (b, 0, i)),
            kv((n_chunk, d)), kv((d, n_chunk)),
            kv((seq, d)), kv((NSA_V_ROWS, seq)), kv((seq, d)), kv((NSA_V_ROWS, seq)),
            pl.BlockSpec(ovl_t.shape, lambda b, g, i: (0, 0)),
        ],
        out_specs=pl.BlockSpec((tq, gs * rep * d), lambda b, g, i: (b * nq + i, g)),
        out_shape=jax.ShapeDtypeStruct((batch * seq, g_n * rep * d), BF16),
        scratch_shapes=[
            pltpu.VMEM((gs, n_slc, tq), F32),
            pltpu.VMEM((gs, n_slc, tq), F32),
            pltpu.VMEM((gs, n_slc, tq), F32),
            pltpu.VMEM((gs, d, rep * tq), F32),
            pltpu.VMEM((gs, NSA_V_ROWS, rep * tq), F32),
            pltpu.VMEM((gs, SEL_KEY_TILE, PAIR * tq), F32),
            pltpu.VMEM((gs, rep // PAIR, wk, PAIR * tq), F32),
            pltpu.VMEM((wk, tq), F32),
            pltpu.VMEM((gs, 2, max(wk, SEL_KEY_TILE), PAIR * tq), BF16),
        ],
        compiler_params=_params("parallel", "parallel", "arbitrary"),
        name="nsa_attention",
    )(qt, gt, kc, vct, ks, vst, kw, vwt, ovl_t)


def _rope_table(pos, head_dim):
    rd = head_dim // ROPE_FRACTION
    half = rd // 2
    inv = 1.0 / (ROPE_THETA ** (jnp.arange(half, dtype=F32) * (2.0 / rd)))
    ang = pos.astype(F32)[:, None] * inv
    cos, sin = jnp.cos(ang), jnp.sin(ang)
    n = pos.shape[0]
    ones = jnp.ones((n, head_dim - rd), F32)
    zeros = jnp.zeros((n, head_dim - rd), F32)
    zh = jnp.zeros((n, half), F32)
    c = jnp.concatenate([cos, cos, ones], axis=1)
    s1 = jnp.concatenate([-sin, zh, zeros], axis=1)
    s2 = jnp.concatenate([zh, sin, zeros], axis=1)
    reps = LANES // head_dim
    return jnp.concatenate([jnp.tile(c, (1, reps)), jnp.tile(s1, (1, reps)), jnp.tile(s2, (1, reps))], axis=1)


def _rope_table_t(positions, head_dim):
    rd = head_dim // ROPE_FRACTION
    half = rd // 2
    inv = 1.0 / (ROPE_THETA ** (jnp.arange(half, dtype=F32) * (2.0 / rd)))
    ang = positions.astype(F32)[:, None, :] * inv[None, :, None]
    return jnp.concatenate([jnp.cos(ang), jnp.sin(ang)], axis=1)


def _overlap_matrix(seq):
    n_cmp_pad = seq // CMP_STRIDE
    n_slc = seq // SLC_BLOCK
    cs = np.arange(n_cmp_pad)[None, :] * CMP_STRIDE
    ss = np.arange(n_slc)[:, None] * SLC_BLOCK
    ovl_t = ((cs < ss + SLC_BLOCK) & (cs + CMP_BLOCK > ss)).astype(np.float32)
    return jnp.asarray(ovl_t, BF16)


def kernel(x, positions, ffn1_norm, ffn1_w_in, ffn1_w_out, mix_norm, ffn2_norm, ffn2_w_in, ffn2_w_out,
           swa_w_in, swa_q_norm, swa_k_norm, swa_sinks, swa_w_out,
           nsa_w_in, nsa_q_norm, nsa_k_norm, nsa_cmp_pe_k, nsa_cmp_w1_k, nsa_cmp_w2_k,
           nsa_cmp_pe_v, nsa_cmp_w1_v, nsa_cmp_w2_v, nsa_w_out):
    batch, seq, d_model = x.shape
    depth = ffn1_norm.shape[0]
    m = batch * seq
    h = x.reshape(m, d_model)

    f1_in, f1_out = ffn1_w_in.astype(BF16), ffn1_w_out.astype(BF16)
    f2_in, f2_out = ffn2_w_in.astype(BF16), ffn2_w_out.astype(BF16)
    swa_in, swa_out = swa_w_in.astype(BF16), swa_w_out.astype(BF16)
    nsa_n = nsa_w_in.shape[2]
    nsa_tn = NSA_PROJ_TILE
    nsa_pad = -nsa_n % nsa_tn
    nsa_in = jnp.pad(nsa_w_in.astype(BF16), ((0, 0), (0, 0), (0, nsa_pad)))
    nsa_out = nsa_w_out.astype(BF16)
    w1_k, w2_k = nsa_cmp_w1_k.astype(BF16), nsa_cmp_w2_k.astype(BF16)
    w1_v, w2_v = nsa_cmp_w1_v.astype(BF16), nsa_cmp_w2_v.astype(BF16)

    n1 = ffn1_norm.reshape(depth, 1, d_model)
    nm = mix_norm.reshape(depth, 1, d_model)
    n2 = ffn2_norm.reshape(depth, 1, d_model)

    pos_flat = positions.reshape(m)
    rope_swa = _rope_table(pos_flat, SWA_HEAD_DIM)
    rope_nsa = _rope_table(pos_flat, NSA_HEAD_DIM)
    rope_nsa_t = _rope_table_t(positions, NSA_HEAD_DIM)
    n_chunk = seq // CMP_STRIDE
    end_idx = jnp.minimum(jnp.arange(n_chunk) * CMP_STRIDE + (CMP_BLOCK - 1), seq - 1)
    rope_cmp = _rope_table(positions[:, end_idx].reshape(batch * n_chunk), NSA_HEAD_DIM)
    ovl_t = _overlap_matrix(seq)

    for i in range(depth):
        h = _ffn(h, n1, f1_in, f1_out, i)
        j = i // N_MIXERS
        if i % N_MIXERS == 0:
            p = _proj(h, nm, swa_in, i, j, SWA_PROJ_TILE)
            qn2 = jnp.tile(swa_q_norm[j], LANES // SWA_HEAD_DIM).reshape(1, LANES)
            kn2 = jnp.tile(swa_k_norm[j], LANES // SWA_HEAD_DIM).reshape(1, LANES)
            o = _swa_attention(p, rope_swa, qn2, kn2, swa_sinks[j], batch, seq)
            h = _oproj(h, o, swa_out, j)
        else:
            p = _proj(h, nm, nsa_in, i, j, nsa_tn)
            kn = nsa_k_norm[j].reshape(1, NSA_HEAD_DIM)
            kc, vct = _nsa_compress(p, rope_cmp, kn, nsa_cmp_pe_k[j], w1_k[j], w2_k[j],
                                    nsa_cmp_pe_v[j], w1_v[j], w2_v[j], batch, seq)
            qn_t = jnp.broadcast_to(nsa_q_norm[j][:, None], (NSA_HEAD_DIM, NSA_TILE))
            qt, ks, vst, kw, vwt = _nsa_qkv(p, rope_nsa, rope_nsa_t, qn_t, kn, batch, seq)
            gt = _nsa_gates(p, batch, seq)
            o = _nsa_attend(qt, gt, kc, vct, ks, vst, kw, vwt, ovl_t, batch, seq)
            h = _oproj(h, o, nsa_out, j)
        h = _ffn(h, n2, f2_in, f2_out, i)
    return h.reshape(batch, seq, d_model)
```
